```python
import math
import jax
import jax.numpy as jnp
from jax import lax
import numpy as np

D_MODEL = 1024
BATCH = 8
SEQ = 4096
DEPTH = 1

A_DK = 128
A_DV = 128
A_HEADS = D_MODEL // (2 * A_DV)
CONV_K = 4
B_DK = 128
B_DV = 128
B_HEADS = D_MODEL // (2 * B_DV)
ROPE_BASE = 10000.0
CHUNK = 64
N_GROUPS = 8
EXPERTS_PER_GROUP = 8
N_EXPERTS = N_GROUPS * EXPERTS_PER_GROUP
TOP_K = 2
D_EXPERT = 256
EPS = 1e-6

A_QK = A_HEADS * A_DK
A_V = A_HEADS * A_DV
A_CONV = 2 * A_QK + A_V
B_QK = B_HEADS * B_DK
B_V = B_HEADS * B_DV
D_MIX = A_V + B_V
IN_SIZES = (A_QK, A_QK, A_V, A_V, A_HEADS, A_HEADS, B_QK, B_QK, B_V, B_V)
D_IN = sum(IN_SIZES)

kernel_name = "hybrid_deltanet_retention_hmoe"


def rms_norm(x, w):
    xf = x.astype(jnp.float32)
    y = xf * lax.rsqrt(jnp.mean(xf * xf, axis=-1, keepdims=True) + EPS)
    return (y * w.astype(jnp.float32)).astype(x.dtype)


def l2norm(t):
    return t * lax.rsqrt(jnp.sum(t * t, axis=-1, keepdims=True) + EPS)


def to_heads(t, n_heads):
    b, s, _ = t.shape
    return t.reshape(b, s, n_heads, -1).transpose(0, 2, 1, 3)


def to_chunks(t):
    b, h, s = t.shape[:3]
    return t.reshape(b, h, s // CHUNK, CHUNK, *t.shape[3:])


def from_chunks(o):
    n, b, h, c, d = o.shape
    return jnp.moveaxis(o, 0, 2).reshape(b, h, n * c, d).transpose(0, 2, 1, 3)


def causal_conv(x, w):
    return lax.conv_general_dilated(
        x, w[:, None, :], window_strides=(1,), padding=[(CONV_K - 1, 0)],
        dimension_numbers=("NWC", "WIO", "NWC"), feature_group_count=x.shape[-1])


def rope(t, cos, sin):
    half = t.shape[-1] // 2
    t1, t2 = t[..., :half], t[..., half:]
    return jnp.concatenate([t1 * cos - t2 * sin, t1 * sin + t2 * cos], axis=-1)


def gated_delta_net(qkv, gate, beta_in, alpha_in, conv_w, a_log, dt_bias, norm_w):
    b, s, _ = qkv.shape
    qkv = jax.nn.silu(causal_conv(qkv, conv_w.astype(jnp.float32)))
    q, k, v = jnp.split(qkv, [A_QK, 2 * A_QK], axis=-1)
    q = l2norm(to_heads(q, A_HEADS)) * (A_DK ** -0.5)
    k = l2norm(to_heads(k, A_HEADS))
    v = to_heads(v, A_HEADS)
    beta = jax.nn.sigmoid(beta_in).transpose(0, 2, 1)
    g = (-jnp.exp(a_log.astype(jnp.float32))
         * jax.nn.softplus(alpha_in + dt_bias.astype(jnp.float32))).transpose(0, 2, 1)
    q, k, v, beta, g = (to_chunks(t) for t in (q, k, v, beta, g))
    gc = jnp.cumsum(g, axis=-1)
    causal = jnp.tril(jnp.ones((CHUNK, CHUNK), dtype=bool))
    strict = jnp.tril(jnp.ones((CHUNK, CHUNK), dtype=bool), -1)
    diff = gc[..., :, None] - gc[..., None, :]
    decay = jnp.where(causal, jnp.exp(jnp.where(causal, diff, 0.0)), 0.0)
    kb = k * beta[..., None]
    a_mat = jnp.where(strict, jnp.einsum("bhnid,bhnjd->bhnij", kb, k) * decay, 0.0)
    rhs = jnp.concatenate([v * beta[..., None], kb * jnp.exp(gc)[..., None]], axis=-1)
    uw = lax.linalg.triangular_solve(a_mat, rhs, left_side=True, lower=True, unit_diagonal=True)
    u, w = uw[..., :A_DV], uw[..., A_DV:]
    attn = jnp.einsum("bhnid,bhnjd->bhnij", q, k) * decay
    q_dec = q * jnp.exp(gc)[..., None]
    k_dec = k * jnp.exp(gc[..., -1:] - gc)[..., None]
    last = jnp.exp(gc[..., -1])

    def step(state, inp):
        q_c, k_c, u_c, w_c, attn_c, last_c = inp
        v_new = u_c - jnp.einsum("bhck,bhkv->bhcv", w_c, state)
        o_c = (jnp.einsum("bhck,bhkv->bhcv", q_c, state)
               + jnp.einsum("bhcj,bhjv->bhcv", attn_c, v_new))
        state = state * last_c[..., None, None] + jnp.einsum("bhck,bhcv->bhkv", k_c, v_new)
        return state, o_c

    xs = tuple(jnp.moveaxis(t, 2, 0) for t in (q_dec, k_dec, u, w, attn, last))
    s0 = jnp.zeros((b, A_HEADS, A_DK, A_DV), jnp.float32)
    _, o = lax.scan(step, s0, xs)
    o = from_chunks(o)
    o = rms_norm(o, norm_w) * jax.nn.silu(gate.reshape(b, s, A_HEADS, A_DV))
    return o.reshape(b, s, A_V)


def retention(q, k, v, gate, cos, sin, norm_w):
    b, s, _ = q.shape
    q = rope(to_heads(q, B_HEADS), cos, sin)
    k = rope(to_heads(k, B_HEADS), cos, sin) * (B_DK ** -0.5)
    v = to_heads(v, B_HEADS)
    log_gamma = jnp.log(1.0 - 2.0 ** (-5.0 - jnp.arange(B_HEADS, dtype=jnp.float32)))
    pos = jnp.arange(CHUNK, dtype=jnp.float32)
    rel = pos[:, None] - pos[None, :]
    decay = jnp.where(rel >= 0, jnp.exp(jnp.maximum(rel, 0.0) * log_gamma[:, None, None]), 0.0)
    q, k, v = (to_chunks(t) for t in (q, k, v))
    attn = jnp.einsum("bhnid,bhnjd->bhnij", q, k) * decay[None, :, None]
    inner = jnp.einsum("bhnij,bhnjv->bhniv", attn, v)
    q_dec = q * jnp.exp((pos + 1.0) * log_gamma[:, None])[None, :, None, :, None]
    k_dec = k * jnp.exp((CHUNK - 1.0 - pos) * log_gamma[:, None])[None, :, None, :, None]
    chunk_decay = jnp.exp(CHUNK * log_gamma)[None, :, None, None]

    def step(state, inp):
        q_c, k_c, v_c = inp
        o_c = jnp.einsum("bhck,bhkv->bhcv", q_c, state)
        state = state * chunk_decay + jnp.einsum("bhck,bhcv->bhkv", k_c, v_c)
        return state, o_c

    xs = tuple(jnp.moveaxis(t, 2, 0) for t in (q_dec, k_dec, v))
    s0 = jnp.zeros((b, B_HEADS, B_DK, B_DV), jnp.float32)
    _, cross = lax.scan(step, s0, xs)
    o = from_chunks(cross) + from_chunks(jnp.moveaxis(inner, 2, 0))
    o = rms_norm(o, norm_w.reshape(B_HEADS, B_DV)) * jax.nn.silu(gate.reshape(b, s, B_HEADS, B_DV))
    return o.reshape(b, s, B_V)


def hierarchical_moe(h, w_router_group, w_router_expert, w_gate, w_up, w_down):
    b, s, d = h.shape
    ht = h.reshape(b * s, d)
    hf = ht.astype(jnp.float32)
    g_logits = hf @ w_router_group.astype(jnp.float32)
    g_prob = jax.nn.softmax(g_logits, axis=-1)
    g_idx = jnp.argmax(g_logits, axis=-1)
    g_w = jnp.take_along_axis(g_prob, g_idx[:, None], axis=-1)[:, 0]
    e_logits = (hf @ w_router_expert.astype(jnp.float32)).reshape(-1, N_GROUPS, EXPERTS_PER_GROUP)
    e_logits = jnp.take_along_axis(e_logits, g_idx[:, None, None], axis=1)[:, 0]
    e_prob = jax.nn.softmax(e_logits, axis=-1)
    top_p, top_i = lax.top_k(e_prob, TOP_K)
    top_p = top_p / jnp.sum(top_p, axis=-1, keepdims=True)
    weights = (g_w[:, None] * top_p).reshape(-1)
    expert_id = (g_idx[:, None] * EXPERTS_PER_GROUP + top_i).reshape(-1).astype(jnp.int32)
    order = jnp.argsort(expert_id)
    tok = order // TOP_K
    xs = ht[tok]
    group_sizes = jnp.zeros((N_EXPERTS,), jnp.int32).at[expert_id].add(1)
    gate = lax.ragged_dot(xs, w_gate, group_sizes)
    up = lax.ragged_dot(xs, w_up, group_sizes)
    y = lax.ragged_dot((jax.nn.silu(gate) * up).astype(xs.dtype), w_down, group_sizes)
    y = y * weights[order].astype(y.dtype)[:, None]
    out = jnp.zeros_like(ht).at[tok].add(y)
    return out.reshape(b, s, d)


def setup_inputs(seed: int = 0) -> dict:
    key = jax.random.key(seed)
    ks = jax.random.split(key, 20)
    f32 = jnp.float32

    def nrm(k, shape, scale):
        return jax.random.normal(k, shape, f32) * scale

    def gain(k, shape):
        return 1.0 + 0.02 * jax.random.normal(k, shape, f32)

    x = nrm(ks[0], (BATCH, SEQ, D_MODEL), 1.0)
    attn_norm = gain(ks[1], (DEPTH, D_MODEL))
    w_in = nrm(ks[2], (DEPTH, D_MODEL, D_IN), D_MODEL ** -0.5)
    conv_a = nrm(ks[3], (DEPTH, CONV_K, A_CONV), CONV_K ** -0.5)
    a_log = jnp.log(jax.random.uniform(ks[4], (DEPTH, A_HEADS), f32, 1.0, 16.0))
    dt = jnp.exp(jax.random.uniform(ks[5], (DEPTH, A_HEADS), f32, math.log(1e-3), math.log(1e-1)))
    dt_bias = dt + jnp.log(-jnp.expm1(-dt))
    norm_a = gain(ks[6], (DEPTH, A_DV))
    norm_b = gain(ks[7], (DEPTH, B_V))
    w_out = nrm(ks[8], (DEPTH, D_MIX, D_MODEL), D_MIX ** -0.5)
    ffn_norm = gain(ks[9], (DEPTH, D_MODEL))
    w_router_group = nrm(ks[10], (DEPTH, D_MODEL, N_GROUPS), D_MODEL ** -0.5)
    w_router_expert = nrm(ks[11], (DEPTH, D_MODEL, N_EXPERTS), D_MODEL ** -0.5)
    w_gate = nrm(ks[12], (DEPTH, N_EXPERTS, D_MODEL, D_EXPERT), D_MODEL ** -0.5)
    w_up = nrm(ks[13], (DEPTH, N_EXPERTS, D_MODEL, D_EXPERT), D_MODEL ** -0.5)
    w_down = nrm(ks[14], (DEPTH, N_EXPERTS, D_EXPERT, D_MODEL), D_EXPERT ** -0.5)
    final_norm = gain(ks[15], (D_MODEL,))
    return {"x": x, "attn_norm": attn_norm, "w_in": w_in, "conv_a": conv_a,
            "a_log": a_log, "dt_bias": dt_bias, "norm_a": norm_a, "norm_b": norm_b,
            "w_out": w_out, "ffn_norm": ffn_norm, "w_router_group": w_router_group,
            "w_router_expert": w_router_expert, "w_gate": w_gate, "w_up": w_up,
            "w_down": w_down, "final_norm": final_norm}


def reference(x, attn_norm, w_in, conv_a, a_log, dt_bias, norm_a, norm_b, w_out, ffn_norm,
              w_router_group, w_router_expert, w_gate, w_up, w_down, final_norm):
    s = x.shape[1]
    pos = jnp.arange(s, dtype=jnp.float32)
    inv_freq = ROPE_BASE ** (-jnp.arange(0, B_DK, 2, dtype=jnp.float32) / B_DK)
    ang = pos[:, None] * inv_freq[None, :]
    cos, sin = jnp.cos(ang), jnp.sin(ang)
    offsets = [int(o) for o in np.cumsum(IN_SIZES)[:-1]]
    for l in range(DEPTH):
        h = rms_norm(x, attn_norm[l])
        proj = jnp.einsum("bsd,de->bse", h, w_in[l]).astype(jnp.float32)
        q_a, k_a, v_a, g_a, beta_a, alpha_a, q_b, k_b, v_b, g_b = jnp.split(proj, offsets, axis=-1)
        o_a = gated_delta_net(jnp.concatenate([q_a, k_a, v_a], axis=-1), g_a, beta_a, alpha_a,
                              conv_a[l], a_log[l], dt_bias[l], norm_a[l])
        o_b = retention(q_b, k_b, v_b, g_b, cos, sin, norm_b[l])
        mixed = jnp.concatenate([o_a, o_b], axis=-1).astype(x.dtype)
        x = x + jnp.einsum("bse,ed->bsd", mixed, w_out[l])
        h2 = rms_norm(x, ffn_norm[l])
        x = x + hierarchical_moe(h2, w_router_group[l], w_router_expert[l],
                                 w_gate[l], w_up[l], w_down[l])
    return rms_norm(x, final_norm)
```

```python
import functools
import math

import jax
import jax.numpy as jnp
from jax import lax
from jax.experimental import pallas as pl
from jax.experimental.pallas import tpu as pltpu

F32 = jnp.float32
BF16 = jnp.bfloat16
I32 = jnp.int32

D_MODEL = 1024
HEADS = 4
DH = 128
CONV_K = 4
CHUNK = 64
ROPE_BASE = 10000.0
N_GROUPS = 8
EPG = 8
N_EXPERTS = N_GROUPS * EPG
TOP_K = 2
D_EXPERT = 256
EPS = 1e-6
QK = HEADS * DH
N_MAIN = 8 * QK
LANES = 128
SUBLANES = 8

TM_PROJ = 512
TS_GDN = 512
TS_RET = 256
TM_MIX = 256
TT_DISPATCH = 512
TM_EXPERT = 256
TT_COMBINE = 256
VMEM_LIMIT = 48 * 1024 * 1024
_LOG_GAMMA = tuple(math.log(1.0 - 2.0 ** (-5.0 - hh)) for hh in range(HEADS))


def _sigmoid(x):
    return 1.0 / (1.0 + jnp.exp(-x))


def _softplus(x):
    return jnp.maximum(x, 0.0) + jnp.log1p(jnp.exp(-jnp.abs(x)))


def _split_bf16(a):
    hi = a.astype(BF16)
    lo = (a - hi.astype(F32)).astype(BF16)
    return hi, lo


def _dot(a, b):
    return jnp.dot(a, b, preferred_element_type=F32)


def _dot_nt(a, b):
    return lax.dot_general(a, b, (((1,), (1,)), ((), ())), preferred_element_type=F32)


def _dot_tn(a, b):
    return lax.dot_general(a, b, (((0,), (0,)), ((), ())), preferred_element_type=F32)


def _bmm(a, b):
    return jnp.stack([_dot(a[c], b[c]) for c in range(a.shape[0])])


def _bmm_nt(a, b):
    return jnp.stack([_dot_nt(a[c], b[c]) for c in range(a.shape[0])])


def _proj_kernel(x_ref, nw_ref, wm_ref, wsh_ref, wsl_ref, alog_ref, dtb_ref,
                 pm_ref, gcol_ref, grow_ref):
    tm = x_ref.shape[0]
    x = x_ref[...]
    h = x * lax.rsqrt(jnp.mean(x * x, axis=-1, keepdims=True) + EPS) * nw_ref[...]
    hb = h.astype(BF16)
    nc = 512
    for j in range(N_MAIN // nc):
        pm_ref[:, j * nc:(j + 1) * nc] = _dot(hb, wm_ref[:, j * nc:(j + 1) * nc]).astype(BF16)
    hl = (h - hb.astype(F32)).astype(BF16)
    sm = _dot(hb, wsh_ref[...]) + _dot(hb, wsl_ref[...]) + _dot(hl, wsh_ref[...])
    lane = lax.broadcasted_iota(I32, (tm, LANES), 1)
    row = lax.broadcasted_iota(I32, (tm, LANES), 0)
    beta = _sigmoid(sm)
    g = -jnp.exp(alog_ref[...]) * _softplus(sm + dtb_ref[...])
    rin = row & (CHUNK - 1)
    gc = g
    d = 1
    while d < CHUNK:
        gc = gc + jnp.where(rin >= d, pltpu.roll(gc, d, axis=0), 0.0)
        d *= 2
    out = jnp.where(lane < HEADS, beta, jnp.where(lane < 2 * HEADS, gc, 0.0))
    gcol_ref[...] = out
    tr = out.T
    for hh in range(HEADS):
        for c in range(tm // CHUNK):
            grow_ref[hh, c, 0:1, :] = tr[hh:hh + 1, c * CHUNK:(c + 1) * CHUNK]
            grow_ref[hh, c, 1:2, :] = tr[HEADS + hh:HEADS + hh + 1, c * CHUNK:(c + 1) * CHUNK]


def _proj(x2, attn_norm, w_main, ws_hi, ws_lo, alog_l, dtb_l):
    t = x2.shape[0]
    tm = TM_PROJ
    return pl.pallas_call(
        _proj_kernel,
        grid=(t // tm,),
        in_specs=[
            pl.BlockSpec((tm, D_MODEL), lambda i: (i, 0)),
            pl.BlockSpec((1, D_MODEL), lambda i: (0, 0)),
            pl.BlockSpec((D_MODEL, N_MAIN), lambda i: (0, 0)),
            pl.BlockSpec((D_MODEL, LANES), lambda i: (0, 0)),
            pl.BlockSpec((D_MODEL, LANES), lambda i: (0, 0)),
            pl.BlockSpec((1, LANES), lambda i: (0, 0)),
            pl.BlockSpec((1, LANES), lambda i: (0, 0)),
        ],
        out_specs=[
            pl.BlockSpec((tm, N_MAIN), lambda i: (i, 0)),
            pl.BlockSpec((tm, LANES), lambda i: (i, 0)),
            pl.BlockSpec((HEADS, tm // CHUNK, 2, CHUNK), lambda i: (0, i, 0, 0)),
        ],
        out_shape=[
            jax.ShapeDtypeStruct((t, N_MAIN), BF16),
            jax.ShapeDtypeStruct((t, LANES), F32),
            jax.ShapeDtypeStruct((HEADS, t // CHUNK, 2, CHUNK), F32),
        ],
        compiler_params=pltpu.CompilerParams(
            dimension_semantics=("arbitrary",), vmem_limit_bytes=VMEM_LIMIT),
        name="proj",
    )(x2, attn_norm, w_main, ws_hi, ws_lo, alog_l, dtb_l)


def _gdn_kernel(q_ref, k_ref, v_ref, gate_ref, cq_ref, ck_ref, cv_ref, gcol_ref, grow_ref, nw_ref,
                o_ref, tail_ref, ext_ref, st_ref):
    ts = q_ref.shape[0]
    nc = ts // CHUNK
    s = pl.program_id(1)
    h = pl.program_id(2)

    @pl.when(s == 0)
    def _init():
        tail_ref[h] = jnp.zeros((3, SUBLANES, DH), F32)
        st_ref[h] = jnp.zeros((DH, DH), F32)

    def conv_silu(x_ref, c_ref, idx):
        x = x_ref[...].astype(F32)
        ext_ref[idx, 0:SUBLANES, :] = tail_ref[h, idx]
        ext_ref[idx, SUBLANES:SUBLANES + ts, :] = x
        tail_ref[h, idx] = x[ts - SUBLANES:ts, :]
        w = c_ref[...]
        y = w[0:1, :] * ext_ref[idx, pl.ds(SUBLANES - 3, ts), :]
        for j in range(1, CONV_K):
            y = y + w[j:j + 1, :] * ext_ref[idx, pl.ds(SUBLANES - 3 + j, ts), :]
        return y * _sigmoid(y)

    q = conv_silu(q_ref, cq_ref, 0)
    k = conv_silu(k_ref, ck_ref, 1)
    v = conv_silu(v_ref, cv_ref, 2)
    qh = q * (lax.rsqrt(jnp.sum(q * q, axis=-1, keepdims=True) + EPS) * (DH ** -0.5))
    kh = k * lax.rsqrt(jnp.sum(k * k, axis=-1, keepdims=True) + EPS)

    lane = lax.broadcasted_iota(I32, (ts, LANES), 1)
    gblk = gcol_ref[...]
    beta_c = jnp.sum(jnp.where(lane == h, gblk, 0.0), axis=-1, keepdims=True)
    gc_c = jnp.sum(jnp.where(lane == h + HEADS, gblk, 0.0), axis=-1, keepdims=True)
    gr = grow_ref[...]
    beta_r = gr[:, 0:1, :]
    gc_r = gr[:, 1:2, :]

    q3 = qh.reshape(nc, CHUNK, DH)
    k3 = kh.reshape(nc, CHUNK, DH)
    v3 = v.reshape(nc, CHUNK, DH)
    bc = beta_c.reshape(nc, CHUNK, 1)
    gcc = gc_c.reshape(nc, CHUNK, 1)
    gl = gcc[:, CHUNK - 1:CHUNK, :]

    ii = lax.broadcasted_iota(I32, (nc, CHUNK, CHUNK), 1)
    jj = lax.broadcasted_iota(I32, (nc, CHUNK, CHUNK), 2)
    causal = ii >= jj
    strict = ii > jj
    dec = jnp.where(causal, jnp.exp(jnp.where(causal, gcc - gc_r, 0.0)), 0.0)

    k16 = k3.astype(BF16)
    q16 = q3.astype(BF16)
    v16 = v3.astype(BF16)
    gram = _bmm_nt(k16, k16)
    qk = _bmm_nt(q16, k16)
    a = jnp.where(strict, gram * dec * bc, 0.0)

    eye = (ii == jj).astype(F32)
    dm = eye - jnp.where((ii >> 1) == (jj >> 1), a, 0.0)
    sblk = 2
    while sblk < CHUNK:
        sh = sblk.bit_length() - 1
        off = jnp.where(((ii >> (sh + 1)) == (jj >> (sh + 1))) & ((ii >> sh) != (jj >> sh)), a, 0.0)
        d16 = dm.astype(BF16)
        t1 = _bmm(d16, off.astype(BF16))
        dm = dm - _bmm(t1.astype(BF16), d16)
        sblk *= 2

    tb = dm * beta_r
    tbe = tb * jnp.exp(gc_r)
    u = _bmm(tb.astype(BF16), v16)
    w = _bmm(tbe.astype(BF16), k16)
    attn = qk * dec
    wu = jnp.concatenate([w, u], axis=-1)
    wu16 = wu.astype(BF16)
    aw = _bmm(attn.astype(BF16), wu16)
    qe = q3 * jnp.exp(gcc) - aw[:, :, 0:DH]
    ol = aw[:, :, DH:2 * DH]
    kdec = k3 * jnp.exp(gl - gcc)

    trow = lax.broadcasted_iota(I32, (2 * CHUNK, DH), 0)
    xn = []
    for p in range(nc // 2):
        kd2 = kdec[2 * p:2 * p + 2].reshape(2 * CHUNK, DH)
        wu2 = wu16[2 * p:2 * p + 2].reshape(2 * CHUNK, 2 * DH)
        xn.append(_dot_tn(jnp.where(trow < CHUNK, kd2, 0.0).astype(BF16), wu2))
        xn.append(_dot_tn(jnp.where(trow >= CHUNK, kd2, 0.0).astype(BF16), wu2))

    st = st_ref[h]
    outs = []
    for c in range(nc):
        s16 = st.astype(BF16)
        outs.append(_dot(qe[c].astype(BF16), s16) + ol[c])
        st = jnp.exp(gl[c]) * st + (xn[c][:, DH:2 * DH] - _dot(xn[c][:, 0:DH].astype(BF16), s16))
    st_ref[h] = st
    o = jnp.concatenate(outs, axis=0)
    o = o * lax.rsqrt(jnp.mean(o * o, axis=-1, keepdims=True) + EPS) * nw_ref[...]
    gate = gate_ref[...].astype(F32)
    o_ref[...] = (o * (gate * _sigmoid(gate))).astype(o_ref.dtype)


def _gdn(pm, conv_a, gcol, grow, norm_a, batch, seq):
    ts = TS_GDN
    ns = seq // ts
    nc = ts // CHUNK
    t = batch * seq

    def col(cb):
        return pl.BlockSpec((ts, DH), lambda b, s, h, cb=cb: (b * ns + s, cb * HEADS + h))

    def ccol(cb):
        return pl.BlockSpec((CONV_K, DH), lambda b, s, h, cb=cb: (0, cb * HEADS + h))

    return pl.pallas_call(
        _gdn_kernel,
        grid=(batch, ns, HEADS),
        in_specs=[
            col(0), col(1), col(2), col(3),
            ccol(0), ccol(1), ccol(2),
            pl.BlockSpec((ts, LANES), lambda b, s, h: (b * ns + s, 0)),
            pl.BlockSpec((None, nc, 2, CHUNK), lambda b, s, h: (h, b * ns + s, 0, 0)),
            pl.BlockSpec((1, DH), lambda b, s, h: (0, 0)),
        ],
        out_specs=pl.BlockSpec((ts, DH), lambda b, s, h: (b * ns + s, h)),
        out_shape=jax.ShapeDtypeStruct((t, QK), BF16),
        scratch_shapes=[
            pltpu.VMEM((HEADS, 3, SUBLANES, DH), F32),
            pltpu.VMEM((3, ts + SUBLANES, DH), F32),
            pltpu.VMEM((HEADS, DH, DH), F32),
        ],
        compiler_params=pltpu.CompilerParams(
            dimension_semantics=("arbitrary", "arbitrary", "arbitrary"), vmem_limit_bytes=VMEM_LIMIT),
        name="gdn",
    )(pm, pm, pm, pm, conv_a, conv_a, conv_a, gcol, grow, norm_a)


def _ret_kernel(q_ref, k_ref, v_ref, gate_ref, cos_ref, sin_ref, nw_ref, o_ref, st_ref):
    c = q_ref.shape[0]
    s = pl.program_id(1)
    h = pl.program_id(2)

    @pl.when(s == 0)
    def _init():
        st_ref[h] = jnp.zeros((DH, DH), F32)

    cosf = cos_ref[...]
    sinf = sin_ref[...]

    def rope(x_ref):
        x = x_ref[...].astype(F32)
        return x * cosf + pltpu.roll(x, DH // 2, axis=1) * sinf

    q = rope(q_ref)
    k = rope(k_ref) * (DH ** -0.5)
    v16 = v_ref[...]
    lg = jnp.full((1, 1), _LOG_GAMMA[HEADS - 1], F32)
    for hh in range(HEADS - 1):
        lg = jnp.where(h == hh, _LOG_GAMMA[hh], lg)
    ii = lax.broadcasted_iota(I32, (c, c), 0)
    jj = lax.broadcasted_iota(I32, (c, c), 1)
    rel = (ii - jj).astype(F32)
    dec = jnp.where(rel >= 0, jnp.exp(jnp.maximum(rel, 0.0) * lg), 0.0)
    pos = lax.broadcasted_iota(I32, (c, 1), 0).astype(F32)
    q16 = q.astype(BF16)
    k16 = k.astype(BF16)
    attn = _dot_nt(q16, k16) * dec
    inner = _dot(attn.astype(BF16), v16)
    st = st_ref[h]
    qd = q * jnp.exp((pos + 1.0) * lg)
    cross = _dot(qd.astype(BF16), st.astype(BF16))
    kd = k * jnp.exp((c - 1.0 - pos) * lg)
    st_ref[h] = st * jnp.exp(c * lg) + _dot_tn(kd.astype(BF16), v16)
    o = cross + inner
    o = o * lax.rsqrt(jnp.mean(o * o, axis=-1, keepdims=True) + EPS) * nw_ref[...]
    gate = gate_ref[...].astype(F32)
    o_ref[...] = (o * (gate * _sigmoid(gate))).astype(o_ref.dtype)


def _ret(pm, cosf, sinf, norm_b, batch, seq):
    ts = TS_RET
    ns = seq // ts
    t = batch * seq

    def col(cb):
        return pl.BlockSpec((ts, DH), lambda b, s, h, cb=cb: (b * ns + s, (4 + cb) * HEADS + h))

    return pl.pallas_call(
        _ret_kernel,
        grid=(batch, ns, HEADS),
        in_specs=[
            col(0), col(1), col(2), col(3),
            pl.BlockSpec((ts, DH), lambda b, s, h: (s, 0)),
            pl.BlockSpec((ts, DH), lambda b, s, h: (s, 0)),
            pl.BlockSpec((1, DH), lambda b, s, h: (0, h)),
        ],
        out_specs=pl.BlockSpec((ts, DH), lambda b, s, h: (b * ns + s, h)),
        out_shape=jax.ShapeDtypeStruct((t, QK), BF16),
        scratch_shapes=[pltpu.VMEM((HEADS, DH, DH), F32)],
        compiler_params=pltpu.CompilerParams(
            dimension_semantics=("arbitrary", "arbitrary", "arbitrary"), vmem_limit_bytes=VMEM_LIMIT),
        name="ret",
    )(pm, pm, pm, pm, cosf, sinf, norm_b)


def _mix_kernel(x_ref, oa_ref, ob_ref, wo_ref, nw_ref, wrh_ref, wrl_ref,
                x1_ref, h2_ref, ri_ref, wt_ref, cnt_ref, carry_ref):
    tm = x_ref.shape[0]
    i = pl.program_id(0)

    @pl.when(i == 0)
    def _init():
        carry_ref[...] = jnp.zeros((1, LANES), F32)

    x1 = x_ref[...] + _dot(oa_ref[...], wo_ref[0:QK, :]) + _dot(ob_ref[...], wo_ref[QK:2 * QK, :])
    x1_ref[...] = x1
    h2 = x1 * lax.rsqrt(jnp.mean(x1 * x1, axis=-1, keepdims=True) + EPS) * nw_ref[...]
    h2_ref[...] = h2
    hh, hl = _split_bf16(h2)
    lg = _dot(hh, wrh_ref[...]) + _dot(hh, wrl_ref[...]) + _dot(hl, wrh_ref[...])
    lane = lax.broadcasted_iota(I32, (tm, LANES), 1)
    big = jnp.int32(1 << 20)
    neg = jnp.float32(-jnp.inf)

    def argmax_first(vals, mask):
        mv = jnp.where(mask, vals, neg)
        m = jnp.max(mv, axis=-1, keepdims=True)
        idx = jnp.min(jnp.where(mask & (mv == m), lane, big), axis=-1, keepdims=True)
        return m, idx

    gmask = lane < N_GROUPS
    gm, gi = argmax_first(lg, gmask)
    gw = 1.0 / jnp.sum(jnp.where(gmask, jnp.exp(lg - gm), 0.0), axis=-1, keepdims=True)
    emask = (lane >= N_EXPERTS) & ((lane >> 3) == gi + N_EXPERTS // EPG)
    m1, i1 = argmax_first(lg, emask)
    m2, i2 = argmax_first(lg, emask & (lane != i1))
    e21 = jnp.exp(m2 - m1)
    w0 = gw / (1.0 + e21)
    w1 = gw * e21 / (1.0 + e21)
    e0 = i1 - N_EXPERTS
    e1 = i2 - N_EXPERTS

    oh = ((lane == e0) | (lane == e1)).astype(BF16)
    ri = lax.broadcasted_iota(I32, (tm, tm), 0)
    ci = lax.broadcasted_iota(I32, (tm, tm), 1)
    lstrict = (ri > ci).astype(BF16)
    pref = _dot(lstrict, oh) + carry_ref[...]
    r0 = jnp.sum(jnp.where(lane == e0, pref, 0.0), axis=-1, keepdims=True)
    r1 = jnp.sum(jnp.where(lane == e1, pref, 0.0), axis=-1, keepdims=True)
    carry = carry_ref[...] + jnp.sum(oh.astype(F32), axis=0, keepdims=True)
    carry_ref[...] = carry
    cnt_ref[...] = jnp.broadcast_to(carry, (SUBLANES, LANES))

    wt_ref[...] = jnp.where(lane == 0, w0, jnp.where(lane == 1, w1, 0.0))
    rmat = jnp.where(lane == 0, e0.astype(F32),
                     jnp.where(lane == 1, e1.astype(F32),
                               jnp.where(lane == 2, r0, jnp.where(lane == 3, r1, 0.0))))
    ri_ref[...] = rmat.T[0:SUBLANES, :].astype(I32)


def _mix(x2, oa, ob, w_out16, ffn_norm, wr_hi, wr_lo):
    t = x2.shape[0]
    tm = TM_MIX
    return pl.pallas_call(
        _mix_kernel,
        grid=(t // tm,),
        in_specs=[
            pl.BlockSpec((tm, D_MODEL), lambda i: (i, 0)),
            pl.BlockSpec((tm, QK), lambda i: (i, 0)),
            pl.BlockSpec((tm, QK), lambda i: (i, 0)),
            pl.BlockSpec((D_MODEL, D_MODEL), lambda i: (0, 0)),
            pl.BlockSpec((1, D_MODEL), lambda i: (0, 0)),
            pl.BlockSpec((D_MODEL, LANES), lambda i: (0, 0)),
            pl.BlockSpec((D_MODEL, LANES), lambda i: (0, 0)),
        ],
        out_specs=[
            pl.BlockSpec((tm, D_MODEL), lambda i: (i, 0)),
            pl.BlockSpec((tm, D_MODEL), lambda i: (i, 0)),
            pl.BlockSpec((SUBLANES, tm), lambda i: (0, i)),
            pl.BlockSpec((tm, LANES), lambda i: (i, 0)),
            pl.BlockSpec((SUBLANES, LANES), lambda i: (0, 0)),
        ],
        out_shape=[
            jax.ShapeDtypeStruct((t, D_MODEL), F32),
            jax.ShapeDtypeStruct((t, D_MODEL), F32),
            jax.ShapeDtypeStruct((SUBLANES, t), I32),
            jax.ShapeDtypeStruct((t, LANES), F32),
            jax.ShapeDtypeStruct((SUBLANES, LANES), F32),
        ],
        scratch_shapes=[pltpu.VMEM((1, LANES), F32)],
        compiler_params=pltpu.CompilerParams(
            dimension_semantics=("arbitrary",), vmem_limit_bytes=VMEM_LIMIT),
        name="mix",
    )(x2, oa, ob, w_out16, ffn_norm, wr_hi, wr_lo)


def _dispatch_kernel(e0_ref, e1_ref, r0_ref, r1_ref, off_ref, h2_ref, xs_in_ref, xs_ref, sem):
    del xs_in_ref
    tt = e0_ref.shape[0]
    base = pl.program_id(0) * tt

    def row_copy(t, p):
        return pltpu.make_async_copy(h2_ref.at[pl.ds(base + t, 1)], xs_ref.at[pl.ds(p, 1)], sem)

    def issue(t, carry):
        row_copy(t, off_ref[e0_ref[t]] + r0_ref[t]).start()
        row_copy(t, off_ref[e1_ref[t]] + r1_ref[t]).start()
        return carry

    lax.fori_loop(0, tt, issue, 0)

    def drain(t, carry):
        row_copy(0, 0).wait()
        row_copy(0, 0).wait()
        return carry

    lax.fori_loop(0, tt, drain, 0)


def _dispatch(e0, e1, r0, r1, off, h2, xs0):
    t = h2.shape[0]
    tt = TT_DISPATCH
    smem = functools.partial(pl.BlockSpec, memory_space=pltpu.SMEM)
    return pl.pallas_call(
        _dispatch_kernel,
        grid=(t // tt,),
        in_specs=[
            smem((tt,), lambda i: (i,)), smem((tt,), lambda i: (i,)),
            smem((tt,), lambda i: (i,)), smem((tt,), lambda i: (i,)),
            smem((LANES,), lambda i: (0,)),
            pl.BlockSpec(memory_space=pl.ANY),
            pl.BlockSpec(memory_space=pl.ANY),
        ],
        out_specs=pl.BlockSpec(memory_space=pl.ANY),
        out_shape=jax.ShapeDtypeStruct(xs0.shape, xs0.dtype),
        scratch_shapes=[pltpu.SemaphoreType.DMA(())],
        input_output_aliases={6: 0},
        compiler_params=pltpu.CompilerParams(
            dimension_semantics=("arbitrary",), has_side_effects=True),
        name="dispatch",
    )(e0, e1, r0, r1, off, h2, xs0)


def _expert_kernel(te_ref, nu_ref, xs_ref, wg_ref, wu_ref, wd_ref, y_ref):
    i = pl.program_id(0)

    @pl.when(i < nu_ref[0])
    def _compute():
        x16 = xs_ref[...].astype(BF16)
        g = _dot(x16, wg_ref[...].astype(BF16))
        u = _dot(x16, wu_ref[...].astype(BF16))
        hid = (g * _sigmoid(g)) * u
        y_ref[...] = _dot(hid.astype(BF16), wd_ref[...].astype(BF16))

    @pl.when(i >= nu_ref[0])
    def _unused():
        y_ref[...] = jnp.zeros(y_ref.shape, y_ref.dtype)


def _experts(tile_expert, n_used, xs, w_gate, w_up, w_down):
    p_pad = xs.shape[0]
    tm = TM_EXPERT
    nt = p_pad // tm

    def row(i, te, nu):
        return (jnp.minimum(i, nu[0] - 1), 0)

    def wsel(i, te, nu):
        return (te[jnp.minimum(i, nu[0] - 1)], 0, 0)

    grid_spec = pltpu.PrefetchScalarGridSpec(
        num_scalar_prefetch=2,
        grid=(nt,),
        in_specs=[
            pl.BlockSpec((tm, D_MODEL), row),
            pl.BlockSpec((None, D_MODEL, D_EXPERT), wsel),
            pl.BlockSpec((None, D_MODEL, D_EXPERT), wsel),
            pl.BlockSpec((None, D_EXPERT, D_MODEL), wsel),
        ],
        out_specs=pl.BlockSpec((tm, D_MODEL), lambda i, te, nu: (i, 0)),
    )
    return pl.pallas_call(
        _expert_kernel,
        grid_spec=grid_spec,
        out_shape=jax.ShapeDtypeStruct((p_pad, D_MODEL), F32),
        compiler_params=pltpu.CompilerParams(
            dimension_semantics=("arbitrary",), vmem_limit_bytes=VMEM_LIMIT),
        name="experts",
    )(tile_expert, n_used, xs, w_gate, w_up, w_down)


def _combine_kernel(e0_ref, e1_ref, r0_ref, r1_ref, off_ref, x1_ref, wt_ref, nw_ref, y_ref,
                    o_ref, buf_ref, sem):
    tt = x1_ref.shape[0]

    def row_copy(t, p, slot):
        return pltpu.make_async_copy(y_ref.at[pl.ds(p, 1)], buf_ref.at[slot, pl.ds(t, 1)], sem)

    def issue(t, carry):
        row_copy(t, off_ref[e0_ref[t]] + r0_ref[t], 0).start()
        row_copy(t, off_ref[e1_ref[t]] + r1_ref[t], 1).start()
        return carry

    lax.fori_loop(0, tt, issue, 0)

    def drain(t, carry):
        row_copy(0, 0, 0).wait()
        row_copy(0, 0, 1).wait()
        return carry

    lax.fori_loop(0, tt, drain, 0)

    wt = wt_ref[...]
    xo = x1_ref[...] + wt[:, 0:1] * buf_ref[0] + wt[:, 1:2] * buf_ref[1]
    o_ref[...] = xo * lax.rsqrt(jnp.mean(xo * xo, axis=-1, keepdims=True) + EPS) * nw_ref[...]


def _combine(e0, e1, r0, r1, off, x1, wt, final_norm, y):
    t = x1.shape[0]
    tt = TT_COMBINE
    smem = functools.partial(pl.BlockSpec, memory_space=pltpu.SMEM)
    return pl.pallas_call(
        _combine_kernel,
        grid=(t // tt,),
        in_specs=[
            smem((tt,), lambda i: (i,)), smem((tt,), lambda i: (i,)),
            smem((tt,), lambda i: (i,)), smem((tt,), lambda i: (i,)),
            smem((LANES,), lambda i: (0,)),
            pl.BlockSpec((tt, D_MODEL), lambda i: (i, 0)),
            pl.BlockSpec((tt, LANES), lambda i: (i, 0)),
            pl.BlockSpec((1, D_MODEL), lambda i: (0, 0)),
            pl.BlockSpec(memory_space=pl.ANY),
        ],
        out_specs=pl.BlockSpec((tt, D_MODEL), lambda i: (i, 0)),
        out_shape=jax.ShapeDtypeStruct((t, D_MODEL), F32),
        scratch_shapes=[pltpu.VMEM((2, tt, D_MODEL), F32), pltpu.SemaphoreType.DMA(())],
        compiler_params=pltpu.CompilerParams(
            dimension_semantics=("arbitrary",), vmem_limit_bytes=VMEM_LIMIT),
        name="combine",
    )(e0, e1, r0, r1, off, x1, wt, final_norm, y)


def _pad_lanes(a, start):
    rows, n = a.shape
    return jnp.pad(a, ((0, 0), (start, LANES - start - n)))


def kernel(x, attn_norm, w_in, conv_a, a_log, dt_bias, norm_a, norm_b, w_out, ffn_norm,
           w_router_group, w_router_expert, w_gate, w_up, w_down, final_norm):
    batch, seq, d = x.shape
    t = batch * seq
    x2 = x.reshape(t, d)
    depth = attn_norm.shape[0]
    assert depth == 1, "the closing RMSNorm is fused into the single layer's combine step"

    pos = jnp.arange(seq, dtype=F32)
    inv_freq = ROPE_BASE ** (-jnp.arange(0, DH, 2, dtype=F32) / DH)
    ang = pos[:, None] * inv_freq[None, :]
    cos, sin = jnp.cos(ang), jnp.sin(ang)
    cosf = jnp.concatenate([cos, cos], axis=-1)
    sinf = jnp.concatenate([-sin, sin], axis=-1)

    for l in range(depth):
        wl = w_in[l]
        w_main = jnp.concatenate([wl[:, 0:4 * QK], wl[:, 4 * QK + 2 * HEADS:]], axis=1).astype(BF16)
        ws_hi, ws_lo = _split_bf16(_pad_lanes(wl[:, 4 * QK:4 * QK + 2 * HEADS], 0))
        alog_l = _pad_lanes(a_log[l][None, :], HEADS)
        dtb_l = _pad_lanes(dt_bias[l][None, :], HEADS)
        pm, gcol, grow = _proj(x2, attn_norm[l][None, :], w_main, ws_hi, ws_lo, alog_l, dtb_l)

        oa = _gdn(pm, conv_a[l], gcol, grow, norm_a[l][None, :], batch, seq)
        ob = _ret(pm, cosf, sinf, norm_b[l][None, :], batch, seq)

        wr = jnp.concatenate(
            [_pad_lanes(w_router_group[l], 0)[:, 0:N_EXPERTS], w_router_expert[l]], axis=1)
        wr_hi, wr_lo = _split_bf16(wr)
        x1, h2, ri, wt, cnt = _mix(x2, oa, ob, w_out[l].astype(BF16), ffn_norm[l][None, :], wr_hi, wr_lo)

        tm = TM_EXPERT
        n_tiles = (t * TOP_K) // tm + N_EXPERTS
        counts = cnt[0, 0:N_EXPERTS].astype(I32)
        tiles_per = (counts + tm - 1) // tm
        tile_end = jnp.cumsum(tiles_per)
        off = jnp.pad((tile_end - tiles_per) * tm, (0, LANES - N_EXPERTS))
        tile_expert = jnp.minimum(
            jnp.searchsorted(tile_end, jnp.arange(n_tiles, dtype=I32), side="right"), N_EXPERTS - 1
        ).astype(I32)
        n_used = tile_end[N_EXPERTS - 1:N_EXPERTS].astype(I32)

        e0, e1, r0, r1 = ri[0], ri[1], ri[2], ri[3]
        xs = _dispatch(e0, e1, r0, r1, off, h2, jnp.zeros((n_tiles * tm, d), F32))
        y = _experts(tile_expert, n_used, xs, w_gate[l], w_up[l], w_down[l])
        x2 = _combine(e0, e1, r0, r1, off, x1, wt, final_norm[None, :], y)
    return x2.reshape(batch, seq, d)
```

```python
import functools
import math

import jax
import jax.numpy as jnp
from jax import lax
from jax.experimental import pallas as pl
from jax.experimental.pallas import tpu as pltpu

F32 = jnp.float32
BF16 = jnp.bfloat16
I32 = jnp.int32

D_MODEL = 1024
HEADS = 4
DH = 128
CONV_K = 4
CHUNK = 64
ROPE_BASE = 10000.0
N_GROUPS = 8
EPG = 8
N_EXPERTS = N_GROUPS * EPG
TOP_K = 2
D_EXPERT = 256
EPS = 1e-6
QK = HEADS * DH
N_MAIN = 8 * QK
LANES = 128
SUBLANES = 8

TM_PROJ = 512
TS_GDN = 512
TS_RET = 256
TM_MIX = 256
TT_DISPATCH = 512
TM_EXPERT = 256
TT_COMBINE = 256
VMEM_LIMIT = 48 * 1024 * 1024
_LOG_GAMMA = tuple(math.log(1.0 - 2.0 ** (-5.0 - hh)) for hh in range(HEADS))


def _sigmoid(x):
    return 1.0 / (1.0 + jnp.exp(-x))


def _softplus(x):
    return jnp.maximum(x, 0.0) + jnp.log1p(jnp.exp(-jnp.abs(x)))


def _split_bf16(a):
    hi = a.astype(BF16)
    lo = (a - hi.astype(F32)).astype(BF16)
    return hi, lo


def _dot(a, b):
    return jnp.dot(a, b, preferred_element_type=F32)


def _dot_nt(a, b):
    return lax.dot_general(a, b, (((1,), (1,)), ((), ())), preferred_element_type=F32)


def _dot_tn(a, b):
    return lax.dot_general(a, b, (((0,), (0,)), ((), ())), preferred_element_type=F32)


def _bmm(a, b):
    return jnp.stack([_dot(a[c], b[c]) for c in range(a.shape[0])])


def _bmm_nt(a, b):
    return jnp.stack([_dot_nt(a[c], b[c]) for c in range(a.shape[0])])


def _proj_kernel(x_ref, nw_ref, wm_ref, wsh_ref, wsl_ref, alog_ref, dtb_ref,
                 pm_ref, gcol_ref, grow_ref):
    tm = x_ref.shape[0]
    x = x_ref[...]
    h = x * lax.rsqrt(jnp.mean(x * x, axis=-1, keepdims=True) + EPS) * nw_ref[...]
    hb = h.astype(BF16)
    nc = 512
    for j in range(N_MAIN // nc):
        pm_ref[:, j * nc:(j + 1) * nc] = _dot(hb, wm_ref[:, j * nc:(j + 1) * nc]).astype(BF16)
    hl = (h - hb.astype(F32)).astype(BF16)
    sm = _dot(hb, wsh_ref[...]) + _dot(hb, wsl_ref[...]) + _dot(hl, wsh_ref[...])
    lane = lax.broadcasted_iota(I32, (tm, LANES), 1)
    row = lax.broadcasted_iota(I32, (tm, LANES), 0)
    beta = _sigmoid(sm)
    g = -jnp.exp(alog_ref[...]) * _softplus(sm + dtb_ref[...])
    rin = row & (CHUNK - 1)
    gc = g
    d = 1
    while d < CHUNK:
        gc = gc + jnp.where(rin >= d, pltpu.roll(gc, d, axis=0), 0.0)
        d *= 2
    out = jnp.where(lane < HEADS, beta, jnp.where(lane < 2 * HEADS, gc, 0.0))
    gcol_ref[...] = out
    tr = out.T
    for hh in range(HEADS):
        for c in range(tm // CHUNK):
            grow_ref[hh, c, 0:1, :] = tr[hh:hh + 1, c * CHUNK:(c + 1) * CHUNK]
            grow_ref[hh, c, 1:2, :] = tr[HEADS + hh:HEADS + hh + 1, c * CHUNK:(c + 1) * CHUNK]


def _proj(x2, attn_norm, w_main, ws_hi, ws_lo, alog_l, dtb_l):
    t = x2.shape[0]
    tm = TM_PROJ
    return pl.pallas_call(
        _proj_kernel,
        grid=(t // tm,),
        in_specs=[
            pl.BlockSpec((tm, D_MODEL), lambda i: (i, 0)),
            pl.BlockSpec((1, D_MODEL), lambda i: (0, 0)),
            pl.BlockSpec((D_MODEL, N_MAIN), lambda i: (0, 0)),
            pl.BlockSpec((D_MODEL, LANES), lambda i: (0, 0)),
            pl.BlockSpec((D_MODEL, LANES), lambda i: (0, 0)),
            pl.BlockSpec((1, LANES), lambda i: (0, 0)),
            pl.BlockSpec((1, LANES), lambda i: (0, 0)),
        ],
        out_specs=[
            pl.BlockSpec((tm, N_MAIN), lambda i: (i, 0)),
            pl.BlockSpec((tm, LANES), lambda i: (i, 0)),
            pl.BlockSpec((HEADS, tm // CHUNK, 2, CHUNK), lambda i: (0, i, 0, 0)),
        ],
        out_shape=[
            jax.ShapeDtypeStruct((t, N_MAIN), BF16),
            jax.ShapeDtypeStruct((t, LANES), F32),
            jax.ShapeDtypeStruct((HEADS, t // CHUNK, 2, CHUNK), F32),
        ],
        compiler_params=pltpu.CompilerParams(
            dimension_semantics=("arbitrary",), vmem_limit_bytes=VMEM_LIMIT),
        name="proj",
    )(x2, attn_norm, w_main, ws_hi, ws_lo, alog_l, dtb_l)


def _gdn_kernel(q_ref, k_ref, v_ref, gate_ref, cq_ref, ck_ref, cv_ref, gcol_ref, grow_ref, nw_ref,
                o_ref, tail_ref, ext_ref, st_ref):
    ts = q_ref.shape[0]
    nc = ts // CHUNK
    s = pl.program_id(1)
    h = pl.program_id(2)

    @pl.when(s == 0)
    def _init():
        tail_ref[h] = jnp.zeros((3, SUBLANES, DH), F32)
        st_ref[h] = jnp.zeros((DH, DH), F32)

    def conv_silu(x_ref, c_ref, idx):
        x = x_ref[...].astype(F32)
        ext_ref[idx, 0:SUBLANES, :] = tail_ref[h, idx]
        ext_ref[idx, SUBLANES:SUBLANES + ts, :] = x
        tail_ref[h, idx] = x[ts - SUBLANES:ts, :]
        w = c_ref[...]
        y = w[0:1, :] * ext_ref[idx, pl.ds(SUBLANES - 3, ts), :]
        for j in range(1, CONV_K):
            y = y + w[j:j + 1, :] * ext_ref[idx, pl.ds(SUBLANES - 3 + j, ts), :]
        return y * _sigmoid(y)

    q = conv_silu(q_ref, cq_ref, 0)
    k = conv_silu(k_ref, ck_ref, 1)
    v = conv_silu(v_ref, cv_ref, 2)
    qh = q * (lax.rsqrt(jnp.sum(q * q, axis=-1, keepdims=True) + EPS) * (DH ** -0.5))
    kh = k * lax.rsqrt(jnp.sum(k * k, axis=-1, keepdims=True) + EPS)

    lane = lax.broadcasted_iota(I32, (ts, LANES), 1)
    gblk = gcol_ref[...]
    beta_c = jnp.sum(jnp.where(lane == h, gblk, 0.0), axis=-1, keepdims=True)
    gc_c = jnp.sum(jnp.where(lane == h + HEADS, gblk, 0.0), axis=-1, keepdims=True)
    gr = grow_ref[...]
    beta_r = gr[:, 0:1, :]
    gc_r = gr[:, 1:2, :]

    q3 = qh.reshape(nc, CHUNK, DH)
    k3 = kh.reshape(nc, CHUNK, DH)
    v3 = v.reshape(nc, CHUNK, DH)
    bc = beta_c.reshape(nc, CHUNK, 1)
    gcc = gc_c.reshape(nc, CHUNK, 1)
    gl = gcc[:, CHUNK - 1:CHUNK, :]

    ii = lax.broadcasted_iota(I32, (nc, CHUNK, CHUNK), 1)
    jj = lax.broadcasted_iota(I32, (nc, CHUNK, CHUNK), 2)
    causal = ii >= jj
    strict = ii > jj
    dec = jnp.where(causal, jnp.exp(jnp.where(causal, gcc - gc_r, 0.0)), 0.0)

    k16 = k3.astype(BF16)
    q16 = q3.astype(BF16)
    v16 = v3.astype(BF16)
    gram = _bmm_nt(k16, k16)
    qk = _bmm_nt(q16, k16)
    a = jnp.where(strict, gram * dec * bc, 0.0)

    eye = (ii == jj).astype(F32)
    dm = eye - jnp.where((ii >> 1) == (jj >> 1), a, 0.0)
    sblk = 2
    while sblk < CHUNK:
        sh = sblk.bit_length() - 1
        off = jnp.where(((ii >> (sh + 1)) == (jj >> (sh + 1))) & ((ii >> sh) != (jj >> sh)), a, 0.0)
        d16 = dm.astype(BF16)
        t1 = _bmm(d16, off.astype(BF16))
        dm = dm - _bmm(t1.astype(BF16), d16)
        sblk *= 2

    tb = dm * beta_r
    tbe = tb * jnp.exp(gc_r)
    u = _bmm(tb.astype(BF16), v16)
    w = _bmm(tbe.astype(BF16), k16)
    attn = qk * dec
    wu = jnp.concatenate([w, u], axis=-1)
    wu16 = wu.astype(BF16)
    aw = _bmm(attn.astype(BF16), wu16)
    qe = q3 * jnp.exp(gcc) - aw[:, :, 0:DH]
    ol = aw[:, :, DH:2 * DH]
    kdec = k3 * jnp.exp(gl - gcc)

    trow = lax.broadcasted_iota(I32, (2 * CHUNK, DH), 0)
    xn = []
    for p in range(nc // 2):
        kd2 = kdec[2 * p:2 * p + 2].reshape(2 * CHUNK, DH)
        wu2 = wu16[2 * p:2 * p + 2].reshape(2 * CHUNK, 2 * DH)
        xn.append(_dot_tn(jnp.where(trow < CHUNK, kd2, 0.0).astype(BF16), wu2))
        xn.append(_dot_tn(jnp.where(trow >= CHUNK, kd2, 0.0).astype(BF16), wu2))

    st = st_ref[h]
    outs = []
    for c in range(nc):
        s16 = st.astype(BF16)
        outs.append(_dot(qe[c].astype(BF16), s16) + ol[c])
        st = jnp.exp(gl[c]) * st + (xn[c][:, DH:2 * DH] - _dot(xn[c][:, 0:DH].astype(BF16), s16))
    st_ref[h] = st
    o = jnp.concatenate(outs, axis=0)
    o = o * lax.rsqrt(jnp.mean(o * o, axis=-1, keepdims=True) + EPS) * nw_ref[...]
    gate = gate_ref[...].astype(F32)
    o_ref[...] = (o * (gate * _sigmoid(gate))).astype(o_ref.dtype)


def _gdn(pm, conv_a, gcol, grow, norm_a, batch, seq):
    ts = TS_GDN
    ns = seq // ts
    nc = ts // CHUNK
    t = batch * seq

    def col(cb):
        return pl.BlockSpec((ts, DH), lambda b, s, h, cb=cb: (b * ns + s, cb * HEADS + h))

    def ccol(cb):
        return pl.BlockSpec((CONV_K, DH), lambda b, s, h, cb=cb: (0, cb * HEADS + h))

    return pl.pallas_call(
        _gdn_kernel,
        grid=(batch, ns, HEADS),
        in_specs=[
            col(0), col(1), col(2), col(3),
            ccol(0), ccol(1), ccol(2),
            pl.BlockSpec((ts, LANES), lambda b, s, h: (b * ns + s, 0)),
            pl.BlockSpec((None, nc, 2, CHUNK), lambda b, s, h: (h, b * ns + s, 0, 0)),
            pl.BlockSpec((1, DH), lambda b, s, h: (0, 0)),
        ],
        out_specs=pl.BlockSpec((ts, DH), lambda b, s, h: (b * ns + s, h)),
        out_shape=jax.ShapeDtypeStruct((t, QK), BF16),
        scratch_shapes=[
            pltpu.VMEM((HEADS, 3, SUBLANES, DH), F32),
            pltpu.VMEM((3, ts + SUBLANES, DH), F32),
            pltpu.VMEM((HEADS, DH, DH), F32),
        ],
        compiler_params=pltpu.CompilerParams(
            dimension_semantics=("arbitrary", "arbitrary", "arbitrary"), vmem_limit_bytes=VMEM_LIMIT),
        name="gdn",
    )(pm, pm, pm, pm, conv_a, conv_a, conv_a, gcol, grow, norm_a)


def _ret_kernel(q_ref, k_ref, v_ref, gate_ref, cos_ref, sin_ref, nw_ref, o_ref, st_ref):
    c = q_ref.shape[0]
    s = pl.program_id(1)
    h = pl.program_id(2)

    @pl.when(s == 0)
    def _init():
        st_ref[h] = jnp.zeros((DH, DH), F32)

    cosf = cos_ref[...]
    sinf = sin_ref[...]

    def rope(x_ref):
        x = x_ref[...].astype(F32)
        return x * cosf + pltpu.roll(x, DH // 2, axis=1) * sinf

    q = rope(q_ref)
    k = rope(k_ref) * (DH ** -0.5)
    v16 = v_ref[...]
    lg = jnp.full((1, 1), _LOG_GAMMA[HEADS - 1], F32)
    for hh in range(HEADS - 1):
        lg = jnp.where(h == hh, _LOG_GAMMA[hh], lg)
    ii = lax.broadcasted_iota(I32, (c, c), 0)
    jj = lax.broadcasted_iota(I32, (c, c), 1)
    rel = (ii - jj).astype(F32)
    dec = jnp.where(rel >= 0, jnp.exp(jnp.maximum(rel, 0.0) * lg), 0.0)
    pos = lax.broadcasted_iota(I32, (c, 1), 0).astype(F32)
    q16 = q.astype(BF16)
    k16 = k.astype(BF16)
    attn = _dot_nt(q16, k16) * dec
    inner = _dot(attn.astype(BF16), v16)
    st = st_ref[h]
    qd = q * jnp.exp((pos + 1.0) * lg)
    cross = _dot(qd.astype(BF16), st.astype(BF16))
    kd = k * jnp.exp((c - 1.0 - pos) * lg)
    st_ref[h] = st * jnp.exp(c * lg) + _dot_tn(kd.astype(BF16), v16)
    o = cross + inner
    o = o * lax.rsqrt(jnp.mean(o * o, axis=-1, keepdims=True) + EPS) * nw_ref[...]
    gate = gate_ref[...].astype(F32)
    o_ref[...] = (o * (gate * _sigmoid(gate))).astype(o_ref.dtype)


def _ret(pm, cosf, sinf, norm_b, batch, seq):
    ts = TS_RET
    ns = seq // ts
    t = batch * seq

    def col(cb):
        return pl.BlockSpec((ts, DH), lambda b, s, h, cb=cb: (b * ns + s, (4 + cb) * HEADS + h))

    return pl.pallas_call(
        _ret_kernel,
        grid=(batch, ns, HEADS),
        in_specs=[
            col(0), col(1), col(2), col(3),
            pl.BlockSpec((ts, DH), lambda b, s, h: (s, 0)),
            pl.BlockSpec((ts, DH), lambda b, s, h: (s, 0)),
            pl.BlockSpec((1, DH), lambda b, s, h: (0, h)),
        ],
        out_specs=pl.BlockSpec((ts, DH), lambda b, s, h: (b * ns + s, h)),
        out_shape=jax.ShapeDtypeStruct((t, QK), BF16),
        scratch_shapes=[pltpu.VMEM((HEADS, DH, DH), F32)],
        compiler_params=pltpu.CompilerParams(
            dimension_semantics=("arbitrary", "arbitrary", "arbitrary"), vmem_limit_bytes=VMEM_LIMIT),
        name="ret",
    )(pm, pm, pm, pm, cosf, sinf, norm_b)


def _mix_kernel(x_ref, oa_ref, ob_ref, wo_ref, nw_ref, wrh_ref, wrl_ref,
                x1_ref, h2_ref, ri_ref, wt_ref, cnt_ref, carry_ref):
    tm = x_ref.shape[0]
    i = pl.program_id(0)

    @pl.when(i == 0)
    def _init():
        carry_ref[...] = jnp.zeros((1, LANES), F32)

    x1 = x_ref[...] + _dot(oa_ref[...], wo_ref[0:QK, :]) + _dot(ob_ref[...], wo_ref[QK:2 * QK, :])
    x1_ref[...] = x1
    h2 = x1 * lax.rsqrt(jnp.mean(x1 * x1, axis=-1, keepdims=True) + EPS) * nw_ref[...]
    h2_ref[...] = h2
    hh, hl = _split_bf16(h2)
    lg = _dot(hh, wrh_ref[...]) + _dot(hh, wrl_ref[...]) + _dot(hl, wrh_ref[...])
    lane = lax.broadcasted_iota(I32, (tm, LANES), 1)
    big = jnp.int32(1 << 20)
    neg = jnp.float32(-jnp.inf)

    def argmax_first(vals, mask):
        mv = jnp.where(mask, vals, neg)
        m = jnp.max(mv, axis=-1, keepdims=True)
        idx = jnp.min(jnp.where(mask & (mv == m), lane, big), axis=-1, keepdims=True)
        return m, idx

    gmask = lane < N_GROUPS
    gm, gi = argmax_first(lg, gmask)
    gw = 1.0 / jnp.sum(jnp.where(gmask, jnp.exp(lg - gm), 0.0), axis=-1, keepdims=True)
    emask = (lane >= N_EXPERTS) & ((lane >> 3) == gi + N_EXPERTS // EPG)
    m1, i1 = argmax_first(lg, emask)
    m2, i2 = argmax_first(lg, emask & (lane != i1))
    e21 = jnp.exp(m2 - m1)
    w0 = gw / (1.0 + e21)
    w1 = gw * e21 / (1.0 + e21)
    e0 = i1 - N_EXPERTS
    e1 = i2 - N_EXPERTS

    oh = ((lane == e0) | (lane == e1)).astype(BF16)
    ri = lax.broadcasted_iota(I32, (tm, tm), 0)
    ci = lax.broadcasted_iota(I32, (tm, tm), 1)
    lstrict = (ri > ci).astype(BF16)
    pref = _dot(lstrict, oh) + carry_ref[...]
    r0 = jnp.sum(jnp.where(lane == e0, pref, 0.0), axis=-1, keepdims=True)
    r1 = jnp.sum(jnp.where(lane == e1, pref, 0.0), axis=-1, keepdims=True)
    carry = carry_ref[...] + jnp.sum(oh.astype(F32), axis=0, keepdims=True)
    carry_ref[...] = carry
    cnt_ref[...] = jnp.broadcast_to(carry, (SUBLANES, LANES))

    wt_ref[...] = jnp.where(lane == 0, w0, jnp.where(lane == 1, w1, 0.0))
    rmat = jnp.where(lane == 0, e0.astype(F32),
                     jnp.where(lane == 1, e1.astype(F32),
                               jnp.where(lane == 2, r0, jnp.where(lane == 3, r1, 0.0))))
    ri_ref[...] = rmat.T[0:SUBLANES, :].astype(I32)


def _mix(x2, oa, ob, w_out16, ffn_norm, wr_hi, wr_lo):
    t = x2.shape[0]
    tm = TM_MIX
    return pl.pallas_call(
        _mix_kernel,
        grid=(t // tm,),
        in_specs=[
            pl.BlockSpec((tm, D_MODEL), lambda i: (i, 0)),
            pl.BlockSpec((tm, QK), lambda i: (i, 0)),
            pl.BlockSpec((tm, QK), lambda i: (i, 0)),
            pl.BlockSpec((D_MODEL, D_MODEL), lambda i: (0, 0)),
            pl.BlockSpec((1, D_MODEL), lambda i: (0, 0)),
            pl.BlockSpec((D_MODEL, LANES), lambda i: (0, 0)),
            pl.BlockSpec((D_MODEL, LANES), lambda i: (0, 0)),
        ],
        out_specs=[
            pl.BlockSpec((tm, D_MODEL), lambda i: (i, 0)),
            pl.BlockSpec((tm, D_MODEL), lambda i: (i, 0)),
            pl.BlockSpec((SUBLANES, tm), lambda i: (0, i)),
            pl.BlockSpec((tm, LANES), lambda i: (i, 0)),
            pl.BlockSpec((SUBLANES, LANES), lambda i: (0, 0)),
        ],
        out_shape=[
            jax.ShapeDtypeStruct((t, D_MODEL), F32),
            jax.ShapeDtypeStruct((t, D_MODEL), F32),
            jax.ShapeDtypeStruct((SUBLANES, t), I32),
            jax.ShapeDtypeStruct((t, LANES), F32),
            jax.ShapeDtypeStruct((SUBLANES, LANES), F32),
        ],
        scratch_shapes=[pltpu.VMEM((1, LANES), F32)],
        compiler_params=pltpu.CompilerParams(
            dimension_semantics=("arbitrary",), vmem_limit_bytes=VMEM_LIMIT),
        name="mix",
    )(x2, oa, ob, w_out16, ffn_norm, wr_hi, wr_lo)


def _dispatch_kernel(e0_ref, e1_ref, r0_ref, r1_ref, off_ref, h2_ref, xs_in_ref, xs_ref, sem):
    del xs_in_ref
    tt = e0_ref.shape[0]

    def row_copy(t, p):
        return pltpu.make_async_copy(h2_ref.at[pl.ds(t, 1)], xs_ref.at[pl.ds(p, 1)], sem)

    def issue(t, carry):
        row_copy(t, off_ref[e0_ref[t]] + r0_ref[t]).start()
        row_copy(t, off_ref[e1_ref[t]] + r1_ref[t]).start()
        return carry

    lax.fori_loop(0, tt, issue, 0)

    def drain(t, carry):
        row_copy(0, 0).wait()
        row_copy(0, 0).wait()
        return carry

    lax.fori_loop(0, tt, drain, 0)


def _dispatch(e0, e1, r0, r1, off, h2, xs0):
    t = h2.shape[0]
    tt = TT_DISPATCH
    smem = functools.partial(pl.BlockSpec, memory_space=pltpu.SMEM)
    return pl.pallas_call(
        _dispatch_kernel,
        grid=(t // tt,),
        in_specs=[
            smem((tt,), lambda i: (i,)), smem((tt,), lambda i: (i,)),
            smem((tt,), lambda i: (i,)), smem((tt,), lambda i: (i,)),
            smem((LANES,), lambda i: (0,)),
            pl.BlockSpec((tt, D_MODEL), lambda i: (i, 0)),
            pl.BlockSpec(memory_space=pl.ANY),
        ],
        out_specs=pl.BlockSpec(memory_space=pl.ANY),
        out_shape=jax.ShapeDtypeStruct(xs0.shape, xs0.dtype),
        scratch_shapes=[pltpu.SemaphoreType.DMA(())],
        input_output_aliases={6: 0},
        compiler_params=pltpu.CompilerParams(
            dimension_semantics=("arbitrary",), has_side_effects=True),
        name="dispatch",
    )(e0, e1, r0, r1, off, h2, xs0)


def _expert_kernel(te_ref, nu_ref, xs_ref, wg_ref, wu_ref, wd_ref, y_ref):
    i = pl.program_id(0)

    @pl.when(i < nu_ref[0])
    def _compute():
        x16 = xs_ref[...].astype(BF16)
        g = _dot(x16, wg_ref[...].astype(BF16))
        u = _dot(x16, wu_ref[...].astype(BF16))
        hid = (g * _sigmoid(g)) * u
        y_ref[...] = _dot(hid.astype(BF16), wd_ref[...].astype(BF16))

    @pl.when(i >= nu_ref[0])
    def _unused():
        y_ref[...] = jnp.zeros(y_ref.shape, y_ref.dtype)


def _experts(tile_expert, n_used, xs, w_gate, w_up, w_down):
    p_pad = xs.shape[0]
    tm = TM_EXPERT
    nt = p_pad // tm

    def row(i, te, nu):
        return (jnp.minimum(i, nu[0] - 1), 0)

    def wsel(i, te, nu):
        return (te[jnp.minimum(i, nu[0] - 1)], 0, 0)

    grid_spec = pltpu.PrefetchScalarGridSpec(
        num_scalar_prefetch=2,
        grid=(nt,),
        in_specs=[
            pl.BlockSpec((tm, D_MODEL), row),
            pl.BlockSpec((None, D_MODEL, D_EXPERT), wsel),
            pl.BlockSpec((None, D_MODEL, D_EXPERT), wsel),
            pl.BlockSpec((None, D_EXPERT, D_MODEL), wsel),
        ],
        out_specs=pl.BlockSpec((tm, D_MODEL), lambda i, te, nu: (i, 0)),
    )
    return pl.pallas_call(
        _expert_kernel,
        grid_spec=grid_spec,
        out_shape=jax.ShapeDtypeStruct((p_pad, D_MODEL), F32),
        compiler_params=pltpu.CompilerParams(
            dimension_semantics=("arbitrary",), vmem_limit_bytes=VMEM_LIMIT),
        name="experts",
    )(tile_expert, n_used, xs, w_gate, w_up, w_down)


def _combine_kernel(e0_ref, e1_ref, r0_ref, r1_ref, off_ref, x1_ref, wt_ref, nw_ref, y_ref,
                    o_ref, buf_ref, sem):
    tt = x1_ref.shape[0]

    def row_copy(t, p, slot):
        return pltpu.make_async_copy(y_ref.at[pl.ds(p, 1)], buf_ref.at[slot, pl.ds(t, 1)], sem)

    def issue(t, carry):
        row_copy(t, off_ref[e0_ref[t]] + r0_ref[t], 0).start()
        row_copy(t, off_ref[e1_ref[t]] + r1_ref[t], 1).start()
        return carry

    lax.fori_loop(0, tt, issue, 0)

    def drain(t, carry):
        row_copy(0, 0, 0).wait()
        row_copy(0, 0, 1).wait()
        return carry

    lax.fori_loop(0, tt, drain, 0)

    wt = wt_ref[...]
    xo = x1_ref[...] + wt[:, 0:1] * buf_ref[0] + wt[:, 1:2] * buf_ref[1]
    o_ref[...] = xo * lax.rsqrt(jnp.mean(xo * xo, axis=-1, keepdims=True) + EPS) * nw_ref[...]


def _combine(e0, e1, r0, r1, off, x1, wt, final_norm, y):
    t = x1.shape[0]
    tt = TT_COMBINE
    smem = functools.partial(pl.BlockSpec, memory_space=pltpu.SMEM)
    return pl.pallas_call(
        _combine_kernel,
        grid=(t // tt,),
        in_specs=[
            smem((tt,), lambda i: (i,)), smem((tt,), lambda i: (i,)),
            smem((tt,), lambda i: (i,)), smem((tt,), lambda i: (i,)),
            smem((LANES,), lambda i: (0,)),
            pl.BlockSpec((tt, D_MODEL), lambda i: (i, 0)),
            pl.BlockSpec((tt, LANES), lambda i: (i, 0)),
            pl.BlockSpec((1, D_MODEL), lambda i: (0, 0)),
            pl.BlockSpec(memory_space=pl.ANY),
        ],
        out_specs=pl.BlockSpec((tt, D_MODEL), lambda i: (i, 0)),
        out_shape=jax.ShapeDtypeStruct((t, D_MODEL), F32),
        scratch_shapes=[pltpu.VMEM((2, tt, D_MODEL), F32), pltpu.SemaphoreType.DMA(())],
        compiler_params=pltpu.CompilerParams(
            dimension_semantics=("arbitrary",), vmem_limit_bytes=VMEM_LIMIT),
        name="combine",
    )(e0, e1, r0, r1, off, x1, wt, final_norm, y)


def _pad_lanes(a, start):
    rows, n = a.shape
    return jnp.pad(a, ((0, 0), (start, LANES - start - n)))


def kernel(x, attn_norm, w_in, conv_a, a_log, dt_bias, norm_a, norm_b, w_out, ffn_norm,
           w_router_group, w_router_expert, w_gate, w_up, w_down, final_norm):
    batch, seq, d = x.shape
    t = batch * seq
    x2 = x.reshape(t, d)
    depth = attn_norm.shape[0]
    assert depth == 1, "the closing RMSNorm is fused into the single layer's combine step"

    pos = jnp.arange(seq, dtype=F32)
    inv_freq = ROPE_BASE ** (-jnp.arange(0, DH, 2, dtype=F32) / DH)
    ang = pos[:, None] * inv_freq[None, :]
    cos, sin = jnp.cos(ang), jnp.sin(ang)
    cosf = jnp.concatenate([cos, cos], axis=-1)
    sinf = jnp.concatenate([-sin, sin], axis=-1)

    for l in range(depth):
        wl = w_in[l]
        w_main = jnp.concatenate([wl[:, 0:4 * QK], wl[:, 4 * QK + 2 * HEADS:]], axis=1).astype(BF16)
        ws_hi, ws_lo = _split_bf16(_pad_lanes(wl[:, 4 * QK:4 * QK + 2 * HEADS], 0))
        alog_l = _pad_lanes(a_log[l][None, :], HEADS)
        dtb_l = _pad_lanes(dt_bias[l][None, :], HEADS)
        pm, gcol, grow = _proj(x2, attn_norm[l][None, :], w_main, ws_hi, ws_lo, alog_l, dtb_l)

        oa = _gdn(pm, conv_a[l], gcol, grow, norm_a[l][None, :], batch, seq)
        ob = _ret(pm, cosf, sinf, norm_b[l][None, :], batch, seq)

        wr = jnp.concatenate(
            [_pad_lanes(w_router_group[l], 0)[:, 0:N_EXPERTS], w_router_expert[l]], axis=1)
        wr_hi, wr_lo = _split_bf16(wr)
        x1, h2, ri, wt, cnt = _mix(x2, oa, ob, w_out[l].astype(BF16), ffn_norm[l][None, :], wr_hi, wr_lo)

        tm = TM_EXPERT
        n_tiles = (t * TOP_K) // tm + N_EXPERTS
        counts = cnt[0, 0:N_EXPERTS].astype(I32)
        tiles_per = (counts + tm - 1) // tm
        tile_end = jnp.cumsum(tiles_per)
        off = jnp.pad((tile_end - tiles_per) * tm, (0, LANES - N_EXPERTS))
        tile_expert = jnp.minimum(
            jnp.sum(tile_end[None, :] <= jnp.arange(n_tiles, dtype=I32)[:, None], axis=1), N_EXPERTS - 1
        ).astype(I32)
        n_used = tile_end[N_EXPERTS - 1:N_EXPERTS].astype(I32)

        e0, e1, r0, r1 = ri[0], ri[1], ri[2], ri[3]
        xs = _dispatch(e0, e1, r0, r1, off, h2, jnp.zeros((n_tiles * tm, d), F32))
        y = _experts(tile_expert, n_used, xs, w_gate[l], w_up[l], w_down[l])
        x2 = _combine(e0, e1, r0, r1, off, x1, wt, final_norm[None, :], y)
    return x2.reshape(batch, seq, d)
```

```python
import functools
import math

import jax
import jax.numpy as jnp
from jax import lax
from jax.experimental import pallas as pl
from jax.experimental.pallas import tpu as pltpu

F32 = jnp.float32
BF16 = jnp.bfloat16
I32 = jnp.int32

D_MODEL = 1024
HEADS = 4
DH = 128
CONV_K = 4
CHUNK = 64
ROPE_BASE = 10000.0
N_GROUPS = 8
EPG = 8
N_EXPERTS = N_GROUPS * EPG
TOP_K = 2
D_EXPERT = 256
EPS = 1e-6
QK = HEADS * DH
N_MAIN = 8 * QK
LANES = 128
SUBLANES = 8

TM_PROJ = 512
TS_GDN = 512
TS_RET = 256
TM_MIX = 256
TS_MOE = 4096
SUB_MOE = 512
CH_MOE = 128
EXPERTS_PER_STEP = 2
RANK_BITS = 16
RANK_RADIX = 1 << RANK_BITS
VMEM_LIMIT = 48 * 1024 * 1024
VMEM_LIMIT_MOE = 58 * 1024 * 1024
_LOG_GAMMA = tuple(math.log(1.0 - 2.0 ** (-5.0 - hh)) for hh in range(HEADS))


def _sigmoid(x):
    return 1.0 / (1.0 + jnp.exp(-x))


def _softplus(x):
    return jnp.maximum(x, 0.0) + jnp.log1p(jnp.exp(-jnp.abs(x)))


def _split_bf16(a):
    hi = a.astype(BF16)
    lo = (a - hi.astype(F32)).astype(BF16)
    return hi, lo


def _dot(a, b):
    return jnp.dot(a, b, preferred_element_type=F32)


def _dot_nt(a, b):
    return lax.dot_general(a, b, (((1,), (1,)), ((), ())), preferred_element_type=F32)


def _dot_tn(a, b):
    return lax.dot_general(a, b, (((0,), (0,)), ((), ())), preferred_element_type=F32)


def _bmm(a, b):
    return jnp.stack([_dot(a[c], b[c]) for c in range(a.shape[0])])


def _bmm_nt(a, b):
    return jnp.stack([_dot_nt(a[c], b[c]) for c in range(a.shape[0])])


def _proj_kernel(x_ref, nw_ref, wm_ref, wsh_ref, wsl_ref, alog_ref, dtb_ref,
                 pm_ref, gcol_ref, grow_ref):
    tm = x_ref.shape[0]
    x = x_ref[...]
    h = x * lax.rsqrt(jnp.mean(x * x, axis=-1, keepdims=True) + EPS) * nw_ref[...]
    hb = h.astype(BF16)
    nc = 512
    for j in range(N_MAIN // nc):
        pm_ref[:, j * nc:(j + 1) * nc] = _dot(hb, wm_ref[:, j * nc:(j + 1) * nc]).astype(BF16)
    hl = (h - hb.astype(F32)).astype(BF16)
    sm = _dot(hb, wsh_ref[...]) + _dot(hb, wsl_ref[...]) + _dot(hl, wsh_ref[...])
    lane = lax.broadcasted_iota(I32, (tm, LANES), 1)
    row = lax.broadcasted_iota(I32, (tm, LANES), 0)
    beta = _sigmoid(sm)
    g = -jnp.exp(alog_ref[...]) * _softplus(sm + dtb_ref[...])
    rin = row & (CHUNK - 1)
    gc = g
    d = 1
    while d < CHUNK:
        gc = gc + jnp.where(rin >= d, pltpu.roll(gc, d, axis=0), 0.0)
        d *= 2
    out = jnp.where(lane < HEADS, beta, jnp.where(lane < 2 * HEADS, gc, 0.0))
    gcol_ref[...] = out
    tr = out.T
    for hh in range(HEADS):
        for c in range(tm // CHUNK):
            grow_ref[hh, c, 0:1, :] = tr[hh:hh + 1, c * CHUNK:(c + 1) * CHUNK]
            grow_ref[hh, c, 1:2, :] = tr[HEADS + hh:HEADS + hh + 1, c * CHUNK:(c + 1) * CHUNK]


def _proj(x2, attn_norm, w_main, ws_hi, ws_lo, alog_l, dtb_l):
    t = x2.shape[0]
    tm = TM_PROJ
    return pl.pallas_call(
        _proj_kernel,
        grid=(t // tm,),
        in_specs=[
            pl.BlockSpec((tm, D_MODEL), lambda i: (i, 0)),
            pl.BlockSpec((1, D_MODEL), lambda i: (0, 0)),
            pl.BlockSpec((D_MODEL, N_MAIN), lambda i: (0, 0)),
            pl.BlockSpec((D_MODEL, LANES), lambda i: (0, 0)),
            pl.BlockSpec((D_MODEL, LANES), lambda i: (0, 0)),
            pl.BlockSpec((1, LANES), lambda i: (0, 0)),
            pl.BlockSpec((1, LANES), lambda i: (0, 0)),
        ],
        out_specs=[
            pl.BlockSpec((tm, N_MAIN), lambda i: (i, 0)),
            pl.BlockSpec((tm, LANES), lambda i: (i, 0)),
            pl.BlockSpec((HEADS, tm // CHUNK, 2, CHUNK), lambda i: (0, i, 0, 0)),
        ],
        out_shape=[
            jax.ShapeDtypeStruct((t, N_MAIN), BF16),
            jax.ShapeDtypeStruct((t, LANES), F32),
            jax.ShapeDtypeStruct((HEADS, t // CHUNK, 2, CHUNK), F32),
        ],
        compiler_params=pltpu.CompilerParams(
            dimension_semantics=("arbitrary",), vmem_limit_bytes=VMEM_LIMIT),
        name="proj",
    )(x2, attn_norm, w_main, ws_hi, ws_lo, alog_l, dtb_l)


def _gdn_kernel(q_ref, k_ref, v_ref, gate_ref, cq_ref, ck_ref, cv_ref, gcol_ref, grow_ref, nw_ref,
                o_ref, tail_ref, ext_ref, st_ref):
    ts = q_ref.shape[0]
    nc = ts // CHUNK
    s = pl.program_id(1)
    h = pl.program_id(2)

    @pl.when(s == 0)
    def _init():
        tail_ref[h] = jnp.zeros((3, SUBLANES, DH), F32)
        st_ref[h] = jnp.zeros((DH, DH), F32)

    def conv_silu(x_ref, c_ref, idx):
        x = x_ref[...].astype(F32)
        ext_ref[idx, 0:SUBLANES, :] = tail_ref[h, idx]
        ext_ref[idx, SUBLANES:SUBLANES + ts, :] = x
        tail_ref[h, idx] = x[ts - SUBLANES:ts, :]
        w = c_ref[...]
        y = w[0:1, :] * ext_ref[idx, pl.ds(SUBLANES - 3, ts), :]
        for j in range(1, CONV_K):
            y = y + w[j:j + 1, :] * ext_ref[idx, pl.ds(SUBLANES - 3 + j, ts), :]
        return y * _sigmoid(y)

    q = conv_silu(q_ref, cq_ref, 0)
    k = conv_silu(k_ref, ck_ref, 1)
    v = conv_silu(v_ref, cv_ref, 2)
    qh = q * (lax.rsqrt(jnp.sum(q * q, axis=-1, keepdims=True) + EPS) * (DH ** -0.5))
    kh = k * lax.rsqrt(jnp.sum(k * k, axis=-1, keepdims=True) + EPS)

    lane = lax.broadcasted_iota(I32, (ts, LANES), 1)
    gblk = gcol_ref[...]
    beta_c = jnp.sum(jnp.where(lane == h, gblk, 0.0), axis=-1, keepdims=True)
    gc_c = jnp.sum(jnp.where(lane == h + HEADS, gblk, 0.0), axis=-1, keepdims=True)
    gr = grow_ref[...]
    beta_r = gr[:, 0:1, :]
    gc_r = gr[:, 1:2, :]

    q3 = qh.reshape(nc, CHUNK, DH)
    k3 = kh.reshape(nc, CHUNK, DH)
    v3 = v.reshape(nc, CHUNK, DH)
    bc = beta_c.reshape(nc, CHUNK, 1)
    gcc = gc_c.reshape(nc, CHUNK, 1)
    gl = gcc[:, CHUNK - 1:CHUNK, :]

    ii = lax.broadcasted_iota(I32, (nc, CHUNK, CHUNK), 1)
    jj = lax.broadcasted_iota(I32, (nc, CHUNK, CHUNK), 2)
    causal = ii >= jj
    strict = ii > jj
    dec = jnp.where(causal, jnp.exp(jnp.where(causal, gcc - gc_r, 0.0)), 0.0)

    k16 = k3.astype(BF16)
    q16 = q3.astype(BF16)
    v16 = v3.astype(BF16)
    gram = _bmm_nt(k16, k16)
    qk = _bmm_nt(q16, k16)
    a = jnp.where(strict, gram * dec * bc, 0.0)

    eye = (ii == jj).astype(F32)
    dm = eye - jnp.where((ii >> 1) == (jj >> 1), a, 0.0)
    sblk = 2
    while sblk < CHUNK:
        sh = sblk.bit_length() - 1
        off = jnp.where(((ii >> (sh + 1)) == (jj >> (sh + 1))) & ((ii >> sh) != (jj >> sh)), a, 0.0)
        d16 = dm.astype(BF16)
        t1 = _bmm(d16, off.astype(BF16))
        dm = dm - _bmm(t1.astype(BF16), d16)
        sblk *= 2

    tb = dm * beta_r
    tbe = tb * jnp.exp(gc_r)
    u = _bmm(tb.astype(BF16), v16)
    w = _bmm(tbe.astype(BF16), k16)
    attn = qk * dec
    wu = jnp.concatenate([w, u], axis=-1)
    wu16 = wu.astype(BF16)
    aw = _bmm(attn.astype(BF16), wu16)
    qe = q3 * jnp.exp(gcc) - aw[:, :, 0:DH]
    ol = aw[:, :, DH:2 * DH]
    kdec = k3 * jnp.exp(gl - gcc)

    trow = lax.broadcasted_iota(I32, (2 * CHUNK, DH), 0)
    xn = []
    for p in range(nc // 2):
        kd2 = kdec[2 * p:2 * p + 2].reshape(2 * CHUNK, DH)
        wu2 = wu16[2 * p:2 * p + 2].reshape(2 * CHUNK, 2 * DH)
        xn.append(_dot_tn(jnp.where(trow < CHUNK, kd2, 0.0).astype(BF16), wu2))
        xn.append(_dot_tn(jnp.where(trow >= CHUNK, kd2, 0.0).astype(BF16), wu2))

    st = st_ref[h]
    outs = []
    for c in range(nc):
        s16 = st.astype(BF16)
        outs.append(_dot(qe[c].astype(BF16), s16) + ol[c])
        st = jnp.exp(gl[c]) * st + (xn[c][:, DH:2 * DH] - _dot(xn[c][:, 0:DH].astype(BF16), s16))
    st_ref[h] = st
    o = jnp.concatenate(outs, axis=0)
    o = o * lax.rsqrt(jnp.mean(o * o, axis=-1, keepdims=True) + EPS) * nw_ref[...]
    gate = gate_ref[...].astype(F32)
    o_ref[...] = (o * (gate * _sigmoid(gate))).astype(o_ref.dtype)


def _gdn(pm, conv_a, gcol, grow, norm_a, batch, seq):
    ts = TS_GDN
    ns = seq // ts
    nc = ts // CHUNK
    t = batch * seq

    def col(cb):
        return pl.BlockSpec((ts, DH), lambda b, s, h, cb=cb: (b * ns + s, cb * HEADS + h))

    def ccol(cb):
        return pl.BlockSpec((CONV_K, DH), lambda b, s, h, cb=cb: (0, cb * HEADS + h))

    return pl.pallas_call(
        _gdn_kernel,
        grid=(batch, ns, HEADS),
        in_specs=[
            col(0), col(1), col(2), col(3),
            ccol(0), ccol(1), ccol(2),
            pl.BlockSpec((ts, LANES), lambda b, s, h: (b * ns + s, 0)),
            pl.BlockSpec((None, nc, 2, CHUNK), lambda b, s, h: (h, b * ns + s, 0, 0)),
            pl.BlockSpec((1, DH), lambda b, s, h: (0, 0)),
        ],
        out_specs=pl.BlockSpec((ts, DH), lambda b, s, h: (b * ns + s, h)),
        out_shape=jax.ShapeDtypeStruct((t, QK), BF16),
        scratch_shapes=[
            pltpu.VMEM((HEADS, 3, SUBLANES, DH), F32),
            pltpu.VMEM((3, ts + SUBLANES, DH), F32),
            pltpu.VMEM((HEADS, DH, DH), F32),
        ],
        compiler_params=pltpu.CompilerParams(
            dimension_semantics=("arbitrary", "arbitrary", "arbitrary"), vmem_limit_bytes=VMEM_LIMIT),
        name="gdn",
    )(pm, pm, pm, pm, conv_a, conv_a, conv_a, gcol, grow, norm_a)


def _ret_kernel(q_ref, k_ref, v_ref, gate_ref, cos_ref, sin_ref, nw_ref, o_ref, st_ref):
    c = q_ref.shape[0]
    s = pl.program_id(1)
    h = pl.program_id(2)

    @pl.when(s == 0)
    def _init():
        st_ref[h] = jnp.zeros((DH, DH), F32)

    cosf = cos_ref[...]
    sinf = sin_ref[...]

    def rope(x_ref):
        x = x_ref[...].astype(F32)
        return x * cosf + pltpu.roll(x, DH // 2, axis=1) * sinf

    q = rope(q_ref)
    k = rope(k_ref) * (DH ** -0.5)
    v16 = v_ref[...]
    lg = jnp.full((1, 1), _LOG_GAMMA[HEADS - 1], F32)
    for hh in range(HEADS - 1):
        lg = jnp.where(h == hh, _LOG_GAMMA[hh], lg)
    ii = lax.broadcasted_iota(I32, (c, c), 0)
    jj = lax.broadcasted_iota(I32, (c, c), 1)
    rel = (ii - jj).astype(F32)
    dec = jnp.where(rel >= 0, jnp.exp(jnp.maximum(rel, 0.0) * lg), 0.0)
    pos = lax.broadcasted_iota(I32, (c, 1), 0).astype(F32)
    q16 = q.astype(BF16)
    k16 = k.astype(BF16)
    attn = _dot_nt(q16, k16) * dec
    inner = _dot(attn.astype(BF16), v16)
    st = st_ref[h]
    qd = q * jnp.exp((pos + 1.0) * lg)
    cross = _dot(qd.astype(BF16), st.astype(BF16))
    kd = k * jnp.exp((c - 1.0 - pos) * lg)
    st_ref[h] = st * jnp.exp(c * lg) + _dot_tn(kd.astype(BF16), v16)
    o = cross + inner
    o = o * lax.rsqrt(jnp.mean(o * o, axis=-1, keepdims=True) + EPS) * nw_ref[...]
    gate = gate_ref[...].astype(F32)
    o_ref[...] = (o * (gate * _sigmoid(gate))).astype(o_ref.dtype)


def _ret(pm, cosf, sinf, norm_b, batch, seq):
    ts = TS_RET
    ns = seq // ts
    t = batch * seq

    def col(cb):
        return pl.BlockSpec((ts, DH), lambda b, s, h, cb=cb: (b * ns + s, (4 + cb) * HEADS + h))

    return pl.pallas_call(
        _ret_kernel,
        grid=(batch, ns, HEADS),
        in_specs=[
            col(0), col(1), col(2), col(3),
            pl.BlockSpec((ts, DH), lambda b, s, h: (s, 0)),
            pl.BlockSpec((ts, DH), lambda b, s, h: (s, 0)),
            pl.BlockSpec((1, DH), lambda b, s, h: (0, h)),
        ],
        out_specs=pl.BlockSpec((ts, DH), lambda b, s, h: (b * ns + s, h)),
        out_shape=jax.ShapeDtypeStruct((t, QK), BF16),
        scratch_shapes=[pltpu.VMEM((HEADS, DH, DH), F32)],
        compiler_params=pltpu.CompilerParams(
            dimension_semantics=("arbitrary", "arbitrary", "arbitrary"), vmem_limit_bytes=VMEM_LIMIT),
        name="ret",
    )(pm, pm, pm, pm, cosf, sinf, norm_b)


def _mix_kernel(x_ref, oa_ref, ob_ref, wo_ref, nw_ref, wrh_ref, wrl_ref,
                x1t_ref, h2t_ref, rr_ref, cnt_ref, carry_ref):
    tm = x_ref.shape[0]
    i = pl.program_id(0)

    @pl.when(i % (TS_MOE // tm) == 0)
    def _init():
        carry_ref[...] = jnp.zeros((1, LANES), F32)

    x1 = x_ref[...] + _dot(oa_ref[...], wo_ref[0:QK, :]) + _dot(ob_ref[...], wo_ref[QK:2 * QK, :])
    h2 = x1 * lax.rsqrt(jnp.mean(x1 * x1, axis=-1, keepdims=True) + EPS) * nw_ref[...]
    for s in range(SUBLANES):
        x1t_ref[pl.ds(s, tm, stride=SUBLANES), :] = x1[:, s * LANES:(s + 1) * LANES]
        h2t_ref[pl.ds(s, tm, stride=SUBLANES), :] = h2[:, s * LANES:(s + 1) * LANES]
    hh, hl = _split_bf16(h2)
    lg = _dot(hh, wrh_ref[...]) + _dot(hh, wrl_ref[...]) + _dot(hl, wrh_ref[...])
    lane = lax.broadcasted_iota(I32, (tm, LANES), 1)
    big = jnp.int32(1 << 20)
    neg = jnp.float32(-jnp.inf)

    def argmax_first(vals, mask):
        mv = jnp.where(mask, vals, neg)
        m = jnp.max(mv, axis=-1, keepdims=True)
        idx = jnp.min(jnp.where(mask & (mv == m), lane, big), axis=-1, keepdims=True)
        return m, idx

    gmask = lane < N_GROUPS
    gm, gi = argmax_first(lg, gmask)
    gw = 1.0 / jnp.sum(jnp.where(gmask, jnp.exp(lg - gm), 0.0), axis=-1, keepdims=True)
    emask = (lane >= N_EXPERTS) & ((lane >> 3) == gi + N_EXPERTS // EPG)
    m1, i1 = argmax_first(lg, emask)
    m2, i2 = argmax_first(lg, emask & (lane != i1))
    e21 = jnp.exp(m2 - m1)
    w0 = gw / (1.0 + e21)
    w1 = gw * e21 / (1.0 + e21)
    e0 = i1 - N_EXPERTS
    e1 = i2 - N_EXPERTS

    oh = ((lane == e0) | (lane == e1)).astype(BF16)
    ri = lax.broadcasted_iota(I32, (tm, tm), 0)
    ci = lax.broadcasted_iota(I32, (tm, tm), 1)
    lstrict = (ri > ci).astype(BF16)
    pref = _dot(lstrict, oh) + carry_ref[...]
    r0 = jnp.sum(jnp.where(lane == e0, pref, 0.0), axis=-1, keepdims=True)
    r1 = jnp.sum(jnp.where(lane == e1, pref, 0.0), axis=-1, keepdims=True)
    carry = carry_ref[...] + jnp.sum(oh.astype(F32), axis=0, keepdims=True)
    carry_ref[...] = carry
    cnt_ref[...] = jnp.broadcast_to(carry, (SUBLANES, LANES)).astype(I32)

    rmat = jnp.where(lane == 0, e0.astype(F32) * RANK_RADIX + r0,
                     jnp.where(lane == 1, e1.astype(F32) * RANK_RADIX + r1,
                               jnp.where(lane == 2, w0, jnp.where(lane == 3, w1, 0.0))))
    rr_ref[...] = rmat.T[0:SUBLANES, :]


def _mix(x2, oa, ob, w_out16, ffn_norm, wr_hi, wr_lo):
    t = x2.shape[0]
    tm = TM_MIX
    per = TS_MOE // tm
    return pl.pallas_call(
        _mix_kernel,
        grid=(t // tm,),
        in_specs=[
            pl.BlockSpec((tm, D_MODEL), lambda i: (i, 0)),
            pl.BlockSpec((tm, QK), lambda i: (i, 0)),
            pl.BlockSpec((tm, QK), lambda i: (i, 0)),
            pl.BlockSpec((D_MODEL, D_MODEL), lambda i: (0, 0)),
            pl.BlockSpec((1, D_MODEL), lambda i: (0, 0)),
            pl.BlockSpec((D_MODEL, LANES), lambda i: (0, 0)),
            pl.BlockSpec((D_MODEL, LANES), lambda i: (0, 0)),
        ],
        out_specs=[
            pl.BlockSpec((tm * SUBLANES, LANES), lambda i: (i, 0)),
            pl.BlockSpec((tm * SUBLANES, LANES), lambda i: (i, 0)),
            pl.BlockSpec((SUBLANES, tm), lambda i: (0, i)),
            pl.BlockSpec((None, SUBLANES, LANES), lambda i: (i // per, 0, 0)),
        ],
        out_shape=[
            jax.ShapeDtypeStruct((t * SUBLANES, LANES), F32),
            jax.ShapeDtypeStruct((t * SUBLANES, LANES), F32),
            jax.ShapeDtypeStruct((SUBLANES, t), F32),
            jax.ShapeDtypeStruct((t // TS_MOE, SUBLANES, LANES), I32),
        ],
        scratch_shapes=[pltpu.VMEM((1, LANES), F32)],
        compiler_params=pltpu.CompilerParams(
            dimension_semantics=("arbitrary",), vmem_limit_bytes=VMEM_LIMIT),
        name="mix",
    )(x2, oa, ob, w_out16, ffn_norm, wr_hi, wr_lo)


def _moe_kernel(ps0_ref, ps1_ref, pc0_ref, pc1_ref, w0_ref, w1_ref, cnt_ref,
                h2t_ref, x1t_ref, wg_ref, wu_ref, wd_ref, fnw_ref,
                out_ref, stag_ref, acc_ref, base_ref):
    j = pl.program_id(1)
    sub = out_ref.shape[0]
    nsub = TS_MOE // sub
    npair = N_EXPERTS // EXPERTS_PER_STEP
    ch = CH_MOE

    def row8(r):
        return pl.ds(pl.multiple_of(r * SUBLANES, SUBLANES), SUBLANES)

    def slot(p):
        return base_ref[p >> RANK_BITS] + (p & (RANK_RADIX - 1))

    @pl.when(j == 0)
    def _bases():
        stag_ref[pl.ds(TOP_K * TS_MOE * SUBLANES, ch * SUBLANES), :] = jnp.zeros((ch * SUBLANES, LANES), F32)

        def body(e, run):
            base_ref[e] = run
            return run + cnt_ref[e]

        lax.fori_loop(0, N_EXPERTS, body, jnp.int32(0))

    @pl.when(j < nsub)
    def _scatter():
        def body(t, carry):
            row = h2t_ref[row8(t), :]
            stag_ref[row8(slot(ps0_ref[t])), :] = row
            stag_ref[row8(slot(ps1_ref[t])), :] = row
            return carry

        lax.fori_loop(0, sub, body, 0, unroll=8)

    @pl.when((j >= nsub) & (j < nsub + npair))
    def _experts():
        rows = lax.broadcasted_iota(I32, (ch, LANES), 0)
        for q in range(EXPERTS_PER_STEP):
            e = (j - nsub) * EXPERTS_PER_STEP + q
            n = cnt_ref[e]
            b = base_ref[e]

            def chunk(c, carry, q=q, n=n, b=b):
                r0 = pl.multiple_of((b + c * ch) * SUBLANES, SUBLANES)
                xs = [stag_ref[pl.ds(r0 + s, ch, stride=SUBLANES), :] for s in range(SUBLANES)]
                x16 = jnp.concatenate([v.astype(BF16) for v in xs], axis=1)
                g = _dot(x16, wg_ref[q])
                u = _dot(x16, wu_ref[q])
                y = _dot(((g * _sigmoid(g)) * u).astype(BF16), wd_ref[q])
                keep = rows < (n - c * ch)
                for s in range(SUBLANES):
                    stag_ref[pl.ds(r0 + s, ch, stride=SUBLANES), :] = jnp.where(
                        keep, y[:, s * LANES:(s + 1) * LANES], xs[s])
                return carry

            lax.fori_loop(0, (n + ch - 1) // ch, chunk, 0)

    @pl.when(j >= nsub + npair)
    def _combine():
        def body(t, carry):
            acc_ref[row8(t), :] = (x1t_ref[row8(t), :]
                                   + w0_ref[t] * stag_ref[row8(slot(pc0_ref[t])), :]
                                   + w1_ref[t] * stag_ref[row8(slot(pc1_ref[t])), :])
            return carry

        lax.fori_loop(0, sub, body, 0, unroll=8)
        xo = jnp.concatenate([acc_ref[pl.ds(s, sub, stride=SUBLANES), :] for s in range(SUBLANES)], axis=1)
        out_ref[...] = xo * lax.rsqrt(jnp.mean(xo * xo, axis=-1, keepdims=True) + EPS) * fnw_ref[...]


def _moe(p0, p1, w0, w1, cnt, h2t, x1t, wg16, wu16, wd16, final_norm):
    t = p0.shape[0]
    sub = SUB_MOE
    nsub = TS_MOE // sub
    npair = N_EXPERTS // EXPERTS_PER_STEP
    nt = t // TS_MOE
    smem = functools.partial(pl.BlockSpec, memory_space=pltpu.SMEM)

    def scat(i, j):
        return i * nsub + jnp.minimum(j, nsub - 1)

    def comb(i, j):
        return i * nsub + jnp.clip(j - nsub - npair, 0, nsub - 1)

    def pair(i, j):
        return jnp.clip(j - nsub, 0, npair - 1)

    def wspec(shape):
        return pl.BlockSpec((EXPERTS_PER_STEP,) + shape, lambda i, j: (pair(i, j), 0, 0))

    return pl.pallas_call(
        _moe_kernel,
        grid=(nt, nsub + npair + nsub),
        in_specs=[
            smem((sub,), lambda i, j: (scat(i, j),)), smem((sub,), lambda i, j: (scat(i, j),)),
            smem((sub,), lambda i, j: (comb(i, j),)), smem((sub,), lambda i, j: (comb(i, j),)),
            smem((sub,), lambda i, j: (comb(i, j),)), smem((sub,), lambda i, j: (comb(i, j),)),
            smem((LANES,), lambda i, j: (i,)),
            pl.BlockSpec((sub * SUBLANES, LANES), lambda i, j: (scat(i, j), 0)),
            pl.BlockSpec((sub * SUBLANES, LANES), lambda i, j: (comb(i, j), 0)),
            wspec((D_MODEL, D_EXPERT)), wspec((D_MODEL, D_EXPERT)), wspec((D_EXPERT, D_MODEL)),
            pl.BlockSpec((1, D_MODEL), lambda i, j: (0, 0)),
        ],
        out_specs=pl.BlockSpec((sub, D_MODEL), lambda i, j: (comb(i, j), 0)),
        out_shape=jax.ShapeDtypeStruct((t, D_MODEL), F32),
        scratch_shapes=[
            pltpu.VMEM(((TOP_K * TS_MOE + CH_MOE) * SUBLANES, LANES), F32),
            pltpu.VMEM((sub * SUBLANES, LANES), F32),
            pltpu.SMEM((LANES,), I32),
        ],
        compiler_params=pltpu.CompilerParams(
            dimension_semantics=("arbitrary", "arbitrary"), vmem_limit_bytes=VMEM_LIMIT_MOE),
        name="moe",
    )(p0, p1, p0, p1, w0, w1, cnt, h2t, x1t, wg16, wu16, wd16, final_norm)


def _pad_lanes(a, start):
    rows, n = a.shape
    return jnp.pad(a, ((0, 0), (start, LANES - start - n)))


def kernel(x, attn_norm, w_in, conv_a, a_log, dt_bias, norm_a, norm_b, w_out, ffn_norm,
           w_router_group, w_router_expert, w_gate, w_up, w_down, final_norm):
    batch, seq, d = x.shape
    t = batch * seq
    x2 = x.reshape(t, d)
    assert attn_norm.shape[0] == 1, "the closing RMSNorm is fused into the single layer's MoE combine step"

    pos = jnp.arange(seq, dtype=F32)
    inv_freq = ROPE_BASE ** (-jnp.arange(0, DH, 2, dtype=F32) / DH)
    ang = pos[:, None] * inv_freq[None, :]
    cos, sin = jnp.cos(ang), jnp.sin(ang)
    cosf = jnp.concatenate([cos, cos], axis=-1)
    sinf = jnp.concatenate([-sin, sin], axis=-1)

    wl = w_in[0]
    w_main = jnp.concatenate([wl[:, 0:4 * QK], wl[:, 4 * QK + 2 * HEADS:]], axis=1).astype(BF16)
    ws_hi, ws_lo = _split_bf16(_pad_lanes(wl[:, 4 * QK:4 * QK + 2 * HEADS], 0))
    alog_l = _pad_lanes(a_log[0][None, :], HEADS)
    dtb_l = _pad_lanes(dt_bias[0][None, :], HEADS)
    pm, gcol, grow = _proj(x2, attn_norm[0][None, :], w_main, ws_hi, ws_lo, alog_l, dtb_l)

    oa = _gdn(pm, conv_a[0], gcol, grow, norm_a[0][None, :], batch, seq)
    ob = _ret(pm, cosf, sinf, norm_b[0][None, :], batch, seq)

    wr = jnp.concatenate(
        [_pad_lanes(w_router_group[0], 0)[:, 0:N_EXPERTS], w_router_expert[0]], axis=1)
    wr_hi, wr_lo = _split_bf16(wr)
    x1t, h2t, rr, cnt = _mix(x2, oa, ob, w_out[0].astype(BF16), ffn_norm[0][None, :], wr_hi, wr_lo)

    p0 = rr[0].astype(I32)
    p1 = rr[1].astype(I32)
    out = _moe(p0, p1, rr[2], rr[3], cnt[:, 0, :].reshape(-1), h2t, x1t,
               w_gate[0].astype(BF16), w_up[0].astype(BF16), w_down[0].astype(BF16), final_norm[None, :])
    return out.reshape(batch, seq, d)
```

```python
import functools
import math

import jax
import jax.numpy as jnp
from jax import lax
from jax.experimental import pallas as pl
from jax.experimental.pallas import tpu as pltpu

F32 = jnp.float32
BF16 = jnp.bfloat16
I32 = jnp.int32

D_MODEL = 1024
HEADS = 4
DH = 128
CONV_K = 4
CHUNK = 64
ROPE_BASE = 10000.0
N_GROUPS = 8
EPG = 8
N_EXPERTS = N_GROUPS * EPG
TOP_K = 2
D_EXPERT = 256
EPS = 1e-6
QK = HEADS * DH
N_MAIN = 8 * QK
LANES = 128
SUBLANES = 8

TM_PROJ = 512
TS_GDN = 512
TS_RET = 256
TM_MIX = 256
TS_MOE = 4096
SUB_MOE = 512
CH_FIRST_MOE = 192
CH_MOE = 128
SLACK_MOE = CH_FIRST_MOE + CH_MOE
EXPERTS_PER_STEP = 2
RANK_BITS = 16
RANK_RADIX = 1 << RANK_BITS
VMEM_LIMIT = 48 * 1024 * 1024
VMEM_LIMIT_MOE = 58 * 1024 * 1024
_LOG_GAMMA = tuple(math.log(1.0 - 2.0 ** (-5.0 - hh)) for hh in range(HEADS))


def _sigmoid(x):
    return 1.0 / (1.0 + jnp.exp(-x))


def _softplus(x):
    return jnp.maximum(x, 0.0) + jnp.log1p(jnp.exp(-jnp.abs(x)))


def _split_bf16(a):
    hi = a.astype(BF16)
    lo = (a - hi.astype(F32)).astype(BF16)
    return hi, lo


def _dot(a, b):
    return jnp.dot(a, b, preferred_element_type=F32)


def _dot_nt(a, b):
    return lax.dot_general(a, b, (((1,), (1,)), ((), ())), preferred_element_type=F32)


def _dot_tn(a, b):
    return lax.dot_general(a, b, (((0,), (0,)), ((), ())), preferred_element_type=F32)


def _bmm(a, b):
    return jnp.stack([_dot(a[c], b[c]) for c in range(a.shape[0])])


def _bmm_nt(a, b):
    return jnp.stack([_dot_nt(a[c], b[c]) for c in range(a.shape[0])])


def _proj_kernel(x_ref, nw_ref, wm_ref, wsh_ref, wsl_ref, alog_ref, dtb_ref,
                 pm_ref, gcol_ref, grow_ref):
    tm = x_ref.shape[0]
    x = x_ref[...]
    h = x * lax.rsqrt(jnp.mean(x * x, axis=-1, keepdims=True) + EPS) * nw_ref[...]
    hb = h.astype(BF16)
    nc = 512
    for j in range(N_MAIN // nc):
        pm_ref[:, j * nc:(j + 1) * nc] = _dot(hb, wm_ref[:, j * nc:(j + 1) * nc]).astype(BF16)
    hl = (h - hb.astype(F32)).astype(BF16)
    sm = _dot(hb, wsh_ref[...]) + _dot(hb, wsl_ref[...]) + _dot(hl, wsh_ref[...])
    lane = lax.broadcasted_iota(I32, (tm, LANES), 1)
    row = lax.broadcasted_iota(I32, (tm, LANES), 0)
    beta = _sigmoid(sm)
    g = -jnp.exp(alog_ref[...]) * _softplus(sm + dtb_ref[...])
    rin = row & (CHUNK - 1)
    gc = g
    d = 1
    while d < CHUNK:
        gc = gc + jnp.where(rin >= d, pltpu.roll(gc, d, axis=0), 0.0)
        d *= 2
    out = jnp.where(lane < HEADS, beta, jnp.where(lane < 2 * HEADS, gc, 0.0))
    gcol_ref[...] = out
    tr = out.T
    for hh in range(HEADS):
        for c in range(tm // CHUNK):
            grow_ref[hh, c, 0:1, :] = tr[hh:hh + 1, c * CHUNK:(c + 1) * CHUNK]
            grow_ref[hh, c, 1:2, :] = tr[HEADS + hh:HEADS + hh + 1, c * CHUNK:(c + 1) * CHUNK]


def _proj(x2, attn_norm, w_main, ws_hi, ws_lo, alog_l, dtb_l):
    t = x2.shape[0]
    tm = TM_PROJ
    return pl.pallas_call(
        _proj_kernel,
        grid=(t // tm,),
        in_specs=[
            pl.BlockSpec((tm, D_MODEL), lambda i: (i, 0)),
            pl.BlockSpec((1, D_MODEL), lambda i: (0, 0)),
            pl.BlockSpec((D_MODEL, N_MAIN), lambda i: (0, 0)),
            pl.BlockSpec((D_MODEL, LANES), lambda i: (0, 0)),
            pl.BlockSpec((D_MODEL, LANES), lambda i: (0, 0)),
            pl.BlockSpec((1, LANES), lambda i: (0, 0)),
            pl.BlockSpec((1, LANES), lambda i: (0, 0)),
        ],
        out_specs=[
            pl.BlockSpec((tm, N_MAIN), lambda i: (i, 0)),
            pl.BlockSpec((tm, LANES), lambda i: (i, 0)),
            pl.BlockSpec((HEADS, tm // CHUNK, 2, CHUNK), lambda i: (0, i, 0, 0)),
        ],
        out_shape=[
            jax.ShapeDtypeStruct((t, N_MAIN), BF16),
            jax.ShapeDtypeStruct((t, LANES), F32),
            jax.ShapeDtypeStruct((HEADS, t // CHUNK, 2, CHUNK), F32),
        ],
        compiler_params=pltpu.CompilerParams(
            dimension_semantics=("arbitrary",), vmem_limit_bytes=VMEM_LIMIT),
        name="proj",
    )(x2, attn_norm, w_main, ws_hi, ws_lo, alog_l, dtb_l)


def _gdn_kernel(q_ref, k_ref, v_ref, gate_ref, cq_ref, ck_ref, cv_ref, gcol_ref, grow_ref, nw_ref,
                o_ref, tail_ref, ext_ref, st_ref):
    ts = q_ref.shape[0]
    s = pl.program_id(1)

    @pl.when(s == 0)
    def _init():
        tail_ref[...] = jnp.zeros(tail_ref.shape, F32)
        st_ref[...] = jnp.zeros(st_ref.shape, F32)

    def conv_silu(x_ref, c_ref, idx):
        x = x_ref[...].astype(F32)
        ext_ref[idx, 0:SUBLANES, :] = tail_ref[idx]
        ext_ref[idx, SUBLANES:SUBLANES + ts, :] = x
        tail_ref[idx] = x[ts - SUBLANES:ts, :]
        w = c_ref[...]
        y = w[0:1, :] * ext_ref[idx, pl.ds(SUBLANES - 3, ts), :]
        for j in range(1, CONV_K):
            y = y + w[j:j + 1, :] * ext_ref[idx, pl.ds(SUBLANES - 3 + j, ts), :]
        return y * _sigmoid(y)

    q = conv_silu(q_ref, cq_ref, 0)
    k = conv_silu(k_ref, ck_ref, 1)
    v = conv_silu(v_ref, cv_ref, 2)
    gblk = gcol_ref[...]
    gate = gate_ref[...].astype(F32)
    nw = nw_ref[...]
    o = _gdn_heads(q, k, v, gblk, grow_ref[...], st_ref)
    outs = []
    for h in range(HEADS):
        sl = slice(h * DH, (h + 1) * DH)
        oh = o[:, sl]
        oh = oh * lax.rsqrt(jnp.mean(oh * oh, axis=-1, keepdims=True) + EPS) * nw
        outs.append(oh * (gate[:, sl] * _sigmoid(gate[:, sl])))
    o_ref[...] = jnp.concatenate(outs, axis=1).astype(o_ref.dtype)


def _gdn_heads(q, k, v, gblk, grow, st_ref):
    ts = q.shape[0]
    ncl = ts // CHUNK
    nc = HEADS * ncl

    def per_head(f):
        return jnp.concatenate([f(h, slice(h * DH, (h + 1) * DH)) for h in range(HEADS)], axis=0)

    lane = lax.broadcasted_iota(I32, (ts, LANES), 1)
    q3 = per_head(lambda h, sl: (q[:, sl] * (lax.rsqrt(jnp.sum(q[:, sl] * q[:, sl], axis=-1, keepdims=True) + EPS)
                                             * (DH ** -0.5))).reshape(ncl, CHUNK, DH))
    k3 = per_head(lambda h, sl: (k[:, sl] * lax.rsqrt(jnp.sum(k[:, sl] * k[:, sl], axis=-1, keepdims=True) + EPS)
                                 ).reshape(ncl, CHUNK, DH))
    v3 = per_head(lambda h, sl: v[:, sl].reshape(ncl, CHUNK, DH))
    bc = per_head(lambda h, sl: jnp.sum(jnp.where(lane == h, gblk, 0.0), axis=-1, keepdims=True
                                        ).reshape(ncl, CHUNK, 1))
    gcc = per_head(lambda h, sl: jnp.sum(jnp.where(lane == h + HEADS, gblk, 0.0), axis=-1, keepdims=True
                                         ).reshape(ncl, CHUNK, 1))
    gr = grow.reshape(nc, 2, CHUNK)
    beta_r = gr[:, 0:1, :]
    gc_r = gr[:, 1:2, :]
    gl = gcc[:, CHUNK - 1:CHUNK, :]

    ii = lax.broadcasted_iota(I32, (nc, CHUNK, CHUNK), 1)
    jj = lax.broadcasted_iota(I32, (nc, CHUNK, CHUNK), 2)
    causal = ii >= jj
    strict = ii > jj
    dec = jnp.where(causal, jnp.exp(jnp.where(causal, gcc - gc_r, 0.0)), 0.0)

    k16 = k3.astype(BF16)
    q16 = q3.astype(BF16)
    v16 = v3.astype(BF16)
    gram = _bmm_nt(k16, k16)
    qk = _bmm_nt(q16, k16)
    a = jnp.where(strict, gram * dec * bc, 0.0)

    eye = (ii == jj).astype(F32)
    dm = eye - jnp.where((ii >> 1) == (jj >> 1), a, 0.0)
    sblk = 2
    while sblk < CHUNK:
        sh = sblk.bit_length() - 1
        off = jnp.where(((ii >> (sh + 1)) == (jj >> (sh + 1))) & ((ii >> sh) != (jj >> sh)), a, 0.0)
        d16 = dm.astype(BF16)
        t1 = _bmm(d16, off.astype(BF16))
        dm = dm - _bmm(t1.astype(BF16), d16)
        sblk *= 2

    tb = dm * beta_r
    tbe = tb * jnp.exp(gc_r)
    u = _bmm(tb.astype(BF16), v16)
    w = _bmm(tbe.astype(BF16), k16)
    attn = qk * dec
    wu = jnp.concatenate([w, u], axis=-1)
    wu16 = wu.astype(BF16)
    aw = _bmm(attn.astype(BF16), wu16)
    qe = q3 * jnp.exp(gcc) - aw[:, :, 0:DH]
    ol = aw[:, :, DH:2 * DH]
    kdec = k3 * jnp.exp(gl - gcc)

    trow = lax.broadcasted_iota(I32, (2 * CHUNK, DH), 0)
    xn = []
    for p in range(nc // 2):
        kd2 = kdec[2 * p:2 * p + 2].reshape(2 * CHUNK, DH)
        wu2 = wu16[2 * p:2 * p + 2].reshape(2 * CHUNK, 2 * DH)
        xn.append(_dot_tn(jnp.where(trow < CHUNK, kd2, 0.0).astype(BF16), wu2))
        xn.append(_dot_tn(jnp.where(trow >= CHUNK, kd2, 0.0).astype(BF16), wu2))

    sts = [st_ref[h] for h in range(HEADS)]
    outs = [[] for _ in range(HEADS)]
    for c in range(ncl):
        for h in range(HEADS):
            i = h * ncl + c
            s16 = sts[h].astype(BF16)
            outs[h].append(_dot(qe[i].astype(BF16), s16) + ol[i])
            sts[h] = jnp.exp(gl[i]) * sts[h] + (xn[i][:, DH:2 * DH] - _dot(xn[i][:, 0:DH].astype(BF16), s16))
    for h in range(HEADS):
        st_ref[h] = sts[h]
    return jnp.concatenate([jnp.concatenate(outs[h], axis=0) for h in range(HEADS)], axis=1)


def _gdn(pm, conv_a, gcol, grow, norm_a, batch, seq):
    ts = TS_GDN
    ns = seq // ts
    nc = ts // CHUNK
    t = batch * seq

    def col(cb):
        return pl.BlockSpec((ts, QK), lambda b, s, cb=cb: (b * ns + s, cb))

    def ccol(cb):
        return pl.BlockSpec((CONV_K, QK), lambda b, s, cb=cb: (0, cb))

    return pl.pallas_call(
        _gdn_kernel,
        grid=(batch, ns),
        in_specs=[
            col(0), col(1), col(2), col(3),
            ccol(0), ccol(1), ccol(2),
            pl.BlockSpec((ts, LANES), lambda b, s: (b * ns + s, 0)),
            pl.BlockSpec((HEADS, nc, 2, CHUNK), lambda b, s: (0, b * ns + s, 0, 0)),
            pl.BlockSpec((1, DH), lambda b, s: (0, 0)),
        ],
        out_specs=pl.BlockSpec((ts, QK), lambda b, s: (b * ns + s, 0)),
        out_shape=jax.ShapeDtypeStruct((t, QK), BF16),
        scratch_shapes=[
            pltpu.VMEM((3, SUBLANES, QK), F32),
            pltpu.VMEM((3, ts + SUBLANES, QK), F32),
            pltpu.VMEM((HEADS, DH, DH), F32),
        ],
        compiler_params=pltpu.CompilerParams(
            dimension_semantics=("arbitrary", "arbitrary"), vmem_limit_bytes=VMEM_LIMIT),
        name="gdn",
    )(pm, pm, pm, pm, conv_a, conv_a, conv_a, gcol, grow, norm_a)


def _ret_kernel(q_ref, k_ref, v_ref, gate_ref, cos_ref, sin_ref, nw_ref, o_ref, st_ref):
    c = q_ref.shape[0]
    s = pl.program_id(1)

    @pl.when(s == 0)
    def _init():
        st_ref[...] = jnp.zeros(st_ref.shape, F32)

    cosf = cos_ref[...]
    sinf = sin_ref[...]
    ii = lax.broadcasted_iota(I32, (c, c), 0)
    jj = lax.broadcasted_iota(I32, (c, c), 1)
    rel = (ii - jj).astype(F32)
    pos = lax.broadcasted_iota(I32, (c, 1), 0).astype(F32)
    qa = q_ref[...].astype(F32)
    ka = k_ref[...].astype(F32)
    va = v_ref[...]
    gate = gate_ref[...].astype(F32)
    nw = nw_ref[...]
    outs = []
    for h in range(HEADS):
        sl = slice(h * DH, (h + 1) * DH)
        lg = _LOG_GAMMA[h]
        q = qa[:, sl] * cosf + pltpu.roll(qa[:, sl], DH // 2, axis=1) * sinf
        k = (ka[:, sl] * cosf + pltpu.roll(ka[:, sl], DH // 2, axis=1) * sinf) * (DH ** -0.5)
        v16 = va[:, sl]
        dec = jnp.where(rel >= 0, jnp.exp(jnp.maximum(rel, 0.0) * lg), 0.0)
        k16 = k.astype(BF16)
        attn = _dot_nt(q.astype(BF16), k16) * dec
        inner = _dot(attn.astype(BF16), v16)
        st = st_ref[h]
        qd = q * jnp.exp((pos + 1.0) * lg)
        cross = _dot(qd.astype(BF16), st.astype(BF16))
        kd = k * jnp.exp((c - 1.0 - pos) * lg)
        st_ref[h] = st * math.exp(c * lg) + _dot_tn(kd.astype(BF16), v16)
        o = cross + inner
        o = o * lax.rsqrt(jnp.mean(o * o, axis=-1, keepdims=True) + EPS) * nw[:, sl]
        outs.append(o * (gate[:, sl] * _sigmoid(gate[:, sl])))
    o_ref[...] = jnp.concatenate(outs, axis=1).astype(o_ref.dtype)


def _ret(pm, cosf, sinf, norm_b, batch, seq):
    ts = TS_RET
    ns = seq // ts
    t = batch * seq

    def col(cb):
        return pl.BlockSpec((ts, QK), lambda b, s, cb=cb: (b * ns + s, 4 + cb))

    return pl.pallas_call(
        _ret_kernel,
        grid=(batch, ns),
        in_specs=[
            col(0), col(1), col(2), col(3),
            pl.BlockSpec((ts, DH), lambda b, s: (s, 0)),
            pl.BlockSpec((ts, DH), lambda b, s: (s, 0)),
            pl.BlockSpec((1, QK), lambda b, s: (0, 0)),
        ],
        out_specs=pl.BlockSpec((ts, QK), lambda b, s: (b * ns + s, 0)),
        out_shape=jax.ShapeDtypeStruct((t, QK), BF16),
        scratch_shapes=[pltpu.VMEM((HEADS, DH, DH), F32)],
        compiler_params=pltpu.CompilerParams(
            dimension_semantics=("arbitrary", "arbitrary"), vmem_limit_bytes=VMEM_LIMIT),
        name="ret",
    )(pm, pm, pm, pm, cosf, sinf, norm_b)


def _mix_kernel(x_ref, oa_ref, ob_ref, wo_ref, nw_ref, wrh_ref, wrl_ref,
                x1t_ref, h2t_ref, rr_ref, cnt_ref, carry_ref):
    tm = x_ref.shape[0]
    i = pl.program_id(0)
    per = TS_MOE // tm

    @pl.when(i % per == 0)
    def _init():
        carry_ref[...] = jnp.zeros((1, LANES), F32)

    x1 = x_ref[...] + _dot(oa_ref[...], wo_ref[0:QK, :]) + _dot(ob_ref[...], wo_ref[QK:2 * QK, :])
    h2 = x1 * lax.rsqrt(jnp.mean(x1 * x1, axis=-1, keepdims=True) + EPS) * nw_ref[...]
    for s in range(SUBLANES):
        x1t_ref[pl.ds(s, tm, stride=SUBLANES), :] = x1[:, s * LANES:(s + 1) * LANES]
        h2t_ref[pl.ds(s, tm, stride=SUBLANES), :] = h2[:, s * LANES:(s + 1) * LANES]
    hh, hl = _split_bf16(h2)
    lg = _dot(hh, wrh_ref[...]) + _dot(hh, wrl_ref[...]) + _dot(hl, wrh_ref[...])
    lane = lax.broadcasted_iota(I32, (tm, LANES), 1)
    big = jnp.int32(1 << 20)
    neg = jnp.float32(-jnp.inf)

    def argmax_first(vals, mask):
        mv = jnp.where(mask, vals, neg)
        m = jnp.max(mv, axis=-1, keepdims=True)
        idx = jnp.min(jnp.where(mask & (mv == m), lane, big), axis=-1, keepdims=True)
        return m, idx

    gmask = lane < N_GROUPS
    gm, gi = argmax_first(lg, gmask)
    gw = 1.0 / jnp.sum(jnp.where(gmask, jnp.exp(lg - gm), 0.0), axis=-1, keepdims=True)
    emask = (lane >= N_EXPERTS) & ((lane >> 3) == gi + N_EXPERTS // EPG)
    m1, i1 = argmax_first(lg, emask)
    m2, i2 = argmax_first(lg, emask & (lane != i1))
    e21 = jnp.exp(m2 - m1)
    w0 = gw / (1.0 + e21)
    w1 = gw * e21 / (1.0 + e21)
    e0 = i1 - N_EXPERTS
    e1 = i2 - N_EXPERTS

    oh = ((lane == e0) | (lane == e1)).astype(BF16)
    ri = lax.broadcasted_iota(I32, (tm, tm), 0)
    ci = lax.broadcasted_iota(I32, (tm, tm), 1)
    lstrict = (ri > ci).astype(BF16)
    pref = _dot(lstrict, oh) + carry_ref[...]
    r0 = jnp.sum(jnp.where(lane == e0, pref, 0.0), axis=-1, keepdims=True)
    r1 = jnp.sum(jnp.where(lane == e1, pref, 0.0), axis=-1, keepdims=True)
    carry = carry_ref[...] + jnp.sum(oh.astype(F32), axis=0, keepdims=True)
    carry_ref[...] = carry
    cnt_ref[...] = jnp.broadcast_to(carry, (SUBLANES, LANES)).astype(I32)

    rmat = jnp.where(lane == 0, e0.astype(F32) * RANK_RADIX + r0,
                     jnp.where(lane == 1, e1.astype(F32) * RANK_RADIX + r1,
                               jnp.where(lane == 2, w0, jnp.where(lane == 3, w1, 0.0))))
    k = i % per
    rr_ref[k] = rmat.T[0:SUBLANES, :]

    @pl.when(k == per - 1)
    def _slots():
        lane8 = lax.broadcasted_iota(I32, (SUBLANES, LANES), 1)
        cnt8 = jnp.broadcast_to(carry, (SUBLANES, LANES))
        inc = cnt8
        d = 1
        while d < LANES:
            inc = inc + jnp.where(lane8 >= d, pltpu.roll(inc, d, axis=1), 0.0)
            d *= 2
        base = inc - cnt8
        pk = rr_ref[:, 0:TOP_K, :]
        ex = jnp.floor(pk * (1.0 / RANK_RADIX))
        slot = pk - ex * RANK_RADIX
        for kk in range(N_EXPERTS):
            bk = jnp.sum(jnp.where(lane8 == kk, base, 0.0), axis=-1, keepdims=True)[0:1, :]
            slot = slot + jnp.where(ex == kk, bk, 0.0)
        rr_ref[:, 0:TOP_K, :] = slot


def _mix(x2, oa, ob, w_out16, ffn_norm, wr_hi, wr_lo):
    t = x2.shape[0]
    tm = TM_MIX
    per = TS_MOE // tm
    return pl.pallas_call(
        _mix_kernel,
        grid=(t // tm,),
        in_specs=[
            pl.BlockSpec((tm, D_MODEL), lambda i: (i, 0)),
            pl.BlockSpec((tm, QK), lambda i: (i, 0)),
            pl.BlockSpec((tm, QK), lambda i: (i, 0)),
            pl.BlockSpec((D_MODEL, D_MODEL), lambda i: (0, 0)),
            pl.BlockSpec((1, D_MODEL), lambda i: (0, 0)),
            pl.BlockSpec((D_MODEL, LANES), lambda i: (0, 0)),
            pl.BlockSpec((D_MODEL, LANES), lambda i: (0, 0)),
        ],
        out_specs=[
            pl.BlockSpec((tm * SUBLANES, LANES), lambda i: (i, 0)),
            pl.BlockSpec((tm * SUBLANES, LANES), lambda i: (i, 0)),
            pl.BlockSpec((per, SUBLANES, tm), lambda i: (i // per, 0, 0)),
            pl.BlockSpec((None, SUBLANES, LANES), lambda i: (i // per, 0, 0)),
        ],
        out_shape=[
            jax.ShapeDtypeStruct((t * SUBLANES, LANES), F32),
            jax.ShapeDtypeStruct((t * SUBLANES, LANES), F32),
            jax.ShapeDtypeStruct((t // tm, SUBLANES, tm), F32),
            jax.ShapeDtypeStruct((t // TS_MOE, SUBLANES, LANES), I32),
        ],
        scratch_shapes=[pltpu.VMEM((1, LANES), F32)],
        compiler_params=pltpu.CompilerParams(
            dimension_semantics=("arbitrary",), vmem_limit_bytes=VMEM_LIMIT),
        name="mix",
    )(x2, oa, ob, w_out16, ffn_norm, wr_hi, wr_lo)


def _moe_kernel(ps0_ref, ps1_ref, pc0_ref, pc1_ref, w0_ref, w1_ref, cnt_ref,
                h2t_ref, x1t_ref, wg_ref, wu_ref, wd_ref, fnw_ref,
                out_ref, stag_ref, acc_ref, base_ref):
    j = pl.program_id(1)
    sub = out_ref.shape[0]
    nsub = TS_MOE // sub
    npair = N_EXPERTS // EXPERTS_PER_STEP
    def row8(r):
        return pl.ds(pl.multiple_of(r * SUBLANES, SUBLANES), SUBLANES)

    @pl.when(j == 0)
    def _bases():
        stag_ref[pl.ds(TOP_K * TS_MOE * SUBLANES, SLACK_MOE * SUBLANES), :] = jnp.zeros(
            (SLACK_MOE * SUBLANES, LANES), F32)

        def body(e, run):
            base_ref[e] = run
            return run + cnt_ref[e]

        lax.fori_loop(0, N_EXPERTS, body, jnp.int32(0))

    @pl.when(j < nsub)
    def _scatter():
        def body(t, carry):
            row = h2t_ref[row8(t), :]
            stag_ref[row8(ps0_ref[t]), :] = row
            stag_ref[row8(ps1_ref[t]), :] = row
            return carry

        lax.fori_loop(0, sub, body, 0, unroll=8)

    def load_rows(r0, m):
        xs = [stag_ref[pl.ds(r0 + s, m, stride=SUBLANES), :] for s in range(SUBLANES)]
        return xs, jnp.concatenate([v.astype(BF16) for v in xs], axis=1)

    def store_rows(r0, m, xs, y, nvalid):
        keep = lax.broadcasted_iota(I32, (m, LANES), 0) < nvalid
        for s in range(SUBLANES):
            stag_ref[pl.ds(r0 + s, m, stride=SUBLANES), :] = jnp.where(
                keep, y[:, s * LANES:(s + 1) * LANES], xs[s])

    @pl.when((j >= nsub) & (j < nsub + npair))
    def _experts():
        qs = range(EXPERTS_PER_STEP)
        es = [(j - nsub) * EXPERTS_PER_STEP + q for q in qs]
        ns = [cnt_ref[e] for e in es]
        bs = [base_ref[e] for e in es]
        m = CH_FIRST_MOE
        r0s = [pl.multiple_of(b * SUBLANES, SUBLANES) for b in bs]
        ld = [load_rows(r0, m) for r0 in r0s]
        gs = [_dot(ld[q][1], wg_ref[q]) for q in qs]
        us = [_dot(ld[q][1], wu_ref[q]) for q in qs]
        hs = [((gs[q] * _sigmoid(gs[q])) * us[q]).astype(BF16) for q in qs]
        ys = [_dot(hs[q], wd_ref[q]) for q in qs]
        for q in qs:
            store_rows(r0s[q], m, ld[q][0], ys[q], ns[q])
        ch = CH_MOE
        for q in qs:
            def chunk(c, carry, q=q):
                r0 = pl.multiple_of((bs[q] + m + c * ch) * SUBLANES, SUBLANES)
                xs, x16 = load_rows(r0, ch)
                g = _dot(x16, wg_ref[q])
                u = _dot(x16, wu_ref[q])
                y = _dot(((g * _sigmoid(g)) * u).astype(BF16), wd_ref[q])
                store_rows(r0, ch, xs, y, ns[q] - m - c * ch)
                return carry

            lax.fori_loop(0, (jnp.maximum(ns[q] - m, 0) + ch - 1) // ch, chunk, 0)

    @pl.when(j >= nsub + npair)
    def _combine():
        def body(t, carry):
            acc_ref[row8(t), :] = (x1t_ref[row8(t), :]
                                   + w0_ref[t] * stag_ref[row8(pc0_ref[t]), :]
                                   + w1_ref[t] * stag_ref[row8(pc1_ref[t]), :])
            return carry

        lax.fori_loop(0, sub, body, 0, unroll=8)
        xo = jnp.concatenate([acc_ref[pl.ds(s, sub, stride=SUBLANES), :] for s in range(SUBLANES)], axis=1)
        out_ref[...] = xo * lax.rsqrt(jnp.mean(xo * xo, axis=-1, keepdims=True) + EPS) * fnw_ref[...]


def _moe(p0, p1, w0, w1, cnt, h2t, x1t, wg16, wu16, wd16, final_norm):
    t = p0.shape[0]
    sub = SUB_MOE
    nsub = TS_MOE // sub
    npair = N_EXPERTS // EXPERTS_PER_STEP
    nt = t // TS_MOE
    smem = functools.partial(pl.BlockSpec, memory_space=pltpu.SMEM)

    def scat(i, j):
        return i * nsub + jnp.minimum(j, nsub - 1)

    def comb(i, j):
        return i * nsub + jnp.clip(j - nsub - npair, 0, nsub - 1)

    def pair(i, j):
        return jnp.clip(j - nsub, 0, npair - 1)

    def wspec(shape):
        return pl.BlockSpec((EXPERTS_PER_STEP,) + shape, lambda i, j: (pair(i, j), 0, 0))

    return pl.pallas_call(
        _moe_kernel,
        grid=(nt, nsub + npair + nsub),
        in_specs=[
            smem((sub,), lambda i, j: (scat(i, j),)), smem((sub,), lambda i, j: (scat(i, j),)),
            smem((sub,), lambda i, j: (comb(i, j),)), smem((sub,), lambda i, j: (comb(i, j),)),
            smem((sub,), lambda i, j: (comb(i, j),)), smem((sub,), lambda i, j: (comb(i, j),)),
            smem((LANES,), lambda i, j: (i,)),
            pl.BlockSpec((sub * SUBLANES, LANES), lambda i, j: (scat(i, j), 0)),
            pl.BlockSpec((sub * SUBLANES, LANES), lambda i, j: (comb(i, j), 0)),
            wspec((D_MODEL, D_EXPERT)), wspec((D_MODEL, D_EXPERT)), wspec((D_EXPERT, D_MODEL)),
            pl.BlockSpec((1, D_MODEL), lambda i, j: (0, 0)),
        ],
        out_specs=pl.BlockSpec((sub, D_MODEL), lambda i, j: (comb(i, j), 0)),
        out_shape=jax.ShapeDtypeStruct((t, D_MODEL), F32),
        scratch_shapes=[
            pltpu.VMEM(((TOP_K * TS_MOE + SLACK_MOE) * SUBLANES, LANES), F32),
            pltpu.VMEM((sub * SUBLANES, LANES), F32),
            pltpu.SMEM((LANES,), I32),
        ],
        compiler_params=pltpu.CompilerParams(
            dimension_semantics=("arbitrary", "arbitrary"), vmem_limit_bytes=VMEM_LIMIT_MOE),
        name="moe",
    )(p0, p1, p0, p1, w0, w1, cnt, h2t, x1t, wg16, wu16, wd16, final_norm)


def _pad_lanes(a, start):
    rows, n = a.shape
    return jnp.pad(a, ((0, 0), (start, LANES - start - n)))


def kernel(x, attn_norm, w_in, conv_a, a_log, dt_bias, norm_a, norm_b, w_out, ffn_norm,
           w_router_group, w_router_expert, w_gate, w_up, w_down, final_norm):
    batch, seq, d = x.shape
    t = batch * seq
    x2 = x.reshape(t, d)
    assert attn_norm.shape[0] == 1, "the closing RMSNorm is fused into the single layer's MoE combine step"

    pos = jnp.arange(seq, dtype=F32)
    inv_freq = ROPE_BASE ** (-jnp.arange(0, DH, 2, dtype=F32) / DH)
    ang = pos[:, None] * inv_freq[None, :]
    cos, sin = jnp.cos(ang), jnp.sin(ang)
    cosf = jnp.concatenate([cos, cos], axis=-1)
    sinf = jnp.concatenate([-sin, sin], axis=-1)

    wl = w_in[0]
    w_main = jnp.concatenate([wl[:, 0:4 * QK], wl[:, 4 * QK + 2 * HEADS:]], axis=1).astype(BF16)
    ws_hi, ws_lo = _split_bf16(_pad_lanes(wl[:, 4 * QK:4 * QK + 2 * HEADS], 0))
    alog_l = _pad_lanes(a_log[0][None, :], HEADS)
    dtb_l = _pad_lanes(dt_bias[0][None, :], HEADS)
    pm, gcol, grow = _proj(x2, attn_norm[0][None, :], w_main, ws_hi, ws_lo, alog_l, dtb_l)

    oa = _gdn(pm, conv_a[0], gcol, grow, norm_a[0][None, :], batch, seq)
    ob = _ret(pm, cosf, sinf, norm_b[0][None, :], batch, seq)

    wr = jnp.concatenate(
        [_pad_lanes(w_router_group[0], 0)[:, 0:N_EXPERTS], w_router_expert[0]], axis=1)
    wr_hi, wr_lo = _split_bf16(wr)
    x1t, h2t, rr, cnt = _mix(x2, oa, ob, w_out[0].astype(BF16), ffn_norm[0][None, :], wr_hi, wr_lo)

    p0 = rr[:, 0, :].reshape(-1).astype(I32)
    p1 = rr[:, 1, :].reshape(-1).astype(I32)
    out = _moe(p0, p1, rr[:, 2, :].reshape(-1), rr[:, 3, :].reshape(-1), cnt[:, 0, :].reshape(-1), h2t, x1t,
               w_gate[0].astype(BF16), w_up[0].astype(BF16), w_down[0].astype(BF16), final_norm[None, :])
    return out.reshape(batch, seq, d)
```

```python
import functools
import math

import jax
import jax.numpy as jnp
from jax import lax
from jax.experimental import pallas as pl
from jax.experimental.pallas import tpu as pltpu

F32 = jnp.float32
BF16 = jnp.bfloat16
I32 = jnp.int32

D_MODEL = 1024
HEADS = 4
DH = 128
CONV_K = 4
CHUNK = 64
ROPE_BASE = 10000.0
N_GROUPS = 8
EPG = 8
N_EXPERTS = N_GROUPS * EPG
TOP_K = 2
D_EXPERT = 256
EPS = 1e-6
QK = HEADS * DH
N_MAIN = 8 * QK
LANES = 128
SUBLANES = 8

TM_PROJ = 512
TS_GDN = 512
TS_RET = 256
TM_MIX = 256
TS_MOE = 4096
SUB_MOE = 256
CH_FIRST_MOE = 160
CH_MOE = 128
SLACK_MOE = CH_FIRST_MOE + CH_MOE
EXPERTS_PER_STEP = 4
RANK_BITS = 16
RANK_RADIX = 1 << RANK_BITS
VMEM_LIMIT = 48 * 1024 * 1024
VMEM_LIMIT_MOE = 58 * 1024 * 1024
_LOG_GAMMA = tuple(math.log(1.0 - 2.0 ** (-5.0 - hh)) for hh in range(HEADS))


def _sigmoid(x):
    return 1.0 / (1.0 + jnp.exp(-x))


def _softplus(x):
    return jnp.maximum(x, 0.0) + jnp.log1p(jnp.exp(-jnp.abs(x)))


def _split_bf16(a):
    hi = a.astype(BF16)
    lo = (a - hi.astype(F32)).astype(BF16)
    return hi, lo


def _dot(a, b):
    return jnp.dot(a, b, preferred_element_type=F32)


def _dot_nt(a, b):
    return lax.dot_general(a, b, (((1,), (1,)), ((), ())), preferred_element_type=F32)


def _dot_tn(a, b):
    return lax.dot_general(a, b, (((0,), (0,)), ((), ())), preferred_element_type=F32)


def _bmm(a, b):
    return jnp.stack([_dot(a[c], b[c]) for c in range(a.shape[0])])


def _bmm_nt(a, b):
    return jnp.stack([_dot_nt(a[c], b[c]) for c in range(a.shape[0])])


def _proj_kernel(x_ref, nw_ref, wm_ref, wsh_ref, wsl_ref, alog_ref, dtb_ref, cw_ref,
                 pm_ref, gcol_ref, grow_ref, ext_ref, tail_ref, *, tiles_per_seq):
    tm = x_ref.shape[0]

    @pl.when(pl.program_id(0) % tiles_per_seq == 0)
    def _seq_start():
        tail_ref[...] = jnp.zeros(tail_ref.shape, F32)

    x = x_ref[...]
    h = x * lax.rsqrt(jnp.mean(x * x, axis=-1, keepdims=True) + EPS) * nw_ref[...]
    hb = h.astype(BF16)
    for j in (0, 3, 1, 4, 2, 5, 6, 7):
        blk = _dot(hb, wm_ref[:, j * QK:(j + 1) * QK])
        if j < 3:
            ext_ref[j, 0:SUBLANES, :] = tail_ref[j]
            ext_ref[j, SUBLANES:SUBLANES + tm, :] = blk
            tail_ref[j] = blk[tm - SUBLANES:tm, :]
            w = cw_ref[:, j * QK:(j + 1) * QK]
            y = w[0:1, :] * ext_ref[j, pl.ds(SUBLANES - 3, tm), :]
            for jj in range(1, CONV_K):
                y = y + w[jj:jj + 1, :] * ext_ref[j, pl.ds(SUBLANES - 3 + jj, tm), :]
            blk = y * _sigmoid(y)
        if j < 2:
            scale = DH ** -0.5 if j == 0 else 1.0
            blk = jnp.concatenate(
                [blk[:, hh * DH:(hh + 1) * DH]
                 * (lax.rsqrt(jnp.sum(blk[:, hh * DH:(hh + 1) * DH] ** 2, axis=-1, keepdims=True) + EPS) * scale)
                 for hh in range(HEADS)], axis=1)
        pm_ref[:, j * QK:(j + 1) * QK] = blk.astype(BF16)
    hl = (h - hb.astype(F32)).astype(BF16)
    sm = _dot(hb, wsh_ref[...]) + _dot(hb, wsl_ref[...]) + _dot(hl, wsh_ref[...])
    lane = lax.broadcasted_iota(I32, (tm, LANES), 1)
    row = lax.broadcasted_iota(I32, (tm, LANES), 0)
    beta = _sigmoid(sm)
    g = -jnp.exp(alog_ref[...]) * _softplus(sm + dtb_ref[...])
    rin = row & (CHUNK - 1)
    gc = g
    d = 1
    while d < CHUNK:
        gc = gc + jnp.where(rin >= d, pltpu.roll(gc, d, axis=0), 0.0)
        d *= 2
    out = jnp.where(lane < HEADS, beta, jnp.where(lane < 2 * HEADS, gc, 0.0))
    gcol_ref[...] = out
    tr = out.T
    for hh in range(HEADS):
        for c in range(tm // CHUNK):
            grow_ref[hh, c, 0:1, :] = tr[hh:hh + 1, c * CHUNK:(c + 1) * CHUNK]
            grow_ref[hh, c, 1:2, :] = tr[HEADS + hh:HEADS + hh + 1, c * CHUNK:(c + 1) * CHUNK]


def _proj(x2, attn_norm, w_main, ws_hi, ws_lo, alog_l, dtb_l, conv_a, seq):
    t = x2.shape[0]
    tm = TM_PROJ
    return pl.pallas_call(
        functools.partial(_proj_kernel, tiles_per_seq=seq // tm),
        grid=(t // tm,),
        in_specs=[
            pl.BlockSpec((tm, D_MODEL), lambda i: (i, 0)),
            pl.BlockSpec((1, D_MODEL), lambda i: (0, 0)),
            pl.BlockSpec((D_MODEL, N_MAIN), lambda i: (0, 0)),
            pl.BlockSpec((D_MODEL, LANES), lambda i: (0, 0)),
            pl.BlockSpec((D_MODEL, LANES), lambda i: (0, 0)),
            pl.BlockSpec((1, LANES), lambda i: (0, 0)),
            pl.BlockSpec((1, LANES), lambda i: (0, 0)),
            pl.BlockSpec((CONV_K, 3 * QK), lambda i: (0, 0)),
        ],
        out_specs=[
            pl.BlockSpec((tm, N_MAIN), lambda i: (i, 0)),
            pl.BlockSpec((tm, LANES), lambda i: (i, 0)),
            pl.BlockSpec((HEADS, tm // CHUNK, 2, CHUNK), lambda i: (0, i, 0, 0)),
        ],
        out_shape=[
            jax.ShapeDtypeStruct((t, N_MAIN), BF16),
            jax.ShapeDtypeStruct((t, LANES), F32),
            jax.ShapeDtypeStruct((HEADS, t // CHUNK, 2, CHUNK), F32),
        ],
        scratch_shapes=[
            pltpu.VMEM((3, tm + SUBLANES, QK), F32),
            pltpu.VMEM((3, SUBLANES, QK), F32),
        ],
        compiler_params=pltpu.CompilerParams(
            dimension_semantics=("arbitrary",), vmem_limit_bytes=VMEM_LIMIT),
        name="proj",
    )(x2, attn_norm, w_main, ws_hi, ws_lo, alog_l, dtb_l, conv_a)


def _gdn_kernel(q_ref, k_ref, v_ref, gate_ref, gcol_ref, grow_ref, nw_ref, o_ref, st_ref):
    @pl.when(pl.program_id(1) == 0)
    def _init():
        st_ref[...] = jnp.zeros(st_ref.shape, F32)

    q = q_ref[...].astype(F32)
    k = k_ref[...].astype(F32)
    v = v_ref[...].astype(F32)
    gblk = gcol_ref[...]
    gate = gate_ref[...].astype(F32)
    nw = nw_ref[...]
    o = _gdn_heads(q, k, v, gblk, grow_ref[...], st_ref)
    outs = []
    for h in range(HEADS):
        sl = slice(h * DH, (h + 1) * DH)
        oh = o[:, sl]
        oh = oh * lax.rsqrt(jnp.mean(oh * oh, axis=-1, keepdims=True) + EPS) * nw
        outs.append(oh * (gate[:, sl] * _sigmoid(gate[:, sl])))
    o_ref[...] = jnp.concatenate(outs, axis=1).astype(o_ref.dtype)


def _gdn_heads(q, k, v, gblk, grow, st_ref):
    ts = q.shape[0]
    ncl = ts // CHUNK
    nc = HEADS * ncl

    def per_head(f):
        return jnp.concatenate([f(h, slice(h * DH, (h + 1) * DH)) for h in range(HEADS)], axis=0)

    lane = lax.broadcasted_iota(I32, (ts, LANES), 1)
    q3 = per_head(lambda h, sl: q[:, sl].reshape(ncl, CHUNK, DH))
    k3 = per_head(lambda h, sl: k[:, sl].reshape(ncl, CHUNK, DH))
    v3 = per_head(lambda h, sl: v[:, sl].reshape(ncl, CHUNK, DH))
    bc = per_head(lambda h, sl: jnp.sum(jnp.where(lane == h, gblk, 0.0), axis=-1, keepdims=True
                                        ).reshape(ncl, CHUNK, 1))
    gcc = per_head(lambda h, sl: jnp.sum(jnp.where(lane == h + HEADS, gblk, 0.0), axis=-1, keepdims=True
                                         ).reshape(ncl, CHUNK, 1))
    gr = grow.reshape(nc, 2, CHUNK)
    beta_r = gr[:, 0:1, :]
    gc_r = gr[:, 1:2, :]
    gl = gcc[:, CHUNK - 1:CHUNK, :]

    ii = lax.broadcasted_iota(I32, (nc, CHUNK, CHUNK), 1)
    jj = lax.broadcasted_iota(I32, (nc, CHUNK, CHUNK), 2)
    causal = ii >= jj
    strict = ii > jj
    dec = jnp.where(causal, jnp.exp(jnp.where(causal, gcc - gc_r, 0.0)), 0.0)

    k16 = k3.astype(BF16)
    q16 = q3.astype(BF16)
    v16 = v3.astype(BF16)
    gram = _bmm_nt(k16, k16)
    qk = _bmm_nt(q16, k16)
    a = jnp.where(strict, gram * dec * bc, 0.0)

    eye = (ii == jj).astype(F32)
    dm = eye - jnp.where((ii >> 1) == (jj >> 1), a, 0.0)
    sblk = 2
    while sblk < CHUNK:
        sh = sblk.bit_length() - 1
        off = jnp.where(((ii >> (sh + 1)) == (jj >> (sh + 1))) & ((ii >> sh) != (jj >> sh)), a, 0.0)
        d16 = dm.astype(BF16)
        t1 = _bmm(d16, off.astype(BF16))
        dm = dm - _bmm(t1.astype(BF16), d16)
        sblk *= 2

    tb = dm * beta_r
    tbe = tb * jnp.exp(gc_r)
    u = _bmm(tb.astype(BF16), v16)
    w = _bmm(tbe.astype(BF16), k16)
    attn = qk * dec
    wu = jnp.concatenate([w, u], axis=-1)
    wu16 = wu.astype(BF16)
    aw = _bmm(attn.astype(BF16), wu16)
    qe = q3 * jnp.exp(gcc) - aw[:, :, 0:DH]
    ol = aw[:, :, DH:2 * DH]
    kdec = k3 * jnp.exp(gl - gcc)

    trow = lax.broadcasted_iota(I32, (2 * CHUNK, DH), 0)
    xn = []
    for p in range(nc // 2):
        kd2 = kdec[2 * p:2 * p + 2].reshape(2 * CHUNK, DH)
        wu2 = wu16[2 * p:2 * p + 2].reshape(2 * CHUNK, 2 * DH)
        xn.append(_dot_tn(jnp.where(trow < CHUNK, kd2, 0.0).astype(BF16), wu2))
        xn.append(_dot_tn(jnp.where(trow >= CHUNK, kd2, 0.0).astype(BF16), wu2))

    sts = [st_ref[h] for h in range(HEADS)]
    outs = [[] for _ in range(HEADS)]
    for c in range(ncl):
        for h in range(HEADS):
            i = h * ncl + c
            s16 = sts[h].astype(BF16)
            outs[h].append(_dot(qe[i].astype(BF16), s16) + ol[i])
            sts[h] = jnp.exp(gl[i]) * sts[h] + (xn[i][:, DH:2 * DH] - _dot(xn[i][:, 0:DH].astype(BF16), s16))
    for h in range(HEADS):
        st_ref[h] = sts[h]
    return jnp.concatenate([jnp.concatenate(outs[h], axis=0) for h in range(HEADS)], axis=1)


def _gdn(pm, gcol, grow, norm_a, batch, seq):
    ts = TS_GDN
    ns = seq // ts
    nc = ts // CHUNK
    t = batch * seq

    def col(cb):
        return pl.BlockSpec((ts, QK), lambda b, s, cb=cb: (b * ns + s, cb))

    return pl.pallas_call(
        _gdn_kernel,
        grid=(batch, ns),
        in_specs=[
            col(0), col(1), col(2), col(3),
            pl.BlockSpec((ts, LANES), lambda b, s: (b * ns + s, 0)),
            pl.BlockSpec((HEADS, nc, 2, CHUNK), lambda b, s: (0, b * ns + s, 0, 0)),
            pl.BlockSpec((1, DH), lambda b, s: (0, 0)),
        ],
        out_specs=pl.BlockSpec((ts, QK), lambda b, s: (b * ns + s, 0)),
        out_shape=jax.ShapeDtypeStruct((t, QK), BF16),
        scratch_shapes=[pltpu.VMEM((HEADS, DH, DH), F32)],
        compiler_params=pltpu.CompilerParams(
            dimension_semantics=("arbitrary", "arbitrary"), vmem_limit_bytes=VMEM_LIMIT),
        name="gdn",
    )(pm, pm, pm, pm, gcol, grow, norm_a)


def _ret_kernel(q_ref, k_ref, v_ref, gate_ref, cos_ref, sin_ref, nw_ref, o_ref, st_ref):
    c = q_ref.shape[0]
    s = pl.program_id(1)

    @pl.when(s == 0)
    def _init():
        st_ref[...] = jnp.zeros(st_ref.shape, F32)

    cosf = cos_ref[...]
    sinf = sin_ref[...]
    ii = lax.broadcasted_iota(I32, (c, c), 0)
    jj = lax.broadcasted_iota(I32, (c, c), 1)
    rel = (ii - jj).astype(F32)
    pos = lax.broadcasted_iota(I32, (c, 1), 0).astype(F32)
    qa = q_ref[...].astype(F32)
    ka = k_ref[...].astype(F32)
    va = v_ref[...]
    gate = gate_ref[...].astype(F32)
    nw = nw_ref[...]
    outs = []
    for h in range(HEADS):
        sl = slice(h * DH, (h + 1) * DH)
        lg = _LOG_GAMMA[h]
        q = qa[:, sl] * cosf + pltpu.roll(qa[:, sl], DH // 2, axis=1) * sinf
        k = (ka[:, sl] * cosf + pltpu.roll(ka[:, sl], DH // 2, axis=1) * sinf) * (DH ** -0.5)
        v16 = va[:, sl]
        dec = jnp.where(rel >= 0, jnp.exp(jnp.maximum(rel, 0.0) * lg), 0.0)
        k16 = k.astype(BF16)
        attn = _dot_nt(q.astype(BF16), k16) * dec
        inner = _dot(attn.astype(BF16), v16)
        st = st_ref[h]
        qd = q * jnp.exp((pos + 1.0) * lg)
        cross = _dot(qd.astype(BF16), st.astype(BF16))
        kd = k * jnp.exp((c - 1.0 - pos) * lg)
        st_ref[h] = st * math.exp(c * lg) + _dot_tn(kd.astype(BF16), v16)
        o = cross + inner
        o = o * lax.rsqrt(jnp.mean(o * o, axis=-1, keepdims=True) + EPS) * nw[:, sl]
        outs.append(o * (gate[:, sl] * _sigmoid(gate[:, sl])))
    o_ref[...] = jnp.concatenate(outs, axis=1).astype(o_ref.dtype)


def _ret(pm, cosf, sinf, norm_b, batch, seq):
    ts = TS_RET
    ns = seq // ts
    t = batch * seq

    def col(cb):
        return pl.BlockSpec((ts, QK), lambda b, s, cb=cb: (b * ns + s, 4 + cb))

    return pl.pallas_call(
        _ret_kernel,
        grid=(batch, ns),
        in_specs=[
            col(0), col(1), col(2), col(3),
            pl.BlockSpec((ts, DH), lambda b, s: (s, 0)),
            pl.BlockSpec((ts, DH), lambda b, s: (s, 0)),
            pl.BlockSpec((1, QK), lambda b, s: (0, 0)),
        ],
        out_specs=pl.BlockSpec((ts, QK), lambda b, s: (b * ns + s, 0)),
        out_shape=jax.ShapeDtypeStruct((t, QK), BF16),
        scratch_shapes=[pltpu.VMEM((HEADS, DH, DH), F32)],
        compiler_params=pltpu.CompilerParams(
            dimension_semantics=("arbitrary", "arbitrary"), vmem_limit_bytes=VMEM_LIMIT),
        name="ret",
    )(pm, pm, pm, pm, cosf, sinf, norm_b)


def _mix_kernel(x_ref, oa_ref, ob_ref, wo_ref, nw_ref, wrh_ref, wrl_ref,
                x1t_ref, h2t_ref, rr_ref, cnt_ref, carry_ref):
    tm = x_ref.shape[0]
    i = pl.program_id(0)
    per = TS_MOE // tm

    @pl.when(i % per == 0)
    def _init():
        carry_ref[...] = jnp.zeros((1, LANES), F32)

    x1 = x_ref[...] + _dot(oa_ref[...], wo_ref[0:QK, :]) + _dot(ob_ref[...], wo_ref[QK:2 * QK, :])
    h2 = x1 * lax.rsqrt(jnp.mean(x1 * x1, axis=-1, keepdims=True) + EPS) * nw_ref[...]
    for s in range(SUBLANES):
        x1t_ref[pl.ds(s, tm, stride=SUBLANES), :] = x1[:, s * LANES:(s + 1) * LANES]
        h2t_ref[pl.ds(s, tm, stride=SUBLANES), :] = h2[:, s * LANES:(s + 1) * LANES]
    hh, hl = _split_bf16(h2)
    lg = _dot(hh, wrh_ref[...]) + _dot(hh, wrl_ref[...]) + _dot(hl, wrh_ref[...])
    lane = lax.broadcasted_iota(I32, (tm, LANES), 1)
    big = jnp.int32(1 << 20)
    neg = jnp.float32(-jnp.inf)

    def argmax_first(vals, mask):
        mv = jnp.where(mask, vals, neg)
        m = jnp.max(mv, axis=-1, keepdims=True)
        idx = jnp.min(jnp.where(mask & (mv == m), lane, big), axis=-1, keepdims=True)
        return m, idx

    gmask = lane < N_GROUPS
    gm, gi = argmax_first(lg, gmask)
    gw = 1.0 / jnp.sum(jnp.where(gmask, jnp.exp(lg - gm), 0.0), axis=-1, keepdims=True)
    emask = (lane >= N_EXPERTS) & ((lane >> 3) == gi + N_EXPERTS // EPG)
    m1, i1 = argmax_first(lg, emask)
    m2, i2 = argmax_first(lg, emask & (lane != i1))
    e21 = jnp.exp(m2 - m1)
    w0 = gw / (1.0 + e21)
    w1 = gw * e21 / (1.0 + e21)
    e0 = i1 - N_EXPERTS
    e1 = i2 - N_EXPERTS

    oh = ((lane == e0) | (lane == e1)).astype(BF16)
    ri = lax.broadcasted_iota(I32, (tm, tm), 0)
    ci = lax.broadcasted_iota(I32, (tm, tm), 1)
    lstrict = (ri > ci).astype(BF16)
    pref = _dot(lstrict, oh) + carry_ref[...]
    r0 = jnp.sum(jnp.where(lane == e0, pref, 0.0), axis=-1, keepdims=True)
    r1 = jnp.sum(jnp.where(lane == e1, pref, 0.0), axis=-1, keepdims=True)
    carry = carry_ref[...] + jnp.sum(oh.astype(F32), axis=0, keepdims=True)
    carry_ref[...] = carry
    cnt_ref[...] = jnp.broadcast_to(carry, (SUBLANES, LANES)).astype(I32)

    rmat = jnp.where(lane == 0, e0.astype(F32) * RANK_RADIX + r0,
                     jnp.where(lane == 1, e1.astype(F32) * RANK_RADIX + r1,
                               jnp.where(lane == 2, w0, jnp.where(lane == 3, w1, 0.0))))
    k = i % per
    rr_ref[k] = rmat.T[0:SUBLANES, :]

    @pl.when(k == per - 1)
    def _slots():
        lane8 = lax.broadcasted_iota(I32, (SUBLANES, LANES), 1)
        cnt8 = jnp.broadcast_to(carry, (SUBLANES, LANES))
        inc = cnt8
        d = 1
        while d < LANES:
            inc = inc + jnp.where(lane8 >= d, pltpu.roll(inc, d, axis=1), 0.0)
            d *= 2
        base = inc - cnt8
        pk = rr_ref[:, 0:TOP_K, :]
        ex = jnp.floor(pk * (1.0 / RANK_RADIX))
        slot = pk - ex * RANK_RADIX
        for kk in range(N_EXPERTS):
            bk = jnp.sum(jnp.where(lane8 == kk, base, 0.0), axis=-1, keepdims=True)[0:1, :]
            slot = slot + jnp.where(ex == kk, bk, 0.0)
        rr_ref[:, 0:TOP_K, :] = slot


def _mix(x2, oa, ob, w_out16, ffn_norm, wr_hi, wr_lo):
    t = x2.shape[0]
    tm = TM_MIX
    per = TS_MOE // tm
    return pl.pallas_call(
        _mix_kernel,
        grid=(t // tm,),
        in_specs=[
            pl.BlockSpec((tm, D_MODEL), lambda i: (i, 0)),
            pl.BlockSpec((tm, QK), lambda i: (i, 0)),
            pl.BlockSpec((tm, QK), lambda i: (i, 0)),
            pl.BlockSpec((D_MODEL, D_MODEL), lambda i: (0, 0)),
            pl.BlockSpec((1, D_MODEL), lambda i: (0, 0)),
            pl.BlockSpec((D_MODEL, LANES), lambda i: (0, 0)),
            pl.BlockSpec((D_MODEL, LANES), lambda i: (0, 0)),
        ],
        out_specs=[
            pl.BlockSpec((tm * SUBLANES, LANES), lambda i: (i, 0)),
            pl.BlockSpec((tm * SUBLANES, LANES), lambda i: (i, 0)),
            pl.BlockSpec((per, SUBLANES, tm), lambda i: (i // per, 0, 0)),
            pl.BlockSpec((None, SUBLANES, LANES), lambda i: (i // per, 0, 0)),
        ],
        out_shape=[
            jax.ShapeDtypeStruct((t * SUBLANES, LANES), F32),
            jax.ShapeDtypeStruct((t * SUBLANES, LANES), F32),
            jax.ShapeDtypeStruct((t // tm, SUBLANES, tm), F32),
            jax.ShapeDtypeStruct((t // TS_MOE, SUBLANES, LANES), I32),
        ],
        scratch_shapes=[pltpu.VMEM((1, LANES), F32)],
        compiler_params=pltpu.CompilerParams(
            dimension_semantics=("arbitrary",), vmem_limit_bytes=VMEM_LIMIT),
        name="mix",
    )(x2, oa, ob, w_out16, ffn_norm, wr_hi, wr_lo)


def _moe_kernel(ps0_ref, ps1_ref, pc0_ref, pc1_ref, w0_ref, w1_ref, cnt_ref,
                h2t_ref, x1t_ref, wg_ref, wu_ref, wd_ref, fnw_ref,
                out_ref, stag_ref, acc_ref, base_ref):
    j = pl.program_id(1)
    sub = out_ref.shape[0]
    nsub = TS_MOE // sub
    npair = N_EXPERTS // EXPERTS_PER_STEP
    def row8(r):
        return pl.ds(pl.multiple_of(r * SUBLANES, SUBLANES), SUBLANES)

    @pl.when(j == 0)
    def _bases():
        stag_ref[pl.ds(TOP_K * TS_MOE * SUBLANES, SLACK_MOE * SUBLANES), :] = jnp.zeros(
            (SLACK_MOE * SUBLANES, LANES), F32)

        def body(e, run):
            base_ref[e] = run
            return run + cnt_ref[e]

        lax.fori_loop(0, N_EXPERTS, body, jnp.int32(0))

    @pl.when(j < nsub)
    def _scatter():
        def body(t, carry):
            row = h2t_ref[row8(t), :]
            stag_ref[row8(ps0_ref[t]), :] = row
            stag_ref[row8(ps1_ref[t]), :] = row
            return carry

        lax.fori_loop(0, sub, body, 0, unroll=8)

    def load_rows(r0, m):
        xs = [stag_ref[pl.ds(r0 + s, m, stride=SUBLANES), :] for s in range(SUBLANES)]
        return xs, jnp.concatenate([v.astype(BF16) for v in xs], axis=1)

    def store_rows(r0, m, xs, y, nvalid):
        keep = lax.broadcasted_iota(I32, (m, LANES), 0) < nvalid
        for s in range(SUBLANES):
            stag_ref[pl.ds(r0 + s, m, stride=SUBLANES), :] = jnp.where(
                keep, y[:, s * LANES:(s + 1) * LANES], xs[s])

    @pl.when((j >= nsub) & (j < nsub + npair))
    def _experts():
        qs = range(EXPERTS_PER_STEP)
        es = [(j - nsub) * EXPERTS_PER_STEP + q for q in qs]
        ns = [cnt_ref[e] for e in es]
        bs = [base_ref[e] for e in es]
        m = CH_FIRST_MOE
        r0s = [pl.multiple_of(b * SUBLANES, SUBLANES) for b in bs]
        ld = [load_rows(r0, m) for r0 in r0s]
        gs = [_dot(ld[q][1], wg_ref[q]) for q in qs]
        us = [_dot(ld[q][1], wu_ref[q]) for q in qs]
        hs = [((gs[q] * _sigmoid(gs[q])) * us[q]).astype(BF16) for q in qs]
        ys = [_dot(hs[q], wd_ref[q]) for q in qs]
        for q in qs:
            store_rows(r0s[q], m, ld[q][0], ys[q], ns[q])
        ch = CH_MOE
        for q in qs:
            def chunk(c, carry, q=q):
                r0 = pl.multiple_of((bs[q] + m + c * ch) * SUBLANES, SUBLANES)
                xs, x16 = load_rows(r0, ch)
                g = _dot(x16, wg_ref[q])
                u = _dot(x16, wu_ref[q])
                y = _dot(((g * _sigmoid(g)) * u).astype(BF16), wd_ref[q])
                store_rows(r0, ch, xs, y, ns[q] - m - c * ch)
                return carry

            lax.fori_loop(0, (jnp.maximum(ns[q] - m, 0) + ch - 1) // ch, chunk, 0)

    @pl.when(j >= nsub + npair)
    def _combine():
        def body(t, carry):
            acc_ref[row8(t), :] = (x1t_ref[row8(t), :]
                                   + w0_ref[t] * stag_ref[row8(pc0_ref[t]), :]
                                   + w1_ref[t] * stag_ref[row8(pc1_ref[t]), :])
            return carry

        lax.fori_loop(0, sub, body, 0, unroll=8)
        xo = jnp.concatenate([acc_ref[pl.ds(s, sub, stride=SUBLANES), :] for s in range(SUBLANES)], axis=1)
        out_ref[...] = xo * lax.rsqrt(jnp.mean(xo * xo, axis=-1, keepdims=True) + EPS) * fnw_ref[...]


def _moe(p0, p1, w0, w1, cnt, h2t, x1t, wg16, wu16, wd16, final_norm):
    t = p0.shape[0]
    sub = SUB_MOE
    nsub = TS_MOE // sub
    npair = N_EXPERTS // EXPERTS_PER_STEP
    nt = t // TS_MOE
    smem = functools.partial(pl.BlockSpec, memory_space=pltpu.SMEM)

    def scat(i, j):
        return i * nsub + jnp.minimum(j, nsub - 1)

    def comb(i, j):
        return i * nsub + jnp.clip(j - nsub - npair, 0, nsub - 1)

    def pair(i, j):
        return jnp.clip(j - nsub, 0, npair - 1)

    def wspec(shape):
        return pl.BlockSpec((EXPERTS_PER_STEP,) + shape, lambda i, j: (pair(i, j), 0, 0))

    return pl.pallas_call(
        _moe_kernel,
        grid=(nt, nsub + npair + nsub),
        in_specs=[
            smem((sub,), lambda i, j: (scat(i, j),)), smem((sub,), lambda i, j: (scat(i, j),)),
            smem((sub,), lambda i, j: (comb(i, j),)), smem((sub,), lambda i, j: (comb(i, j),)),
            smem((sub,), lambda i, j: (comb(i, j),)), smem((sub,), lambda i, j: (comb(i, j),)),
            smem((LANES,), lambda i, j: (i,)),
            pl.BlockSpec((sub * SUBLANES, LANES), lambda i, j: (scat(i, j), 0)),
            pl.BlockSpec((sub * SUBLANES, LANES), lambda i, j: (comb(i, j), 0)),
            wspec((D_MODEL, D_EXPERT)), wspec((D_MODEL, D_EXPERT)), wspec((D_EXPERT, D_MODEL)),
            pl.BlockSpec((1, D_MODEL), lambda i, j: (0, 0)),
        ],
        out_specs=pl.BlockSpec((sub, D_MODEL), lambda i, j: (comb(i, j), 0)),
        out_shape=jax.ShapeDtypeStruct((t, D_MODEL), F32),
        scratch_shapes=[
            pltpu.VMEM(((TOP_K * TS_MOE + SLACK_MOE) * SUBLANES, LANES), F32),
            pltpu.VMEM((sub * SUBLANES, LANES), F32),
            pltpu.SMEM((LANES,), I32),
        ],
        compiler_params=pltpu.CompilerParams(
            dimension_semantics=("arbitrary", "arbitrary"), vmem_limit_bytes=VMEM_LIMIT_MOE),
        name="moe",
    )(p0, p1, p0, p1, w0, w1, cnt, h2t, x1t, wg16, wu16, wd16, final_norm)


def _pad_lanes(a, start):
    rows, n = a.shape
    return jnp.pad(a, ((0, 0), (start, LANES - start - n)))


def kernel(x, attn_norm, w_in, conv_a, a_log, dt_bias, norm_a, norm_b, w_out, ffn_norm,
           w_router_group, w_router_expert, w_gate, w_up, w_down, final_norm):
    batch, seq, d = x.shape
    t = batch * seq
    x2 = x.reshape(t, d)
    assert attn_norm.shape[0] == 1, "the closing RMSNorm is fused into the single layer's MoE combine step"

    pos = jnp.arange(seq, dtype=F32)
    inv_freq = ROPE_BASE ** (-jnp.arange(0, DH, 2, dtype=F32) / DH)
    ang = pos[:, None] * inv_freq[None, :]
    cos, sin = jnp.cos(ang), jnp.sin(ang)
    cosf = jnp.concatenate([cos, cos], axis=-1)
    sinf = jnp.concatenate([-sin, sin], axis=-1)

    wl = w_in[0]
    w_main = jnp.concatenate([wl[:, 0:4 * QK], wl[:, 4 * QK + 2 * HEADS:]], axis=1).astype(BF16)
    ws_hi, ws_lo = _split_bf16(_pad_lanes(wl[:, 4 * QK:4 * QK + 2 * HEADS], 0))
    alog_l = _pad_lanes(a_log[0][None, :], HEADS)
    dtb_l = _pad_lanes(dt_bias[0][None, :], HEADS)
    pm, gcol, grow = _proj(x2, attn_norm[0][None, :], w_main, ws_hi, ws_lo, alog_l, dtb_l, conv_a[0], seq)

    oa = _gdn(pm, gcol, grow, norm_a[0][None, :], batch, seq)
    ob = _ret(pm, cosf, sinf, norm_b[0][None, :], batch, seq)

    wr = jnp.concatenate(
        [_pad_lanes(w_router_group[0], 0)[:, 0:N_EXPERTS], w_router_expert[0]], axis=1)
    wr_hi, wr_lo = _split_bf16(wr)
    x1t, h2t, rr, cnt = _mix(x2, oa, ob, w_out[0].astype(BF16), ffn_norm[0][None, :], wr_hi, wr_lo)

    p0 = rr[:, 0, :].reshape(-1).astype(I32)
    p1 = rr[:, 1, :].reshape(-1).astype(I32)
    out = _moe(p0, p1, rr[:, 2, :].reshape(-1), rr[:, 3, :].reshape(-1), cnt[:, 0, :].reshape(-1), h2t, x1t,
               w_gate[0].astype(BF16), w_up[0].astype(BF16), w_down[0].astype(BF16), final_norm[None, :])
    return out.reshape(batch, seq, d)
```

```python
import functools
import math

import jax
import jax.numpy as jnp
from jax import lax
from jax.experimental import pallas as pl
from jax.experimental.pallas import tpu as pltpu

F32 = jnp.float32
BF16 = jnp.bfloat16
I32 = jnp.int32

D_MODEL = 1024
HEADS = 4
DH = 128
CONV_K = 4
CHUNK = 64
ROPE_BASE = 10000.0
N_GROUPS = 8
EPG = 8
N_EXPERTS = N_GROUPS * EPG
TOP_K = 2
D_EXPERT = 256
EPS = 1e-6
QK = HEADS * DH
N_MAIN = 8 * QK
LANES = 128
SUBLANES = 8

TM_PROJ = 512
TS_MIXERS = 512
CHUNK_RET = 256
TM_MIX = 256
TS_MOE = 4096
SUB_MOE = 256
CH_FIRST_MOE = 160
CH_MOE = 128
SLACK_MOE = CH_FIRST_MOE + CH_MOE
EXPERTS_PER_STEP = 4
RANK_BITS = 16
RANK_RADIX = 1 << RANK_BITS
VMEM_LIMIT = 48 * 1024 * 1024
VMEM_LIMIT_MOE = 58 * 1024 * 1024
_LOG_GAMMA = tuple(math.log(1.0 - 2.0 ** (-5.0 - hh)) for hh in range(HEADS))


def _sigmoid(x):
    return 1.0 / (1.0 + jnp.exp(-x))


def _softplus(x):
    return jnp.maximum(x, 0.0) + jnp.log1p(jnp.exp(-jnp.abs(x)))


def _split_bf16(a):
    hi = a.astype(BF16)
    lo = (a - hi.astype(F32)).astype(BF16)
    return hi, lo


def _dot(a, b):
    return jnp.dot(a, b, preferred_element_type=F32)


def _dot_nt(a, b):
    return lax.dot_general(a, b, (((1,), (1,)), ((), ())), preferred_element_type=F32)


def _dot_tn(a, b):
    return lax.dot_general(a, b, (((0,), (0,)), ((), ())), preferred_element_type=F32)


def _bmm(a, b):
    return jnp.stack([_dot(a[c], b[c]) for c in range(a.shape[0])])


def _bmm_nt(a, b):
    return jnp.stack([_dot_nt(a[c], b[c]) for c in range(a.shape[0])])


def _proj_kernel(x_ref, nw_ref, wm_ref, wsh_ref, wsl_ref, alog_ref, dtb_ref, cw_ref,
                 pm_ref, gcol_ref, grow_ref, tail_ref, *, tiles_per_seq):
    tm = x_ref.shape[0]

    @pl.when(pl.program_id(0) % tiles_per_seq == 0)
    def _seq_start():
        tail_ref[...] = jnp.zeros(tail_ref.shape, F32)

    x = x_ref[...]
    h = x * lax.rsqrt(jnp.mean(x * x, axis=-1, keepdims=True) + EPS) * nw_ref[...]
    hb = h.astype(BF16)
    for j in (0, 3, 1, 4, 2, 5, 6, 7):
        blk = _dot(hb, wm_ref[:, j * QK:(j + 1) * QK])
        if j < 3:
            ext = jnp.concatenate([tail_ref[j], blk], axis=0)
            tail_ref[j] = blk[tm - SUBLANES:tm, :]
            w = cw_ref[:, j * QK:(j + 1) * QK]
            y = w[CONV_K - 1:CONV_K, :] * blk
            for jj in range(CONV_K - 1):
                y = y + w[jj:jj + 1, :] * pltpu.roll(ext, CONV_K - 1 - jj, axis=0)[SUBLANES:, :]
            blk = y * _sigmoid(y)
        if j < 2:
            scale = DH ** -0.5 if j == 0 else 1.0
            blk = jnp.concatenate(
                [blk[:, hh * DH:(hh + 1) * DH]
                 * (lax.rsqrt(jnp.sum(blk[:, hh * DH:(hh + 1) * DH] ** 2, axis=-1, keepdims=True) + EPS) * scale)
                 for hh in range(HEADS)], axis=1)
        pm_ref[:, j * QK:(j + 1) * QK] = blk.astype(BF16)
    hl = (h - hb.astype(F32)).astype(BF16)
    sm = _dot(hb, wsh_ref[...]) + _dot(hb, wsl_ref[...]) + _dot(hl, wsh_ref[...])
    lane = lax.broadcasted_iota(I32, (tm, LANES), 1)
    row = lax.broadcasted_iota(I32, (tm, LANES), 0)
    beta = _sigmoid(sm)
    g = -jnp.exp(alog_ref[...]) * _softplus(sm + dtb_ref[...])
    rin = row & (CHUNK - 1)
    gc = g
    d = 1
    while d < CHUNK:
        gc = gc + jnp.where(rin >= d, pltpu.roll(gc, d, axis=0), 0.0)
        d *= 2
    out = jnp.where(lane < HEADS, beta, jnp.where(lane < 2 * HEADS, gc, 0.0))
    gcol_ref[...] = out
    tr = out.T
    for hh in range(HEADS):
        for c in range(tm // CHUNK):
            grow_ref[hh, c, 0:1, :] = tr[hh:hh + 1, c * CHUNK:(c + 1) * CHUNK]
            grow_ref[hh, c, 1:2, :] = tr[HEADS + hh:HEADS + hh + 1, c * CHUNK:(c + 1) * CHUNK]


def _proj(x2, attn_norm, w_main, ws_hi, ws_lo, alog_l, dtb_l, conv_a, seq):
    t = x2.shape[0]
    tm = TM_PROJ
    return pl.pallas_call(
        functools.partial(_proj_kernel, tiles_per_seq=seq // tm),
        grid=(t // tm,),
        in_specs=[
            pl.BlockSpec((tm, D_MODEL), lambda i: (i, 0)),
            pl.BlockSpec((1, D_MODEL), lambda i: (0, 0)),
            pl.BlockSpec((D_MODEL, N_MAIN), lambda i: (0, 0)),
            pl.BlockSpec((D_MODEL, LANES), lambda i: (0, 0)),
            pl.BlockSpec((D_MODEL, LANES), lambda i: (0, 0)),
            pl.BlockSpec((1, LANES), lambda i: (0, 0)),
            pl.BlockSpec((1, LANES), lambda i: (0, 0)),
            pl.BlockSpec((CONV_K, 3 * QK), lambda i: (0, 0)),
        ],
        out_specs=[
            pl.BlockSpec((tm, N_MAIN), lambda i: (i, 0)),
            pl.BlockSpec((tm, LANES), lambda i: (i, 0)),
            pl.BlockSpec((HEADS, tm // CHUNK, 2, CHUNK), lambda i: (0, i, 0, 0)),
        ],
        out_shape=[
            jax.ShapeDtypeStruct((t, N_MAIN), BF16),
            jax.ShapeDtypeStruct((t, LANES), F32),
            jax.ShapeDtypeStruct((HEADS, t // CHUNK, 2, CHUNK), F32),
        ],
        scratch_shapes=[pltpu.VMEM((3, SUBLANES, QK), F32)],
        compiler_params=pltpu.CompilerParams(
            dimension_semantics=("arbitrary",), vmem_limit_bytes=VMEM_LIMIT),
        name="proj",
    )(x2, attn_norm, w_main, ws_hi, ws_lo, alog_l, dtb_l, conv_a)


def _gated_norm(o, gate, nw):
    outs = []
    for h in range(HEADS):
        sl = slice(h * DH, (h + 1) * DH)
        oh = o[:, sl]
        wh = nw if nw.shape[1] == DH else nw[:, sl]
        oh = oh * lax.rsqrt(jnp.mean(oh * oh, axis=-1, keepdims=True) + EPS) * wh
        outs.append(oh * (gate[:, sl] * _sigmoid(gate[:, sl])))
    return jnp.concatenate(outs, axis=1)


def _gdn_heads(q, k, v, gblk, grow, st_ref):
    ts = q.shape[0]
    ncl = ts // CHUNK
    nc = HEADS * ncl

    def per_head(f):
        return jnp.concatenate([f(h, slice(h * DH, (h + 1) * DH)) for h in range(HEADS)], axis=0)

    lane = lax.broadcasted_iota(I32, (ts, LANES), 1)
    q3 = per_head(lambda h, sl: q[:, sl].reshape(ncl, CHUNK, DH))
    k3 = per_head(lambda h, sl: k[:, sl].reshape(ncl, CHUNK, DH))
    v3 = per_head(lambda h, sl: v[:, sl].reshape(ncl, CHUNK, DH))
    bc = per_head(lambda h, sl: jnp.sum(jnp.where(lane == h, gblk, 0.0), axis=-1, keepdims=True
                                        ).reshape(ncl, CHUNK, 1))
    gcc = per_head(lambda h, sl: jnp.sum(jnp.where(lane == h + HEADS, gblk, 0.0), axis=-1, keepdims=True
                                         ).reshape(ncl, CHUNK, 1))
    gr = grow.reshape(nc, 2, CHUNK)
    beta_r = gr[:, 0:1, :]
    gc_r = gr[:, 1:2, :]
    gl = gcc[:, CHUNK - 1:CHUNK, :]

    ii = lax.broadcasted_iota(I32, (nc, CHUNK, CHUNK), 1)
    jj = lax.broadcasted_iota(I32, (nc, CHUNK, CHUNK), 2)
    causal = ii >= jj
    strict = ii > jj
    dec = jnp.where(causal, jnp.exp(jnp.where(causal, gcc - gc_r, 0.0)), 0.0)

    k16 = k3.astype(BF16)
    q16 = q3.astype(BF16)
    v16 = v3.astype(BF16)
    gram = _bmm_nt(k16, k16)
    qk = _bmm_nt(q16, k16)
    a = jnp.where(strict, gram * dec * bc, 0.0)

    eye = (ii == jj).astype(F32)
    dm = eye - jnp.where((ii >> 1) == (jj >> 1), a, 0.0)
    sblk = 2
    while sblk < CHUNK:
        sh = sblk.bit_length() - 1
        off = jnp.where(((ii >> (sh + 1)) == (jj >> (sh + 1))) & ((ii >> sh) != (jj >> sh)), a, 0.0)
        d16 = dm.astype(BF16)
        t1 = _bmm(d16, off.astype(BF16))
        dm = dm - _bmm(t1.astype(BF16), d16)
        sblk *= 2

    tb = dm * beta_r
    tbe = tb * jnp.exp(gc_r)
    u = _bmm(tb.astype(BF16), v16)
    w = _bmm(tbe.astype(BF16), k16)
    attn = qk * dec
    wu = jnp.concatenate([w, u], axis=-1)
    wu16 = wu.astype(BF16)
    aw = _bmm(attn.astype(BF16), wu16)
    qe = q3 * jnp.exp(gcc) - aw[:, :, 0:DH]
    ol = aw[:, :, DH:2 * DH]
    kdec = k3 * jnp.exp(gl - gcc)

    trow = lax.broadcasted_iota(I32, (2 * CHUNK, DH), 0)
    xn = []
    for p in range(nc // 2):
        kd2 = kdec[2 * p:2 * p + 2].reshape(2 * CHUNK, DH)
        wu2 = wu16[2 * p:2 * p + 2].reshape(2 * CHUNK, 2 * DH)
        xn.append(_dot_tn(jnp.where(trow < CHUNK, kd2, 0.0).astype(BF16), wu2))
        xn.append(_dot_tn(jnp.where(trow >= CHUNK, kd2, 0.0).astype(BF16), wu2))

    sts = [st_ref[h] for h in range(HEADS)]
    outs = [[] for _ in range(HEADS)]
    for c in range(ncl):
        for h in range(HEADS):
            i = h * ncl + c
            s16 = sts[h].astype(BF16)
            outs[h].append(_dot(qe[i].astype(BF16), s16) + ol[i])
            sts[h] = jnp.exp(gl[i]) * sts[h] + (xn[i][:, DH:2 * DH] - _dot(xn[i][:, 0:DH].astype(BF16), s16))
    for h in range(HEADS):
        st_ref[h] = sts[h]
    return jnp.concatenate([jnp.concatenate(outs[h], axis=0) for h in range(HEADS)], axis=1)


def _ret_heads(qa, ka, va, cosf, sinf, st_ref):
    rows = qa.shape[0]
    c = CHUNK_RET
    ii = lax.broadcasted_iota(I32, (c, c), 0)
    jj = lax.broadcasted_iota(I32, (c, c), 1)
    rel = (ii - jj).astype(F32)
    pos = lax.broadcasted_iota(I32, (c, 1), 0).astype(F32)
    decs = [jnp.where(rel >= 0, jnp.exp(jnp.maximum(rel, 0.0) * _LOG_GAMMA[h]), 0.0) for h in range(HEADS)]
    sts = [st_ref[h] for h in range(HEADS)]
    outs = [[] for _ in range(HEADS)]
    for ck in range(rows // c):
        rs = slice(ck * c, (ck + 1) * c)
        for h in range(HEADS):
            sl = slice(h * DH, (h + 1) * DH)
            lg = _LOG_GAMMA[h]
            q = qa[rs, sl] * cosf[rs] + pltpu.roll(qa[rs, sl], DH // 2, axis=1) * sinf[rs]
            k = (ka[rs, sl] * cosf[rs] + pltpu.roll(ka[rs, sl], DH // 2, axis=1) * sinf[rs]) * (DH ** -0.5)
            v16 = va[rs, sl]
            k16 = k.astype(BF16)
            attn = _dot_nt(q.astype(BF16), k16) * decs[h]
            inner = _dot(attn.astype(BF16), v16)
            qd = q * jnp.exp((pos + 1.0) * lg)
            cross = _dot(qd.astype(BF16), sts[h].astype(BF16))
            kd = k * jnp.exp((c - 1.0 - pos) * lg)
            sts[h] = sts[h] * math.exp(c * lg) + _dot_tn(kd.astype(BF16), v16)
            outs[h].append(cross + inner)
    for h in range(HEADS):
        st_ref[h] = sts[h]
    return jnp.concatenate([jnp.concatenate(outs[h], axis=0) for h in range(HEADS)], axis=1)


def _route(x, o16, wo_ref, nw, wrh_ref, wrl_ref, carry):
    tm = x.shape[0]
    x1 = x + _dot(o16, wo_ref[...])
    h2 = x1 * lax.rsqrt(jnp.mean(x1 * x1, axis=-1, keepdims=True) + EPS) * nw
    hh, hl = _split_bf16(h2)
    lg = _dot(hh, wrh_ref[...]) + _dot(hh, wrl_ref[...]) + _dot(hl, wrh_ref[...])
    lane = lax.broadcasted_iota(I32, (tm, LANES), 1)
    big = jnp.int32(1 << 20)
    neg = jnp.float32(-jnp.inf)

    def argmax_first(vals, mask):
        mv = jnp.where(mask, vals, neg)
        m = jnp.max(mv, axis=-1, keepdims=True)
        idx = jnp.min(jnp.where(mask & (mv == m), lane, big), axis=-1, keepdims=True)
        return m, idx

    gmask = lane < N_GROUPS
    gm, gi = argmax_first(lg, gmask)
    gw = 1.0 / jnp.sum(jnp.where(gmask, jnp.exp(lg - gm), 0.0), axis=-1, keepdims=True)
    emask = (lane >= N_EXPERTS) & ((lane >> 3) == gi + N_EXPERTS // EPG)
    m1, i1 = argmax_first(lg, emask)
    m2, i2 = argmax_first(lg, emask & (lane != i1))
    e21 = jnp.exp(m2 - m1)
    w0 = gw / (1.0 + e21)
    w1 = gw * e21 / (1.0 + e21)
    e0 = i1 - N_EXPERTS
    e1 = i2 - N_EXPERTS

    oh = ((lane == e0) | (lane == e1)).astype(BF16)
    ri = lax.broadcasted_iota(I32, (tm, tm), 0)
    ci = lax.broadcasted_iota(I32, (tm, tm), 1)
    lstrict = (ri > ci).astype(BF16)
    pref = _dot(lstrict, oh) + carry
    r0 = jnp.sum(jnp.where(lane == e0, pref, 0.0), axis=-1, keepdims=True)
    r1 = jnp.sum(jnp.where(lane == e1, pref, 0.0), axis=-1, keepdims=True)
    carry = carry + jnp.sum(oh.astype(F32), axis=0, keepdims=True)

    rmat = jnp.where(lane == 0, e0.astype(F32) * RANK_RADIX + r0,
                     jnp.where(lane == 1, e1.astype(F32) * RANK_RADIX + r1,
                               jnp.where(lane == 2, w0, jnp.where(lane == 3, w1, 0.0))))
    return x1, h2, rmat.T[0:SUBLANES, :], carry


def _finish_slots(rr_ref, carry):
    lane8 = lax.broadcasted_iota(I32, (SUBLANES, LANES), 1)
    cnt8 = jnp.broadcast_to(carry, (SUBLANES, LANES))
    inc = cnt8
    d = 1
    while d < LANES:
        inc = inc + jnp.where(lane8 >= d, pltpu.roll(inc, d, axis=1), 0.0)
        d *= 2
    base = inc - cnt8
    pk = rr_ref[:, 0:TOP_K, :]
    ex = jnp.floor(pk * (1.0 / RANK_RADIX))
    slot = pk - ex * RANK_RADIX
    for kk in range(N_EXPERTS):
        bk = jnp.sum(jnp.where(lane8 == kk, base, 0.0), axis=-1, keepdims=True)[0:1, :]
        slot = slot + jnp.where(ex == kk, bk, 0.0)
    rr_ref[:, 0:TOP_K, :] = slot


def _mixers_kernel(qa_ref, ka_ref, va_ref, ga_ref, qb_ref, kb_ref, vb_ref, gb_ref, gcol_ref, grow_ref,
                   na_ref, cos_ref, sin_ref, nb_ref, x_ref, wo_ref, fw_ref, wrh_ref, wrl_ref,
                   x1t_ref, h2t_ref, rr_ref, cnt_ref,
                   sta_ref, stb_ref, oprev_ref, carry_ref, *, tiles_per_seq, per):
    i = pl.program_id(0)
    ts = x_ref.shape[0]
    tm = TM_MIX
    prev = i + per - 1

    @pl.when(i == 0)
    def _first():
        oprev_ref[...] = jnp.zeros(oprev_ref.shape, oprev_ref.dtype)

    @pl.when(i % tiles_per_seq == 0)
    def _seq_start():
        sta_ref[...] = jnp.zeros(sta_ref.shape, F32)
        stb_ref[...] = jnp.zeros(stb_ref.shape, F32)

    @pl.when((i == 0) | (prev % per == 0))
    def _moe_tile_start():
        carry_ref[...] = jnp.zeros((1, LANES), F32)

    carry = carry_ref[...]
    nw = fw_ref[...]
    for hf in range(ts // tm):
        rows = slice(hf * tm, (hf + 1) * tm)
        x1, h2, rmt, carry = _route(x_ref[rows, :], oprev_ref[rows, :], wo_ref, nw, wrh_ref, wrl_ref, carry)
        for s in range(SUBLANES):
            x1t_ref[pl.ds(hf * tm * SUBLANES + s, tm, stride=SUBLANES), :] = x1[:, s * LANES:(s + 1) * LANES]
            h2t_ref[pl.ds(hf * tm * SUBLANES + s, tm, stride=SUBLANES), :] = h2[:, s * LANES:(s + 1) * LANES]
        rr_ref[(prev % per) * (ts // tm) + hf] = rmt
    carry_ref[...] = carry
    cnt_ref[...] = jnp.broadcast_to(carry, (SUBLANES, LANES)).astype(I32)

    oa = _gdn_heads(qa_ref[...].astype(F32), ka_ref[...].astype(F32), va_ref[...].astype(F32),
                    gcol_ref[...], grow_ref[...], sta_ref)
    ob = _ret_heads(qb_ref[...].astype(F32), kb_ref[...].astype(F32), vb_ref[...], cos_ref[...], sin_ref[...],
                    stb_ref)
    oa = _gated_norm(oa, ga_ref[...].astype(F32), na_ref[...])
    ob = _gated_norm(ob, gb_ref[...].astype(F32), nb_ref[...])
    oprev_ref[...] = jnp.concatenate([oa, ob], axis=1).astype(oprev_ref.dtype)

    @pl.when((i > 0) & (prev % per == per - 1))
    def _moe_tile_end():
        _finish_slots(rr_ref, carry)


def _mixers(pm, gcol, grow, norm_a, cosf, sinf, norm_b, x2, w_out16, ffn_norm, wr_hi, wr_lo, seq):
    t = x2.shape[0]
    ts = TS_MIXERS
    n = t // ts
    nc = ts // CHUNK
    tiles_per_seq = seq // ts
    per = TS_MOE // ts

    def cur(i):
        return jnp.minimum(i, n - 1)

    def prv(i):
        return jnp.maximum(i - 1, 0)

    def col(cb):
        return pl.BlockSpec((ts, QK), lambda i, cb=cb: (cur(i), cb))

    def const(shape):
        return pl.BlockSpec(shape, lambda i: (0,) * len(shape))

    return pl.pallas_call(
        functools.partial(_mixers_kernel, tiles_per_seq=tiles_per_seq, per=per),
        grid=(n + 1,),
        in_specs=[
            col(0), col(1), col(2), col(3), col(4), col(5), col(6), col(7),
            pl.BlockSpec((ts, LANES), lambda i: (cur(i), 0)),
            pl.BlockSpec((HEADS, nc, 2, CHUNK), lambda i: (0, cur(i), 0, 0)),
            const((1, DH)),
            pl.BlockSpec((ts, DH), lambda i: (cur(i) % tiles_per_seq, 0)),
            pl.BlockSpec((ts, DH), lambda i: (cur(i) % tiles_per_seq, 0)),
            const((1, QK)),
            pl.BlockSpec((ts, D_MODEL), lambda i: (prv(i), 0)),
            const((D_MODEL, D_MODEL)), const((1, D_MODEL)), const((D_MODEL, LANES)), const((D_MODEL, LANES)),
        ],
        out_specs=[
            pl.BlockSpec((ts * SUBLANES, LANES), lambda i: (prv(i), 0)),
            pl.BlockSpec((ts * SUBLANES, LANES), lambda i: (prv(i), 0)),
            pl.BlockSpec((TS_MOE // TM_MIX, SUBLANES, TM_MIX), lambda i: (prv(i) // per, 0, 0)),
            pl.BlockSpec((None, SUBLANES, LANES), lambda i: (prv(i) // per, 0, 0)),
        ],
        out_shape=[
            jax.ShapeDtypeStruct((t * SUBLANES, LANES), F32),
            jax.ShapeDtypeStruct((t * SUBLANES, LANES), F32),
            jax.ShapeDtypeStruct((t // TM_MIX, SUBLANES, TM_MIX), F32),
            jax.ShapeDtypeStruct((t // TS_MOE, SUBLANES, LANES), I32),
        ],
        scratch_shapes=[
            pltpu.VMEM((HEADS, DH, DH), F32),
            pltpu.VMEM((HEADS, DH, DH), F32),
            pltpu.VMEM((ts, D_MODEL), BF16),
            pltpu.VMEM((1, LANES), F32),
        ],
        compiler_params=pltpu.CompilerParams(
            dimension_semantics=("arbitrary",), vmem_limit_bytes=VMEM_LIMIT),
        name="mixers",
    )(pm, pm, pm, pm, pm, pm, pm, pm, gcol, grow, norm_a, cosf, sinf, norm_b, x2, w_out16, ffn_norm, wr_hi, wr_lo)


def _moe_kernel(ps0_ref, ps1_ref, pc0_ref, pc1_ref, w0_ref, w1_ref, cnt_ref,
                h2t_ref, x1t_ref, wg_ref, wu_ref, wd_ref, fnw_ref,
                out_ref, stag_ref, acc_ref, base_ref):
    j = pl.program_id(1)
    sub = out_ref.shape[0]
    nsub = TS_MOE // sub
    npair = N_EXPERTS // EXPERTS_PER_STEP
    def row8(r):
        return pl.ds(pl.multiple_of(r * SUBLANES, SUBLANES), SUBLANES)

    @pl.when(j == 0)
    def _bases():
        stag_ref[pl.ds(TOP_K * TS_MOE * SUBLANES, SLACK_MOE * SUBLANES), :] = jnp.zeros(
            (SLACK_MOE * SUBLANES, LANES), F32)

        def body(e, run):
            base_ref[e] = run
            return run + cnt_ref[e]

        lax.fori_loop(0, N_EXPERTS, body, jnp.int32(0))

    @pl.when(j < nsub)
    def _scatter():
        def body(t, carry):
            row = h2t_ref[row8(t), :]
            stag_ref[row8(ps0_ref[t]), :] = row
            stag_ref[row8(ps1_ref[t]), :] = row
            return carry

        lax.fori_loop(0, sub, body, 0, unroll=8)

    def load_rows(r0, m):
        xs = [stag_ref[pl.ds(r0 + s, m, stride=SUBLANES), :] for s in range(SUBLANES)]
        return xs, jnp.concatenate([v.astype(BF16) for v in xs], axis=1)

    def store_rows(r0, m, xs, y, nvalid):
        keep = lax.broadcasted_iota(I32, (m, LANES), 0) < nvalid
        for s in range(SUBLANES):
            stag_ref[pl.ds(r0 + s, m, stride=SUBLANES), :] = jnp.where(
                keep, y[:, s * LANES:(s + 1) * LANES], xs[s])

    @pl.when((j >= nsub) & (j < nsub + npair))
    def _experts():
        qs = range(EXPERTS_PER_STEP)
        es = [(j - nsub) * EXPERTS_PER_STEP + q for q in qs]
        ns = [cnt_ref[e] for e in es]
        bs = [base_ref[e] for e in es]
        m = CH_FIRST_MOE
        r0s = [pl.multiple_of(b * SUBLANES, SUBLANES) for b in bs]
        ld = [load_rows(r0, m) for r0 in r0s]
        gs = [_dot(ld[q][1], wg_ref[q]) for q in qs]
        us = [_dot(ld[q][1], wu_ref[q]) for q in qs]
        hs = [((gs[q] * _sigmoid(gs[q])) * us[q]).astype(BF16) for q in qs]
        ys = [_dot(hs[q], wd_ref[q]) for q in qs]
        for q in qs:
            store_rows(r0s[q], m, ld[q][0], ys[q], ns[q])
        ch = CH_MOE
        for q in qs:
            def chunk(c, carry, q=q):
                r0 = pl.multiple_of((bs[q] + m + c * ch) * SUBLANES, SUBLANES)
                xs, x16 = load_rows(r0, ch)
                g = _dot(x16, wg_ref[q])
                u = _dot(x16, wu_ref[q])
                y = _dot(((g * _sigmoid(g)) * u).astype(BF16), wd_ref[q])
                store_rows(r0, ch, xs, y, ns[q] - m - c * ch)
                return carry

            lax.fori_loop(0, (jnp.maximum(ns[q] - m, 0) + ch - 1) // ch, chunk, 0)

    @pl.when(j >= nsub + npair)
    def _combine():
        def body(t, carry):
            acc_ref[row8(t), :] = (x1t_ref[row8(t), :]
                                   + w0_ref[t] * stag_ref[row8(pc0_ref[t]), :]
                                   + w1_ref[t] * stag_ref[row8(pc1_ref[t]), :])
            return carry

        lax.fori_loop(0, sub, body, 0, unroll=8)
        xo = jnp.concatenate([acc_ref[pl.ds(s, sub, stride=SUBLANES), :] for s in range(SUBLANES)], axis=1)
        out_ref[...] = xo * lax.rsqrt(jnp.mean(xo * xo, axis=-1, keepdims=True) + EPS) * fnw_ref[...]


def _moe(p0, p1, w0, w1, cnt, h2t, x1t, wg16, wu16, wd16, final_norm):
    t = p0.shape[0]
    sub = SUB_MOE
    nsub = TS_MOE // sub
    npair = N_EXPERTS // EXPERTS_PER_STEP
    nt = t // TS_MOE
    smem = functools.partial(pl.BlockSpec, memory_space=pltpu.SMEM)

    def scat(i, j):
        return i * nsub + jnp.minimum(j, nsub - 1)

    def comb(i, j):
        return i * nsub + jnp.clip(j - nsub - npair, 0, nsub - 1)

    def pair(i, j):
        return jnp.clip(j - nsub, 0, npair - 1)

    def wspec(shape):
        return pl.BlockSpec((EXPERTS_PER_STEP,) + shape, lambda i, j: (pair(i, j), 0, 0))

    return pl.pallas_call(
        _moe_kernel,
        grid=(nt, nsub + npair + nsub),
        in_specs=[
            smem((sub,), lambda i, j: (scat(i, j),)), smem((sub,), lambda i, j: (scat(i, j),)),
            smem((sub,), lambda i, j: (comb(i, j),)), smem((sub,), lambda i, j: (comb(i, j),)),
            smem((sub,), lambda i, j: (comb(i, j),)), smem((sub,), lambda i, j: (comb(i, j),)),
            smem((LANES,), lambda i, j: (i,)),
            pl.BlockSpec((sub * SUBLANES, LANES), lambda i, j: (scat(i, j), 0)),
            pl.BlockSpec((sub * SUBLANES, LANES), lambda i, j: (comb(i, j), 0)),
            wspec((D_MODEL, D_EXPERT)), wspec((D_MODEL, D_EXPERT)), wspec((D_EXPERT, D_MODEL)),
            pl.BlockSpec((1, D_MODEL), lambda i, j: (0, 0)),
        ],
        out_specs=pl.BlockSpec((sub, D_MODEL), lambda i, j: (comb(i, j), 0)),
        out_shape=jax.ShapeDtypeStruct((t, D_MODEL), F32),
        scratch_shapes=[
            pltpu.VMEM(((TOP_K * TS_MOE + SLACK_MOE) * SUBLANES, LANES), F32),
            pltpu.VMEM((sub * SUBLANES, LANES), F32),
            pltpu.SMEM((LANES,), I32),
        ],
        compiler_params=pltpu.CompilerParams(
            dimension_semantics=("arbitrary", "arbitrary"), vmem_limit_bytes=VMEM_LIMIT_MOE),
        name="moe",
    )(p0, p1, p0, p1, w0, w1, cnt, h2t, x1t, wg16, wu16, wd16, final_norm)


def _pad_lanes(a, start):
    rows, n = a.shape
    return jnp.pad(a, ((0, 0), (start, LANES - start - n)))


def kernel(x, attn_norm, w_in, conv_a, a_log, dt_bias, norm_a, norm_b, w_out, ffn_norm,
           w_router_group, w_router_expert, w_gate, w_up, w_down, final_norm):
    batch, seq, d = x.shape
    t = batch * seq
    x2 = x.reshape(t, d)
    assert attn_norm.shape[0] == 1, "the closing RMSNorm is fused into the single layer's MoE combine step"

    pos = jnp.arange(seq, dtype=F32)
    inv_freq = ROPE_BASE ** (-jnp.arange(0, DH, 2, dtype=F32) / DH)
    ang = pos[:, None] * inv_freq[None, :]
    cos, sin = jnp.cos(ang), jnp.sin(ang)
    cosf = jnp.concatenate([cos, cos], axis=-1)
    sinf = jnp.concatenate([-sin, sin], axis=-1)

    wl = w_in[0]
    w_main = jnp.concatenate([wl[:, 0:4 * QK], wl[:, 4 * QK + 2 * HEADS:]], axis=1).astype(BF16)
    ws_hi, ws_lo = _split_bf16(_pad_lanes(wl[:, 4 * QK:4 * QK + 2 * HEADS], 0))
    alog_l = _pad_lanes(a_log[0][None, :], HEADS)
    dtb_l = _pad_lanes(dt_bias[0][None, :], HEADS)
    pm, gcol, grow = _proj(x2, attn_norm[0][None, :], w_main, ws_hi, ws_lo, alog_l, dtb_l, conv_a[0], seq)

    wr = jnp.concatenate(
        [_pad_lanes(w_router_group[0], 0)[:, 0:N_EXPERTS], w_router_expert[0]], axis=1)
    wr_hi, wr_lo = _split_bf16(wr)
    x1t, h2t, rr, cnt = _mixers(pm, gcol, grow, norm_a[0][None, :], cosf, sinf, norm_b[0][None, :], x2,
                                w_out[0].astype(BF16), ffn_norm[0][None, :], wr_hi, wr_lo, seq)

    p0 = rr[:, 0, :].reshape(-1).astype(I32)
    p1 = rr[:, 1, :].reshape(-1).astype(I32)
    out = _moe(p0, p1, rr[:, 2, :].reshape(-1), rr[:, 3, :].reshape(-1), cnt[:, 0, :].reshape(-1), h2t, x1t,
               w_gate[0].astype(BF16), w_up[0].astype(BF16), w_down[0].astype(BF16), final_norm[None, :])
    return out.reshape(batch, seq, d)
```

```python
import functools
import math

import jax
import jax.numpy as jnp
from jax import lax
from jax.experimental import pallas as pl
from jax.experimental.pallas import tpu as pltpu

F32 = jnp.float32
BF16 = jnp.bfloat16
I32 = jnp.int32

D_MODEL = 1024
HEADS = 4
DH = 128
CONV_K = 4
CHUNK = 64
ROPE_BASE = 10000.0
N_GROUPS = 8
EPG = 8
N_EXPERTS = N_GROUPS * EPG
TOP_K = 2
D_EXPERT = 256
EPS = 1e-6
QK = HEADS * DH
N_MAIN = 8 * QK
LANES = 128
SUBLANES = 8

TM_PROJ = 512
TS_MIXERS = 512
CHUNK_RET = 256
TM_MIX = 256
TS_MOE = 4096
SUB_MOE = 256
CH_FIRST_MOE = 160
CH_MOE = 128
SLACK_MOE = CH_FIRST_MOE + CH_MOE
EXPERTS_PER_STEP = 4
RANK_BITS = 16
RANK_RADIX = 1 << RANK_BITS
VMEM_LIMIT = 48 * 1024 * 1024
VMEM_LIMIT_MOE = 58 * 1024 * 1024
_LOG_GAMMA = tuple(math.log(1.0 - 2.0 ** (-5.0 - hh)) for hh in range(HEADS))


def _sigmoid(x):
    return 1.0 / (1.0 + jnp.exp(-x))


def _softplus(x):
    return jnp.maximum(x, 0.0) + jnp.log1p(jnp.exp(-jnp.abs(x)))


def _split_bf16(a):
    hi = a.astype(BF16)
    lo = (a - hi.astype(F32)).astype(BF16)
    return hi, lo


def _dot(a, b):
    return jnp.dot(a, b, preferred_element_type=F32)


def _dot_nt(a, b):
    return lax.dot_general(a, b, (((1,), (1,)), ((), ())), preferred_element_type=F32)


def _dot_tn(a, b):
    return lax.dot_general(a, b, (((0,), (0,)), ((), ())), preferred_element_type=F32)


def _bmm(a, b):
    return jnp.stack([_dot(a[c], b[c]) for c in range(a.shape[0])])


def _bmm_nt(a, b):
    return jnp.stack([_dot_nt(a[c], b[c]) for c in range(a.shape[0])])


def _proj_kernel(x_ref, nw_ref, wm_ref, wsh_ref, wsl_ref, alog_ref, dtb_ref, cw_ref, wg_ref, wu_ref, wd_ref,
                 pm_ref, gcol_ref, grow_ref, wg16_ref, wu16_ref, wd16_ref, tail_ref, *, tiles_per_seq):
    tm = x_ref.shape[0]
    wg16_ref[...] = wg_ref[...].astype(BF16)
    wu16_ref[...] = wu_ref[...].astype(BF16)
    wd16_ref[...] = wd_ref[...].astype(BF16)

    @pl.when(pl.program_id(0) % tiles_per_seq == 0)
    def _seq_start():
        tail_ref[...] = jnp.zeros(tail_ref.shape, F32)

    x = x_ref[...]
    h = x * lax.rsqrt(jnp.mean(x * x, axis=-1, keepdims=True) + EPS) * nw_ref[...]
    hb = h.astype(BF16)
    for j in (0, 3, 1, 4, 2, 5, 6, 7):
        blk = _dot(hb, wm_ref[:, j * QK:(j + 1) * QK])
        if j < 3:
            ext = jnp.concatenate([tail_ref[j], blk], axis=0)
            tail_ref[j] = blk[tm - SUBLANES:tm, :]
            w = cw_ref[:, j * QK:(j + 1) * QK]
            y = w[CONV_K - 1:CONV_K, :] * blk
            for jj in range(CONV_K - 1):
                y = y + w[jj:jj + 1, :] * pltpu.roll(ext, CONV_K - 1 - jj, axis=0)[SUBLANES:, :]
            blk = y * _sigmoid(y)
        if j < 2:
            scale = DH ** -0.5 if j == 0 else 1.0
            blk = jnp.concatenate(
                [blk[:, hh * DH:(hh + 1) * DH]
                 * (lax.rsqrt(jnp.sum(blk[:, hh * DH:(hh + 1) * DH] ** 2, axis=-1, keepdims=True) + EPS) * scale)
                 for hh in range(HEADS)], axis=1)
        pm_ref[:, j * QK:(j + 1) * QK] = blk.astype(BF16)
    hl = (h - hb.astype(F32)).astype(BF16)
    sm = _dot(hb, wsh_ref[...]) + _dot(hb, wsl_ref[...]) + _dot(hl, wsh_ref[...])
    lane = lax.broadcasted_iota(I32, (tm, LANES), 1)
    row = lax.broadcasted_iota(I32, (tm, LANES), 0)
    beta = _sigmoid(sm)
    g = -jnp.exp(alog_ref[...]) * _softplus(sm + dtb_ref[...])
    rin = row & (CHUNK - 1)
    gc = g
    d = 1
    while d < CHUNK:
        gc = gc + jnp.where(rin >= d, pltpu.roll(gc, d, axis=0), 0.0)
        d *= 2
    out = jnp.where(lane < HEADS, beta, jnp.where(lane < 2 * HEADS, gc, 0.0))
    gcol_ref[...] = out
    tr = out.T
    for hh in range(HEADS):
        for c in range(tm // CHUNK):
            grow_ref[hh, c, 0:1, :] = tr[hh:hh + 1, c * CHUNK:(c + 1) * CHUNK]
            grow_ref[hh, c, 1:2, :] = tr[HEADS + hh:HEADS + hh + 1, c * CHUNK:(c + 1) * CHUNK]


def _proj(x2, attn_norm, w_main, ws_hi, ws_lo, alog_l, dtb_l, conv_a, w_gate, w_up, w_down, seq):
    t = x2.shape[0]
    tm = TM_PROJ
    steps = t // tm
    assert N_EXPERTS % steps == 0, "each step casts an equal share of the expert weights"
    epb = N_EXPERTS // steps

    def wspec(shape):
        return pl.BlockSpec((epb,) + shape, lambda i: (i, 0, 0))

    def wshape(shape):
        return jax.ShapeDtypeStruct((N_EXPERTS,) + shape, BF16)

    return pl.pallas_call(
        functools.partial(_proj_kernel, tiles_per_seq=seq // tm),
        grid=(steps,),
        in_specs=[
            pl.BlockSpec((tm, D_MODEL), lambda i: (i, 0)),
            pl.BlockSpec((1, D_MODEL), lambda i: (0, 0)),
            pl.BlockSpec((D_MODEL, N_MAIN), lambda i: (0, 0)),
            pl.BlockSpec((D_MODEL, LANES), lambda i: (0, 0)),
            pl.BlockSpec((D_MODEL, LANES), lambda i: (0, 0)),
            pl.BlockSpec((1, LANES), lambda i: (0, 0)),
            pl.BlockSpec((1, LANES), lambda i: (0, 0)),
            pl.BlockSpec((CONV_K, 3 * QK), lambda i: (0, 0)),
            wspec((D_MODEL, D_EXPERT)), wspec((D_MODEL, D_EXPERT)), wspec((D_EXPERT, D_MODEL)),
        ],
        out_specs=[
            pl.BlockSpec((tm, N_MAIN), lambda i: (i, 0)),
            pl.BlockSpec((tm, LANES), lambda i: (i, 0)),
            pl.BlockSpec((HEADS, tm // CHUNK, 2, CHUNK), lambda i: (0, i, 0, 0)),
            wspec((D_MODEL, D_EXPERT)), wspec((D_MODEL, D_EXPERT)), wspec((D_EXPERT, D_MODEL)),
        ],
        out_shape=[
            jax.ShapeDtypeStruct((t, N_MAIN), BF16),
            jax.ShapeDtypeStruct((t, LANES), F32),
            jax.ShapeDtypeStruct((HEADS, t // CHUNK, 2, CHUNK), F32),
            wshape((D_MODEL, D_EXPERT)), wshape((D_MODEL, D_EXPERT)), wshape((D_EXPERT, D_MODEL)),
        ],
        scratch_shapes=[pltpu.VMEM((3, SUBLANES, QK), F32)],
        compiler_params=pltpu.CompilerParams(
            dimension_semantics=("arbitrary",), vmem_limit_bytes=VMEM_LIMIT),
        name="proj",
    )(x2, attn_norm, w_main, ws_hi, ws_lo, alog_l, dtb_l, conv_a, w_gate, w_up, w_down)


def _gated_norm(o, gate, nw):
    outs = []
    for h in range(HEADS):
        sl = slice(h * DH, (h + 1) * DH)
        oh = o[:, sl]
        wh = nw if nw.shape[1] == DH else nw[:, sl]
        oh = oh * lax.rsqrt(jnp.mean(oh * oh, axis=-1, keepdims=True) + EPS) * wh
        outs.append(oh * (gate[:, sl] * _sigmoid(gate[:, sl])))
    return jnp.concatenate(outs, axis=1)


def _gdn_heads(q, k, v, gblk, grow, st_ref):
    ts = q.shape[0]
    ncl = ts // CHUNK
    nc = HEADS * ncl

    def per_head(f):
        return jnp.concatenate([f(h, slice(h * DH, (h + 1) * DH)) for h in range(HEADS)], axis=0)

    lane = lax.broadcasted_iota(I32, (ts, LANES), 1)
    q3 = per_head(lambda h, sl: q[:, sl].reshape(ncl, CHUNK, DH))
    k3 = per_head(lambda h, sl: k[:, sl].reshape(ncl, CHUNK, DH))
    v3 = per_head(lambda h, sl: v[:, sl].reshape(ncl, CHUNK, DH))
    bc = per_head(lambda h, sl: jnp.sum(jnp.where(lane == h, gblk, 0.0), axis=-1, keepdims=True
                                        ).reshape(ncl, CHUNK, 1))
    gcc = per_head(lambda h, sl: jnp.sum(jnp.where(lane == h + HEADS, gblk, 0.0), axis=-1, keepdims=True
                                         ).reshape(ncl, CHUNK, 1))
    gr = grow.reshape(nc, 2, CHUNK)
    beta_r = gr[:, 0:1, :]
    gc_r = gr[:, 1:2, :]
    gl = gcc[:, CHUNK - 1:CHUNK, :]

    ii = lax.broadcasted_iota(I32, (nc, CHUNK, CHUNK), 1)
    jj = lax.broadcasted_iota(I32, (nc, CHUNK, CHUNK), 2)
    causal = ii >= jj
    strict = ii > jj
    dec = jnp.where(causal, jnp.exp(jnp.where(causal, gcc - gc_r, 0.0)), 0.0)

    k16 = k3.astype(BF16)
    q16 = q3.astype(BF16)
    v16 = v3.astype(BF16)
    gram = _bmm_nt(k16, k16)
    qk = _bmm_nt(q16, k16)
    a = jnp.where(strict, gram * dec * bc, 0.0)

    eye = (ii == jj).astype(F32)
    dm = eye - jnp.where((ii >> 1) == (jj >> 1), a, 0.0)
    sblk = 2
    while sblk < CHUNK:
        sh = sblk.bit_length() - 1
        off = jnp.where(((ii >> (sh + 1)) == (jj >> (sh + 1))) & ((ii >> sh) != (jj >> sh)), a, 0.0)
        d16 = dm.astype(BF16)
        t1 = _bmm(d16, off.astype(BF16))
        dm = dm - _bmm(t1.astype(BF16), d16)
        sblk *= 2

    tb = dm * beta_r
    tbe = tb * jnp.exp(gc_r)
    u = _bmm(tb.astype(BF16), v16)
    w = _bmm(tbe.astype(BF16), k16)
    attn = qk * dec
    wu = jnp.concatenate([w, u], axis=-1)
    wu16 = wu.astype(BF16)
    aw = _bmm(attn.astype(BF16), wu16)
    qe = q3 * jnp.exp(gcc) - aw[:, :, 0:DH]
    ol = aw[:, :, DH:2 * DH]
    kdec = k3 * jnp.exp(gl - gcc)

    trow = lax.broadcasted_iota(I32, (2 * CHUNK, DH), 0)
    xn = []
    for p in range(nc // 2):
        kd2 = kdec[2 * p:2 * p + 2].reshape(2 * CHUNK, DH)
        wu2 = wu16[2 * p:2 * p + 2].reshape(2 * CHUNK, 2 * DH)
        xn.append(_dot_tn(jnp.where(trow < CHUNK, kd2, 0.0).astype(BF16), wu2))
        xn.append(_dot_tn(jnp.where(trow >= CHUNK, kd2, 0.0).astype(BF16), wu2))

    sts = [st_ref[h] for h in range(HEADS)]
    outs = [[] for _ in range(HEADS)]
    for c in range(ncl):
        for h in range(HEADS):
            i = h * ncl + c
            s16 = sts[h].astype(BF16)
            outs[h].append(_dot(qe[i].astype(BF16), s16) + ol[i])
            sts[h] = jnp.exp(gl[i]) * sts[h] + (xn[i][:, DH:2 * DH] - _dot(xn[i][:, 0:DH].astype(BF16), s16))
    for h in range(HEADS):
        st_ref[h] = sts[h]
    return jnp.concatenate([jnp.concatenate(outs[h], axis=0) for h in range(HEADS)], axis=1)


def _ret_heads(qa, ka, va, cosf, sinf, st_ref):
    rows = qa.shape[0]
    c = CHUNK_RET
    ii = lax.broadcasted_iota(I32, (c, c), 0)
    jj = lax.broadcasted_iota(I32, (c, c), 1)
    rel = (ii - jj).astype(F32)
    pos = lax.broadcasted_iota(I32, (c, 1), 0).astype(F32)
    decs = [jnp.where(rel >= 0, jnp.exp(jnp.maximum(rel, 0.0) * _LOG_GAMMA[h]), 0.0) for h in range(HEADS)]
    sts = [st_ref[h] for h in range(HEADS)]
    outs = [[] for _ in range(HEADS)]
    for ck in range(rows // c):
        rs = slice(ck * c, (ck + 1) * c)
        for h in range(HEADS):
            sl = slice(h * DH, (h + 1) * DH)
            lg = _LOG_GAMMA[h]
            q = qa[rs, sl] * cosf[rs] + pltpu.roll(qa[rs, sl], DH // 2, axis=1) * sinf[rs]
            k = (ka[rs, sl] * cosf[rs] + pltpu.roll(ka[rs, sl], DH // 2, axis=1) * sinf[rs]) * (DH ** -0.5)
            v16 = va[rs, sl]
            k16 = k.astype(BF16)
            attn = _dot_nt(q.astype(BF16), k16) * decs[h]
            inner = _dot(attn.astype(BF16), v16)
            qd = q * jnp.exp((pos + 1.0) * lg)
            cross = _dot(qd.astype(BF16), sts[h].astype(BF16))
            kd = k * jnp.exp((c - 1.0 - pos) * lg)
            sts[h] = sts[h] * math.exp(c * lg) + _dot_tn(kd.astype(BF16), v16)
            outs[h].append(cross + inner)
    for h in range(HEADS):
        st_ref[h] = sts[h]
    return jnp.concatenate([jnp.concatenate(outs[h], axis=0) for h in range(HEADS)], axis=1)


def _route(x, o16, wo_ref, nw, wrh_ref, wrl_ref, carry):
    tm = x.shape[0]
    x1 = x + _dot(o16, wo_ref[...])
    h2 = x1 * lax.rsqrt(jnp.mean(x1 * x1, axis=-1, keepdims=True) + EPS) * nw
    hh, hl = _split_bf16(h2)
    lg = _dot(hh, wrh_ref[...]) + _dot(hh, wrl_ref[...]) + _dot(hl, wrh_ref[...])
    lane = lax.broadcasted_iota(I32, (tm, LANES), 1)
    big = jnp.int32(1 << 20)
    neg = jnp.float32(-jnp.inf)

    def argmax_first(vals, mask):
        mv = jnp.where(mask, vals, neg)
        m = jnp.max(mv, axis=-1, keepdims=True)
        idx = jnp.min(jnp.where(mask & (mv == m), lane, big), axis=-1, keepdims=True)
        return m, idx

    gmask = lane < N_GROUPS
    gm, gi = argmax_first(lg, gmask)
    gw = 1.0 / jnp.sum(jnp.where(gmask, jnp.exp(lg - gm), 0.0), axis=-1, keepdims=True)
    emask = (lane >= N_EXPERTS) & ((lane >> 3) == gi + N_EXPERTS // EPG)
    m1, i1 = argmax_first(lg, emask)
    m2, i2 = argmax_first(lg, emask & (lane != i1))
    e21 = jnp.exp(m2 - m1)
    w0 = gw / (1.0 + e21)
    w1 = gw * e21 / (1.0 + e21)
    e0 = i1 - N_EXPERTS
    e1 = i2 - N_EXPERTS

    oh = ((lane == e0) | (lane == e1)).astype(BF16)
    ri = lax.broadcasted_iota(I32, (tm, tm), 0)
    ci = lax.broadcasted_iota(I32, (tm, tm), 1)
    lstrict = (ri > ci).astype(BF16)
    pref = _dot(lstrict, oh) + carry
    r0 = jnp.sum(jnp.where(lane == e0, pref, 0.0), axis=-1, keepdims=True)
    r1 = jnp.sum(jnp.where(lane == e1, pref, 0.0), axis=-1, keepdims=True)
    carry = carry + jnp.sum(oh.astype(F32), axis=0, keepdims=True)

    rmat = jnp.where(lane == 0, e0.astype(F32) * RANK_RADIX + r0,
                     jnp.where(lane == 1, e1.astype(F32) * RANK_RADIX + r1,
                               jnp.where(lane == 2, w0, jnp.where(lane == 3, w1, 0.0))))
    return x1, h2, rmat.T[0:SUBLANES, :], carry


def _finish_slots(rr_ref, carry):
    lane8 = lax.broadcasted_iota(I32, (SUBLANES, LANES), 1)
    cnt8 = jnp.broadcast_to(carry, (SUBLANES, LANES))
    inc = cnt8
    d = 1
    while d < LANES:
        inc = inc + jnp.where(lane8 >= d, pltpu.roll(inc, d, axis=1), 0.0)
        d *= 2
    base = inc - cnt8
    pk = rr_ref[:, 0:TOP_K, :]
    ex = jnp.floor(pk * (1.0 / RANK_RADIX))
    slot = pk - ex * RANK_RADIX
    for kk in range(N_EXPERTS):
        bk = jnp.sum(jnp.where(lane8 == kk, base, 0.0), axis=-1, keepdims=True)[0:1, :]
        slot = slot + jnp.where(ex == kk, bk, 0.0)
    rr_ref[:, 0:TOP_K, :] = slot * SUBLANES


def _mixers_kernel(qa_ref, ka_ref, va_ref, ga_ref, qb_ref, kb_ref, vb_ref, gb_ref, gcol_ref, grow_ref,
                   na_ref, cos_ref, sin_ref, nb_ref, x_ref, wo_ref, fw_ref, wrh_ref, wrl_ref,
                   x1t_ref, h2t_ref, rr_ref, cnt_ref,
                   sta_ref, stb_ref, oprev_ref, carry_ref, *, tiles_per_seq, per):
    i = pl.program_id(0)
    ts = x_ref.shape[0]
    tm = TM_MIX
    prev = i + per - 1

    @pl.when(i == 0)
    def _first():
        oprev_ref[...] = jnp.zeros(oprev_ref.shape, oprev_ref.dtype)

    @pl.when(i % tiles_per_seq == 0)
    def _seq_start():
        sta_ref[...] = jnp.zeros(sta_ref.shape, F32)
        stb_ref[...] = jnp.zeros(stb_ref.shape, F32)

    @pl.when((i == 0) | (prev % per == 0))
    def _moe_tile_start():
        carry_ref[...] = jnp.zeros((1, LANES), F32)

    carry = carry_ref[...]
    nw = fw_ref[...]
    for hf in range(ts // tm):
        rows = slice(hf * tm, (hf + 1) * tm)
        x1, h2, rmt, carry = _route(x_ref[rows, :], oprev_ref[rows, :], wo_ref, nw, wrh_ref, wrl_ref, carry)
        for s in range(SUBLANES):
            x1t_ref[pl.ds(hf * tm * SUBLANES + s, tm, stride=SUBLANES), :] = x1[:, s * LANES:(s + 1) * LANES]
            h2t_ref[pl.ds(hf * tm * SUBLANES + s, tm, stride=SUBLANES), :] = h2[:, s * LANES:(s + 1) * LANES]
        rr_ref[(prev % per) * (ts // tm) + hf] = rmt
    carry_ref[...] = carry
    cnt_ref[...] = jnp.broadcast_to(carry, (SUBLANES, LANES)).astype(I32)

    oa = _gdn_heads(qa_ref[...].astype(F32), ka_ref[...].astype(F32), va_ref[...].astype(F32),
                    gcol_ref[...], grow_ref[...], sta_ref)
    ob = _ret_heads(qb_ref[...].astype(F32), kb_ref[...].astype(F32), vb_ref[...], cos_ref[...], sin_ref[...],
                    stb_ref)
    oa = _gated_norm(oa, ga_ref[...].astype(F32), na_ref[...])
    ob = _gated_norm(ob, gb_ref[...].astype(F32), nb_ref[...])
    oprev_ref[...] = jnp.concatenate([oa, ob], axis=1).astype(oprev_ref.dtype)

    @pl.when((i > 0) & (prev % per == per - 1))
    def _moe_tile_end():
        _finish_slots(rr_ref, carry)


def _mixers(pm, gcol, grow, norm_a, cosf, sinf, norm_b, x2, w_out16, ffn_norm, wr_hi, wr_lo, seq):
    t = x2.shape[0]
    ts = TS_MIXERS
    n = t // ts
    nc = ts // CHUNK
    tiles_per_seq = seq // ts
    per = TS_MOE // ts

    def cur(i):
        return jnp.minimum(i, n - 1)

    def prv(i):
        return jnp.maximum(i - 1, 0)

    def col(cb):
        return pl.BlockSpec((ts, QK), lambda i, cb=cb: (cur(i), cb))

    def const(shape):
        return pl.BlockSpec(shape, lambda i: (0,) * len(shape))

    return pl.pallas_call(
        functools.partial(_mixers_kernel, tiles_per_seq=tiles_per_seq, per=per),
        grid=(n + 1,),
        in_specs=[
            col(0), col(1), col(2), col(3), col(4), col(5), col(6), col(7),
            pl.BlockSpec((ts, LANES), lambda i: (cur(i), 0)),
            pl.BlockSpec((HEADS, nc, 2, CHUNK), lambda i: (0, cur(i), 0, 0)),
            const((1, DH)),
            pl.BlockSpec((ts, DH), lambda i: (cur(i) % tiles_per_seq, 0)),
            pl.BlockSpec((ts, DH), lambda i: (cur(i) % tiles_per_seq, 0)),
            const((1, QK)),
            pl.BlockSpec((ts, D_MODEL), lambda i: (prv(i), 0)),
            const((D_MODEL, D_MODEL)), const((1, D_MODEL)), const((D_MODEL, LANES)), const((D_MODEL, LANES)),
        ],
        out_specs=[
            pl.BlockSpec((ts * SUBLANES, LANES), lambda i: (prv(i), 0)),
            pl.BlockSpec((ts * SUBLANES, LANES), lambda i: (prv(i), 0)),
            pl.BlockSpec((TS_MOE // TM_MIX, SUBLANES, TM_MIX), lambda i: (prv(i) // per, 0, 0)),
            pl.BlockSpec((None, SUBLANES, LANES), lambda i: (prv(i) // per, 0, 0)),
        ],
        out_shape=[
            jax.ShapeDtypeStruct((t * SUBLANES, LANES), F32),
            jax.ShapeDtypeStruct((t * SUBLANES, LANES), F32),
            jax.ShapeDtypeStruct((t // TM_MIX, SUBLANES, TM_MIX), F32),
            jax.ShapeDtypeStruct((t // TS_MOE, SUBLANES, LANES), I32),
        ],
        scratch_shapes=[
            pltpu.VMEM((HEADS, DH, DH), F32),
            pltpu.VMEM((HEADS, DH, DH), F32),
            pltpu.VMEM((ts, D_MODEL), BF16),
            pltpu.VMEM((1, LANES), F32),
        ],
        compiler_params=pltpu.CompilerParams(
            dimension_semantics=("arbitrary",), vmem_limit_bytes=VMEM_LIMIT),
        name="mixers",
    )(pm, pm, pm, pm, pm, pm, pm, pm, gcol, grow, norm_a, cosf, sinf, norm_b, x2, w_out16, ffn_norm, wr_hi, wr_lo)


def _moe_kernel(ps0_ref, ps1_ref, pc0_ref, pc1_ref, w0_ref, w1_ref, cnt_ref,
                h2t_ref, x1t_ref, wg_ref, wu_ref, wd_ref, fnw_ref,
                out_ref, stag_ref, acc_ref, base_ref):
    j = pl.program_id(1)
    sub = out_ref.shape[0]
    nsub = TS_MOE // sub
    npair = N_EXPERTS // EXPERTS_PER_STEP

    def row8(r):
        return pl.ds(pl.multiple_of(r * SUBLANES, SUBLANES), SUBLANES)

    def at(r8):
        return pl.ds(pl.multiple_of(r8, SUBLANES), SUBLANES)

    @pl.when(j == 0)
    def _bases():
        stag_ref[pl.ds(TOP_K * TS_MOE * SUBLANES, SLACK_MOE * SUBLANES), :] = jnp.zeros(
            (SLACK_MOE * SUBLANES, LANES), F32)

        def body(e, run):
            base_ref[e] = run
            return run + cnt_ref[e]

        lax.fori_loop(0, N_EXPERTS, body, jnp.int32(0))

    @pl.when(j < nsub)
    def _scatter():
        def body(t, carry):
            row = h2t_ref[row8(t), :]
            stag_ref[at(ps0_ref[t]), :] = row
            stag_ref[at(ps1_ref[t]), :] = row
            return carry

        lax.fori_loop(0, sub, body, 0, unroll=8)

    def load_rows(r0, m):
        xs = [stag_ref[pl.ds(r0 + s, m, stride=SUBLANES), :] for s in range(SUBLANES)]
        return xs, jnp.concatenate([v.astype(BF16) for v in xs], axis=1)

    def store_rows(r0, m, xs, y, nvalid):
        keep = lax.broadcasted_iota(I32, (m, LANES), 0) < nvalid
        for s in range(SUBLANES):
            stag_ref[pl.ds(r0 + s, m, stride=SUBLANES), :] = jnp.where(
                keep, y[:, s * LANES:(s + 1) * LANES], xs[s])

    @pl.when((j >= nsub) & (j < nsub + npair))
    def _experts():
        qs = range(EXPERTS_PER_STEP)
        es = [(j - nsub) * EXPERTS_PER_STEP + q for q in qs]
        ns = [cnt_ref[e] for e in es]
        bs = [base_ref[e] for e in es]
        m = CH_FIRST_MOE
        r0s = [pl.multiple_of(b * SUBLANES, SUBLANES) for b in bs]
        ld = [load_rows(r0, m) for r0 in r0s]
        gs = [_dot(ld[q][1], wg_ref[q]) for q in qs]
        us = [_dot(ld[q][1], wu_ref[q]) for q in qs]
        hs = [((gs[q] * _sigmoid(gs[q])) * us[q]).astype(BF16) for q in qs]
        ys = [_dot(hs[q], wd_ref[q]) for q in qs]
        for q in qs:
            store_rows(r0s[q], m, ld[q][0], ys[q], ns[q])
        ch = CH_MOE
        for q in qs:
            def chunk(c, carry, q=q):
                r0 = pl.multiple_of((bs[q] + m + c * ch) * SUBLANES, SUBLANES)
                xs, x16 = load_rows(r0, ch)
                g = _dot(x16, wg_ref[q])
                u = _dot(x16, wu_ref[q])
                y = _dot(((g * _sigmoid(g)) * u).astype(BF16), wd_ref[q])
                store_rows(r0, ch, xs, y, ns[q] - m - c * ch)
                return carry

            lax.fori_loop(0, (jnp.maximum(ns[q] - m, 0) + ch - 1) // ch, chunk, 0)

    @pl.when(j >= nsub + npair)
    def _combine():
        def body(t, carry):
            acc_ref[row8(t), :] = (x1t_ref[row8(t), :]
                                   + w0_ref[t] * stag_ref[at(pc0_ref[t]), :]
                                   + w1_ref[t] * stag_ref[at(pc1_ref[t]), :])
            return carry

        lax.fori_loop(0, sub, body, 0, unroll=8)
        xo = jnp.concatenate([acc_ref[pl.ds(s, sub, stride=SUBLANES), :] for s in range(SUBLANES)], axis=1)
        out_ref[...] = xo * lax.rsqrt(jnp.mean(xo * xo, axis=-1, keepdims=True) + EPS) * fnw_ref[...]


def _moe(p0, p1, w0, w1, cnt, h2t, x1t, wg16, wu16, wd16, final_norm):
    t = p0.shape[0]
    sub = SUB_MOE
    nsub = TS_MOE // sub
    npair = N_EXPERTS // EXPERTS_PER_STEP
    nt = t // TS_MOE
    smem = functools.partial(pl.BlockSpec, memory_space=pltpu.SMEM)

    def scat(i, j):
        return i * nsub + jnp.minimum(j, nsub - 1)

    def comb(i, j):
        return i * nsub + jnp.clip(j - nsub - npair, 0, nsub - 1)

    def pair(i, j):
        return jnp.clip(j - nsub, 0, npair - 1)

    def wspec(shape):
        return pl.BlockSpec((EXPERTS_PER_STEP,) + shape, lambda i, j: (pair(i, j), 0, 0))

    return pl.pallas_call(
        _moe_kernel,
        grid=(nt, nsub + npair + nsub),
        in_specs=[
            smem((sub,), lambda i, j: (scat(i, j),)), smem((sub,), lambda i, j: (scat(i, j),)),
            smem((sub,), lambda i, j: (comb(i, j),)), smem((sub,), lambda i, j: (comb(i, j),)),
            smem((sub,), lambda i, j: (comb(i, j),)), smem((sub,), lambda i, j: (comb(i, j),)),
            smem((LANES,), lambda i, j: (i,)),
            pl.BlockSpec((sub * SUBLANES, LANES), lambda i, j: (scat(i, j), 0)),
            pl.BlockSpec((sub * SUBLANES, LANES), lambda i, j: (comb(i, j), 0)),
            wspec((D_MODEL, D_EXPERT)), wspec((D_MODEL, D_EXPERT)), wspec((D_EXPERT, D_MODEL)),
            pl.BlockSpec((1, D_MODEL), lambda i, j: (0, 0)),
        ],
        out_specs=pl.BlockSpec((sub, D_MODEL), lambda i, j: (comb(i, j), 0)),
        out_shape=jax.ShapeDtypeStruct((t, D_MODEL), F32),
        scratch_shapes=[
            pltpu.VMEM(((TOP_K * TS_MOE + SLACK_MOE) * SUBLANES, LANES), F32),
            pltpu.VMEM((sub * SUBLANES, LANES), F32),
            pltpu.SMEM((LANES,), I32),
        ],
        compiler_params=pltpu.CompilerParams(
            dimension_semantics=("arbitrary", "arbitrary"), vmem_limit_bytes=VMEM_LIMIT_MOE),
        name="moe",
    )(p0, p1, p0, p1, w0, w1, cnt, h2t, x1t, wg16, wu16, wd16, final_norm)


def _pad_lanes(a, start):
    rows, n = a.shape
    return jnp.pad(a, ((0, 0), (start, LANES - start - n)))


def kernel(x, attn_norm, w_in, conv_a, a_log, dt_bias, norm_a, norm_b, w_out, ffn_norm,
           w_router_group, w_router_expert, w_gate, w_up, w_down, final_norm):
    batch, seq, d = x.shape
    t = batch * seq
    x2 = x.reshape(t, d)
    assert attn_norm.shape[0] == 1, "the closing RMSNorm is fused into the single layer's MoE combine step"

    pos = jnp.arange(seq, dtype=F32)
    inv_freq = ROPE_BASE ** (-jnp.arange(0, DH, 2, dtype=F32) / DH)
    ang = pos[:, None] * inv_freq[None, :]
    cos, sin = jnp.cos(ang), jnp.sin(ang)
    cosf = jnp.concatenate([cos, cos], axis=-1)
    sinf = jnp.concatenate([-sin, sin], axis=-1)

    wl = w_in[0]
    w_main = jnp.concatenate([wl[:, 0:4 * QK], wl[:, 4 * QK + 2 * HEADS:]], axis=1).astype(BF16)
    ws_hi, ws_lo = _split_bf16(_pad_lanes(wl[:, 4 * QK:4 * QK + 2 * HEADS], 0))
    alog_l = _pad_lanes(a_log[0][None, :], HEADS)
    dtb_l = _pad_lanes(dt_bias[0][None, :], HEADS)
    pm, gcol, grow, wg16, wu16, wd16 = _proj(x2, attn_norm[0][None, :], w_main, ws_hi, ws_lo, alog_l, dtb_l,
                                             conv_a[0], w_gate[0], w_up[0], w_down[0], seq)

    wr = jnp.concatenate(
        [_pad_lanes(w_router_group[0], 0)[:, 0:N_EXPERTS], w_router_expert[0]], axis=1)
    wr_hi, wr_lo = _split_bf16(wr)
    x1t, h2t, rr, cnt = _mixers(pm, gcol, grow, norm_a[0][None, :], cosf, sinf, norm_b[0][None, :], x2,
                                w_out[0].astype(BF16), ffn_norm[0][None, :], wr_hi, wr_lo, seq)

    p0 = rr[:, 0, :].reshape(-1).astype(I32)
    p1 = rr[:, 1, :].reshape(-1).astype(I32)
    out = _moe(p0, p1, rr[:, 2, :].reshape(-1), rr[:, 3, :].reshape(-1), cnt[:, 0, :].reshape(-1), h2t, x1t,
               wg16, wu16, wd16, final_norm[None, :])
    return out.reshape(batch, seq, d)
```

```python
import functools
import math

import jax
import jax.numpy as jnp
from jax import lax
from jax.experimental import pallas as pl
from jax.experimental.pallas import tpu as pltpu

F32 = jnp.float32
BF16 = jnp.bfloat16
I32 = jnp.int32

D_MODEL = 1024
HEADS = 4
DH = 128
CONV_K = 4
CHUNK = 64
ROPE_BASE = 10000.0
N_GROUPS = 8
EPG = 8
N_EXPERTS = N_GROUPS * EPG
TOP_K = 2
D_EXPERT = 256
EPS = 1e-6
QK = HEADS * DH
N_MAIN = 8 * QK
LANES = 128
SUBLANES = 8

TS_FRONT = 512
CHUNK_RET = 256
TM_MIX = 256
TS_MOE = 4096
SUB_MOE = 256
CH_FIRST_MOE = 160
CH_MOE = 128
SLACK_MOE = CH_FIRST_MOE + CH_MOE
EXPERTS_PER_STEP = 4
RANK_BITS = 16
RANK_RADIX = 1 << RANK_BITS
VMEM_LIMIT_FRONT = 58 * 1024 * 1024
VMEM_LIMIT_MOE = 58 * 1024 * 1024
_LOG_GAMMA = tuple(math.log(1.0 - 2.0 ** (-5.0 - hh)) for hh in range(HEADS))


def _sigmoid(x):
    return 1.0 / (1.0 + jnp.exp(-x))


def _softplus(x):
    return jnp.maximum(x, 0.0) + jnp.log1p(jnp.exp(-jnp.abs(x)))


def _split_bf16(a):
    hi = a.astype(BF16)
    lo = (a - hi.astype(F32)).astype(BF16)
    return hi, lo


def _dot(a, b):
    return jnp.dot(a, b, preferred_element_type=F32)


def _dot_nt(a, b):
    return lax.dot_general(a, b, (((1,), (1,)), ((), ())), preferred_element_type=F32)


def _dot_tn(a, b):
    return lax.dot_general(a, b, (((0,), (0,)), ((), ())), preferred_element_type=F32)


def _bmm(a, b):
    return jnp.stack([_dot(a[c], b[c]) for c in range(a.shape[0])])


def _bmm_nt(a, b):
    return jnp.stack([_dot_nt(a[c], b[c]) for c in range(a.shape[0])])


def _project(x, nw_ref, wm_ref, wsh_ref, wsl_ref, alog_ref, dtb_ref, cw_ref, tail_ref, grow_ref):
    tm = x.shape[0]
    h = x * lax.rsqrt(jnp.mean(x * x, axis=-1, keepdims=True) + EPS) * nw_ref[...]
    hb = h.astype(BF16)
    blocks = [None] * (N_MAIN // QK)
    for j in (0, 3, 1, 4, 2, 5, 6, 7):
        blk = _dot(hb, wm_ref[:, j * QK:(j + 1) * QK])
        if j < 3:
            ext = jnp.concatenate([tail_ref[j], blk], axis=0)
            tail_ref[j] = blk[tm - SUBLANES:tm, :]
            w = cw_ref[:, j * QK:(j + 1) * QK]
            y = w[CONV_K - 1:CONV_K, :] * blk
            for jj in range(CONV_K - 1):
                y = y + w[jj:jj + 1, :] * pltpu.roll(ext, CONV_K - 1 - jj, axis=0)[SUBLANES:, :]
            blk = y * _sigmoid(y)
        if j < 2:
            scale = DH ** -0.5 if j == 0 else 1.0
            blk = jnp.concatenate(
                [blk[:, hh * DH:(hh + 1) * DH]
                 * (lax.rsqrt(jnp.sum(blk[:, hh * DH:(hh + 1) * DH] ** 2, axis=-1, keepdims=True) + EPS) * scale)
                 for hh in range(HEADS)], axis=1)
        blocks[j] = blk.astype(BF16)
    hl = (h - hb.astype(F32)).astype(BF16)
    sm = _dot(hb, wsh_ref[...]) + _dot(hb, wsl_ref[...]) + _dot(hl, wsh_ref[...])
    lane = lax.broadcasted_iota(I32, (tm, LANES), 1)
    row = lax.broadcasted_iota(I32, (tm, LANES), 0)
    beta = _sigmoid(sm)
    g = -jnp.exp(alog_ref[...]) * _softplus(sm + dtb_ref[...])
    rin = row & (CHUNK - 1)
    gc = g
    d = 1
    while d < CHUNK:
        gc = gc + jnp.where(rin >= d, pltpu.roll(gc, d, axis=0), 0.0)
        d *= 2
    out = jnp.where(lane < HEADS, beta, jnp.where(lane < 2 * HEADS, gc, 0.0))
    tr = out.T
    for hh in range(HEADS):
        for c in range(tm // CHUNK):
            grow_ref[hh, c, 0:1, :] = tr[hh:hh + 1, c * CHUNK:(c + 1) * CHUNK]
            grow_ref[hh, c, 1:2, :] = tr[HEADS + hh:HEADS + hh + 1, c * CHUNK:(c + 1) * CHUNK]
    return blocks, out


def _gated_norm(o, gate, nw):
    outs = []
    for h in range(HEADS):
        sl = slice(h * DH, (h + 1) * DH)
        oh = o[:, sl]
        wh = nw if nw.shape[1] == DH else nw[:, sl]
        oh = oh * lax.rsqrt(jnp.mean(oh * oh, axis=-1, keepdims=True) + EPS) * wh
        outs.append(oh * (gate[:, sl] * _sigmoid(gate[:, sl])))
    return jnp.concatenate(outs, axis=1)


def _gdn_heads(q, k, v, gblk, grow, st_ref):
    ts = q.shape[0]
    ncl = ts // CHUNK
    nc = HEADS * ncl

    def per_head(f):
        return jnp.concatenate([f(h, slice(h * DH, (h + 1) * DH)) for h in range(HEADS)], axis=0)

    lane = lax.broadcasted_iota(I32, (ts, LANES), 1)
    q3 = per_head(lambda h, sl: q[:, sl].reshape(ncl, CHUNK, DH))
    k3 = per_head(lambda h, sl: k[:, sl].reshape(ncl, CHUNK, DH))
    v3 = per_head(lambda h, sl: v[:, sl].reshape(ncl, CHUNK, DH))
    bc = per_head(lambda h, sl: jnp.sum(jnp.where(lane == h, gblk, 0.0), axis=-1, keepdims=True
                                        ).reshape(ncl, CHUNK, 1))
    gcc = per_head(lambda h, sl: jnp.sum(jnp.where(lane == h + HEADS, gblk, 0.0), axis=-1, keepdims=True
                                         ).reshape(ncl, CHUNK, 1))
    gr = grow.reshape(nc, 2, CHUNK)
    beta_r = gr[:, 0:1, :]
    gc_r = gr[:, 1:2, :]
    gl = gcc[:, CHUNK - 1:CHUNK, :]

    ii = lax.broadcasted_iota(I32, (nc, CHUNK, CHUNK), 1)
    jj = lax.broadcasted_iota(I32, (nc, CHUNK, CHUNK), 2)
    causal = ii >= jj
    strict = ii > jj
    dec = jnp.where(causal, jnp.exp(jnp.where(causal, gcc - gc_r, 0.0)), 0.0)

    k16 = k3.astype(BF16)
    q16 = q3.astype(BF16)
    v16 = v3.astype(BF16)
    gram = _bmm_nt(k16, k16)
    qk = _bmm_nt(q16, k16)
    a = jnp.where(strict, gram * dec * bc, 0.0)

    eye = (ii == jj).astype(F32)
    dm = eye - jnp.where((ii >> 1) == (jj >> 1), a, 0.0)
    sblk = 2
    while sblk < CHUNK:
        sh = sblk.bit_length() - 1
        off = jnp.where(((ii >> (sh + 1)) == (jj >> (sh + 1))) & ((ii >> sh) != (jj >> sh)), a, 0.0)
        d16 = dm.astype(BF16)
        t1 = _bmm(d16, off.astype(BF16))
        dm = dm - _bmm(t1.astype(BF16), d16)
        sblk *= 2

    tb = dm * beta_r
    tbe = tb * jnp.exp(gc_r)
    u = _bmm(tb.astype(BF16), v16)
    w = _bmm(tbe.astype(BF16), k16)
    attn = qk * dec
    wu = jnp.concatenate([w, u], axis=-1)
    wu16 = wu.astype(BF16)
    aw = _bmm(attn.astype(BF16), wu16)
    qe = q3 * jnp.exp(gcc) - aw[:, :, 0:DH]
    ol = aw[:, :, DH:2 * DH]
    kdec = k3 * jnp.exp(gl - gcc)

    trow = lax.broadcasted_iota(I32, (2 * CHUNK, DH), 0)
    xn = []
    for p in range(nc // 2):
        kd2 = kdec[2 * p:2 * p + 2].reshape(2 * CHUNK, DH)
        wu2 = wu16[2 * p:2 * p + 2].reshape(2 * CHUNK, 2 * DH)
        xn.append(_dot_tn(jnp.where(trow < CHUNK, kd2, 0.0).astype(BF16), wu2))
        xn.append(_dot_tn(jnp.where(trow >= CHUNK, kd2, 0.0).astype(BF16), wu2))

    sts = [st_ref[h] for h in range(HEADS)]
    outs = [[] for _ in range(HEADS)]
    for c in range(ncl):
        for h in range(HEADS):
            i = h * ncl + c
            s16 = sts[h].astype(BF16)
            outs[h].append(_dot(qe[i].astype(BF16), s16) + ol[i])
            sts[h] = jnp.exp(gl[i]) * sts[h] + (xn[i][:, DH:2 * DH] - _dot(xn[i][:, 0:DH].astype(BF16), s16))
    for h in range(HEADS):
        st_ref[h] = sts[h]
    return jnp.concatenate([jnp.concatenate(outs[h], axis=0) for h in range(HEADS)], axis=1)


def _ret_heads(qa, ka, va, cosf, sinf, st_ref):
    rows = qa.shape[0]
    c = CHUNK_RET
    ii = lax.broadcasted_iota(I32, (c, c), 0)
    jj = lax.broadcasted_iota(I32, (c, c), 1)
    rel = (ii - jj).astype(F32)
    pos = lax.broadcasted_iota(I32, (c, 1), 0).astype(F32)
    decs = [jnp.where(rel >= 0, jnp.exp(jnp.maximum(rel, 0.0) * _LOG_GAMMA[h]), 0.0) for h in range(HEADS)]
    sts = [st_ref[h] for h in range(HEADS)]
    outs = [[] for _ in range(HEADS)]
    for ck in range(rows // c):
        rs = slice(ck * c, (ck + 1) * c)
        for h in range(HEADS):
            sl = slice(h * DH, (h + 1) * DH)
            lg = _LOG_GAMMA[h]
            q = qa[rs, sl] * cosf[rs] + pltpu.roll(qa[rs, sl], DH // 2, axis=1) * sinf[rs]
            k = (ka[rs, sl] * cosf[rs] + pltpu.roll(ka[rs, sl], DH // 2, axis=1) * sinf[rs]) * (DH ** -0.5)
            v16 = va[rs, sl]
            k16 = k.astype(BF16)
            attn = _dot_nt(q.astype(BF16), k16) * decs[h]
            inner = _dot(attn.astype(BF16), v16)
            qd = q * jnp.exp((pos + 1.0) * lg)
            cross = _dot(qd.astype(BF16), sts[h].astype(BF16))
            kd = k * jnp.exp((c - 1.0 - pos) * lg)
            sts[h] = sts[h] * math.exp(c * lg) + _dot_tn(kd.astype(BF16), v16)
            outs[h].append(cross + inner)
    for h in range(HEADS):
        st_ref[h] = sts[h]
    return jnp.concatenate([jnp.concatenate(outs[h], axis=0) for h in range(HEADS)], axis=1)


def _route(x, o16, wo_ref, nw, wrh_ref, wrl_ref, carry):
    tm = x.shape[0]
    x1 = x + _dot(o16, wo_ref[...])
    h2 = x1 * lax.rsqrt(jnp.mean(x1 * x1, axis=-1, keepdims=True) + EPS) * nw
    hh, hl = _split_bf16(h2)
    lg = _dot(hh, wrh_ref[...]) + _dot(hh, wrl_ref[...]) + _dot(hl, wrh_ref[...])
    lane = lax.broadcasted_iota(I32, (tm, LANES), 1)
    big = jnp.int32(1 << 20)
    neg = jnp.float32(-jnp.inf)

    def argmax_first(vals, mask):
        mv = jnp.where(mask, vals, neg)
        m = jnp.max(mv, axis=-1, keepdims=True)
        idx = jnp.min(jnp.where(mask & (mv == m), lane, big), axis=-1, keepdims=True)
        return m, idx

    gmask = lane < N_GROUPS
    gm, gi = argmax_first(lg, gmask)
    gw = 1.0 / jnp.sum(jnp.where(gmask, jnp.exp(lg - gm), 0.0), axis=-1, keepdims=True)
    emask = (lane >= N_EXPERTS) & ((lane >> 3) == gi + N_EXPERTS // EPG)
    m1, i1 = argmax_first(lg, emask)
    m2, i2 = argmax_first(lg, emask & (lane != i1))
    e21 = jnp.exp(m2 - m1)
    w0 = gw / (1.0 + e21)
    w1 = gw * e21 / (1.0 + e21)
    e0 = i1 - N_EXPERTS
    e1 = i2 - N_EXPERTS

    oh = ((lane == e0) | (lane == e1)).astype(BF16)
    ri = lax.broadcasted_iota(I32, (tm, tm), 0)
    ci = lax.broadcasted_iota(I32, (tm, tm), 1)
    lstrict = (ri > ci).astype(BF16)
    pref = _dot(lstrict, oh) + carry
    r0 = jnp.sum(jnp.where(lane == e0, pref, 0.0), axis=-1, keepdims=True)
    r1 = jnp.sum(jnp.where(lane == e1, pref, 0.0), axis=-1, keepdims=True)
    carry = carry + jnp.sum(oh.astype(F32), axis=0, keepdims=True)

    rmat = jnp.where(lane == 0, e0.astype(F32) * RANK_RADIX + r0,
                     jnp.where(lane == 1, e1.astype(F32) * RANK_RADIX + r1,
                               jnp.where(lane == 2, w0, jnp.where(lane == 3, w1, 0.0))))
    return x1, h2, rmat.T[0:SUBLANES, :], carry


def _finish_slots(rr_ref, carry):
    lane8 = lax.broadcasted_iota(I32, (SUBLANES, LANES), 1)
    cnt8 = jnp.broadcast_to(carry, (SUBLANES, LANES))
    inc = cnt8
    d = 1
    while d < LANES:
        inc = inc + jnp.where(lane8 >= d, pltpu.roll(inc, d, axis=1), 0.0)
        d *= 2
    base = inc - cnt8
    pk = rr_ref[:, 0:TOP_K, :]
    ex = jnp.floor(pk * (1.0 / RANK_RADIX))
    slot = pk - ex * RANK_RADIX
    for kk in range(N_EXPERTS):
        bk = jnp.sum(jnp.where(lane8 == kk, base, 0.0), axis=-1, keepdims=True)[0:1, :]
        slot = slot + jnp.where(ex == kk, bk, 0.0)
    rr_ref[:, 0:TOP_K, :] = slot * SUBLANES


def _front_kernel(xc_ref, x_ref, anw_ref, wm_ref, wsh_ref, wsl_ref, alog_ref, dtb_ref, cw_ref,
                  wg_ref, wu_ref, wd_ref, na_ref, cos_ref, sin_ref, nb_ref, wo_ref, fw_ref, wrh_ref, wrl_ref,
                  x1t_ref, h2t_ref, rr_ref, cnt_ref, wg16_ref, wu16_ref, wd16_ref,
                  tail_ref, grow_ref, sta_ref, stb_ref, oprev_ref, carry_ref, *, tiles_per_seq, per):
    i = pl.program_id(0)
    ts = x_ref.shape[0]
    tm = TM_MIX
    prev = i + per - 1
    wg16_ref[...] = wg_ref[...].astype(BF16)
    wu16_ref[...] = wu_ref[...].astype(BF16)
    wd16_ref[...] = wd_ref[...].astype(BF16)

    @pl.when(i == 0)
    def _first():
        oprev_ref[...] = jnp.zeros(oprev_ref.shape, oprev_ref.dtype)

    @pl.when(i % tiles_per_seq == 0)
    def _seq_start():
        tail_ref[...] = jnp.zeros(tail_ref.shape, F32)
        sta_ref[...] = jnp.zeros(sta_ref.shape, F32)
        stb_ref[...] = jnp.zeros(stb_ref.shape, F32)

    @pl.when((i == 0) | (prev % per == 0))
    def _moe_tile_start():
        carry_ref[...] = jnp.zeros((1, LANES), F32)

    carry = carry_ref[...]
    nw = fw_ref[...]
    for hf in range(ts // tm):
        rows = slice(hf * tm, (hf + 1) * tm)
        x1, h2, rmt, carry = _route(x_ref[rows, :], oprev_ref[rows, :], wo_ref, nw, wrh_ref, wrl_ref, carry)
        for s in range(SUBLANES):
            x1t_ref[pl.ds(hf * tm * SUBLANES + s, tm, stride=SUBLANES), :] = x1[:, s * LANES:(s + 1) * LANES]
            h2t_ref[pl.ds(hf * tm * SUBLANES + s, tm, stride=SUBLANES), :] = h2[:, s * LANES:(s + 1) * LANES]
        rr_ref[(prev % per) * (ts // tm) + hf] = rmt
    carry_ref[...] = carry
    cnt_ref[...] = jnp.broadcast_to(carry, (SUBLANES, LANES)).astype(I32)

    (qa, ka, va, ga, qb, kb, vb, gb), gcol = _project(
        xc_ref[...], anw_ref, wm_ref, wsh_ref, wsl_ref, alog_ref, dtb_ref, cw_ref, tail_ref, grow_ref)
    oa = _gdn_heads(qa.astype(F32), ka.astype(F32), va.astype(F32), gcol, grow_ref[...], sta_ref)
    ob = _ret_heads(qb.astype(F32), kb.astype(F32), vb, cos_ref[...], sin_ref[...], stb_ref)
    oa = _gated_norm(oa, ga.astype(F32), na_ref[...])
    ob = _gated_norm(ob, gb.astype(F32), nb_ref[...])
    oprev_ref[...] = jnp.concatenate([oa, ob], axis=1).astype(oprev_ref.dtype)

    @pl.when((i > 0) & (prev % per == per - 1))
    def _moe_tile_end():
        _finish_slots(rr_ref, carry)


def _front(x2, attn_norm, w_main, ws_hi, ws_lo, alog_l, dtb_l, conv_a, w_gate, w_up, w_down,
           norm_a, cosf, sinf, norm_b, w_out16, ffn_norm, wr_hi, wr_lo, seq):
    t = x2.shape[0]
    ts = TS_FRONT
    n = t // ts
    nc = ts // CHUNK
    tiles_per_seq = seq // ts
    per = TS_MOE // ts
    assert N_EXPERTS % n == 0, "each tile step casts an equal share of the expert weights"
    epb = N_EXPERTS // n

    def cur(i):
        return jnp.minimum(i, n - 1)

    def prv(i):
        return jnp.maximum(i - 1, 0)

    def const(shape):
        return pl.BlockSpec(shape, lambda i: (0,) * len(shape))

    def wspec(shape):
        return pl.BlockSpec((epb,) + shape, lambda i: (cur(i), 0, 0))

    def wshape(shape):
        return jax.ShapeDtypeStruct((N_EXPERTS,) + shape, BF16)

    return pl.pallas_call(
        functools.partial(_front_kernel, tiles_per_seq=tiles_per_seq, per=per),
        grid=(n + 1,),
        in_specs=[
            pl.BlockSpec((ts, D_MODEL), lambda i: (cur(i), 0)),
            pl.BlockSpec((ts, D_MODEL), lambda i: (prv(i), 0)),
            const((1, D_MODEL)), const((D_MODEL, N_MAIN)), const((D_MODEL, LANES)), const((D_MODEL, LANES)),
            const((1, LANES)), const((1, LANES)), const((CONV_K, 3 * QK)),
            wspec((D_MODEL, D_EXPERT)), wspec((D_MODEL, D_EXPERT)), wspec((D_EXPERT, D_MODEL)),
            const((1, DH)),
            pl.BlockSpec((ts, DH), lambda i: (cur(i) % tiles_per_seq, 0)),
            pl.BlockSpec((ts, DH), lambda i: (cur(i) % tiles_per_seq, 0)),
            const((1, QK)),
            const((D_MODEL, D_MODEL)), const((1, D_MODEL)), const((D_MODEL, LANES)), const((D_MODEL, LANES)),
        ],
        out_specs=[
            pl.BlockSpec((ts * SUBLANES, LANES), lambda i: (prv(i), 0)),
            pl.BlockSpec((ts * SUBLANES, LANES), lambda i: (prv(i), 0)),
            pl.BlockSpec((TS_MOE // TM_MIX, SUBLANES, TM_MIX), lambda i: (prv(i) // per, 0, 0)),
            pl.BlockSpec((None, SUBLANES, LANES), lambda i: (prv(i) // per, 0, 0)),
            wspec((D_MODEL, D_EXPERT)), wspec((D_MODEL, D_EXPERT)), wspec((D_EXPERT, D_MODEL)),
        ],
        out_shape=[
            jax.ShapeDtypeStruct((t * SUBLANES, LANES), F32),
            jax.ShapeDtypeStruct((t * SUBLANES, LANES), F32),
            jax.ShapeDtypeStruct((t // TM_MIX, SUBLANES, TM_MIX), F32),
            jax.ShapeDtypeStruct((t // TS_MOE, SUBLANES, LANES), I32),
            wshape((D_MODEL, D_EXPERT)), wshape((D_MODEL, D_EXPERT)), wshape((D_EXPERT, D_MODEL)),
        ],
        scratch_shapes=[
            pltpu.VMEM((3, SUBLANES, QK), F32),
            pltpu.VMEM((HEADS, nc, 2, CHUNK), F32),
            pltpu.VMEM((HEADS, DH, DH), F32),
            pltpu.VMEM((HEADS, DH, DH), F32),
            pltpu.VMEM((ts, D_MODEL), BF16),
            pltpu.VMEM((1, LANES), F32),
        ],
        compiler_params=pltpu.CompilerParams(
            dimension_semantics=("arbitrary",), vmem_limit_bytes=VMEM_LIMIT_FRONT),
        name="front",
    )(x2, x2, attn_norm, w_main, ws_hi, ws_lo, alog_l, dtb_l, conv_a, w_gate, w_up, w_down,
      norm_a, cosf, sinf, norm_b, w_out16, ffn_norm, wr_hi, wr_lo)


def _moe_kernel(ps0_ref, ps1_ref, pc0_ref, pc1_ref, w0_ref, w1_ref, cnt_ref,
                h2t_ref, x1t_ref, wg_ref, wu_ref, wd_ref, fnw_ref,
                out_ref, stag_ref, acc_ref, base_ref):
    j = pl.program_id(1)
    sub = out_ref.shape[0]
    nsub = TS_MOE // sub
    npair = N_EXPERTS // EXPERTS_PER_STEP

    def row8(r):
        return pl.ds(pl.multiple_of(r * SUBLANES, SUBLANES), SUBLANES)

    def at(r8):
        return pl.ds(pl.multiple_of(r8, SUBLANES), SUBLANES)

    @pl.when(j == 0)
    def _bases():
        stag_ref[pl.ds(TOP_K * TS_MOE * SUBLANES, SLACK_MOE * SUBLANES), :] = jnp.zeros(
            (SLACK_MOE * SUBLANES, LANES), F32)

        def body(e, run):
            base_ref[e] = run
            return run + cnt_ref[e]

        lax.fori_loop(0, N_EXPERTS, body, jnp.int32(0))

    @pl.when(j < nsub)
    def _scatter():
        def body(t, carry):
            row = h2t_ref[row8(t), :]
            stag_ref[at(ps0_ref[t]), :] = row
            stag_ref[at(ps1_ref[t]), :] = row
            return carry

        lax.fori_loop(0, sub, body, 0, unroll=8)

    def load_rows(r0, m):
        xs = [stag_ref[pl.ds(r0 + s, m, stride=SUBLANES), :] for s in range(SUBLANES)]
        return xs, jnp.concatenate([v.astype(BF16) for v in xs], axis=1)

    def store_rows(r0, m, xs, y, nvalid):
        keep = lax.broadcasted_iota(I32, (m, LANES), 0) < nvalid
        for s in range(SUBLANES):
            stag_ref[pl.ds(r0 + s, m, stride=SUBLANES), :] = jnp.where(
                keep, y[:, s * LANES:(s + 1) * LANES], xs[s])

    @pl.when((j >= nsub) & (j < nsub + npair))
    def _experts():
        qs = range(EXPERTS_PER_STEP)
        es = [(j - nsub) * EXPERTS_PER_STEP + q for q in qs]
        ns = [cnt_ref[e] for e in es]
        bs = [base_ref[e] for e in es]
        m = CH_FIRST_MOE
        r0s = [pl.multiple_of(b * SUBLANES, SUBLANES) for b in bs]
        ld = [load_rows(r0, m) for r0 in r0s]
        gs = [_dot(ld[q][1], wg_ref[q]) for q in qs]
        us = [_dot(ld[q][1], wu_ref[q]) for q in qs]
        hs = [((gs[q] * _sigmoid(gs[q])) * us[q]).astype(BF16) for q in qs]
        ys = [_dot(hs[q], wd_ref[q]) for q in qs]
        for q in qs:
            store_rows(r0s[q], m, ld[q][0], ys[q], ns[q])
        ch = CH_MOE
        for q in qs:
            def chunk(c, carry, q=q):
                r0 = pl.multiple_of((bs[q] + m + c * ch) * SUBLANES, SUBLANES)
                xs, x16 = load_rows(r0, ch)
                g = _dot(x16, wg_ref[q])
                u = _dot(x16, wu_ref[q])
                y = _dot(((g * _sigmoid(g)) * u).astype(BF16), wd_ref[q])
                store_rows(r0, ch, xs, y, ns[q] - m - c * ch)
                return carry

            lax.fori_loop(0, (jnp.maximum(ns[q] - m, 0) + ch - 1) // ch, chunk, 0)

    @pl.when(j >= nsub + npair)
    def _combine():
        def body(t, carry):
            acc_ref[row8(t), :] = (x1t_ref[row8(t), :]
                                   + w0_ref[t] * stag_ref[at(pc0_ref[t]), :]
                                   + w1_ref[t] * stag_ref[at(pc1_ref[t]), :])
            return carry

        lax.fori_loop(0, sub, body, 0, unroll=8)
        xo = jnp.concatenate([acc_ref[pl.ds(s, sub, stride=SUBLANES), :] for s in range(SUBLANES)], axis=1)
        out_ref[...] = xo * lax.rsqrt(jnp.mean(xo * xo, axis=-1, keepdims=True) + EPS) * fnw_ref[...]


def _moe(p0, p1, w0, w1, cnt, h2t, x1t, wg16, wu16, wd16, final_norm):
    t = p0.shape[0]
    sub = SUB_MOE
    nsub = TS_MOE // sub
    npair = N_EXPERTS // EXPERTS_PER_STEP
    nt = t // TS_MOE
    smem = functools.partial(pl.BlockSpec, memory_space=pltpu.SMEM)

    def scat(i, j):
        return i * nsub + jnp.minimum(j, nsub - 1)

    def comb(i, j):
        return i * nsub + jnp.clip(j - nsub - npair, 0, nsub - 1)

    def pair(i, j):
        return jnp.clip(j - nsub, 0, npair - 1)

    def wspec(shape):
        return pl.BlockSpec((EXPERTS_PER_STEP,) + shape, lambda i, j: (pair(i, j), 0, 0))

    return pl.pallas_call(
        _moe_kernel,
        grid=(nt, nsub + npair + nsub),
        in_specs=[
            smem((sub,), lambda i, j: (scat(i, j),)), smem((sub,), lambda i, j: (scat(i, j),)),
            smem((sub,), lambda i, j: (comb(i, j),)), smem((sub,), lambda i, j: (comb(i, j),)),
            smem((sub,), lambda i, j: (comb(i, j),)), smem((sub,), lambda i, j: (comb(i, j),)),
            smem((LANES,), lambda i, j: (i,)),
            pl.BlockSpec((sub * SUBLANES, LANES), lambda i, j: (scat(i, j), 0)),
            pl.BlockSpec((sub * SUBLANES, LANES), lambda i, j: (comb(i, j), 0)),
            wspec((D_MODEL, D_EXPERT)), wspec((D_MODEL, D_EXPERT)), wspec((D_EXPERT, D_MODEL)),
            pl.BlockSpec((1, D_MODEL), lambda i, j: (0, 0)),
        ],
        out_specs=pl.BlockSpec((sub, D_MODEL), lambda i, j: (comb(i, j), 0)),
        out_shape=jax.ShapeDtypeStruct((t, D_MODEL), F32),
        scratch_shapes=[
            pltpu.VMEM(((TOP_K * TS_MOE + SLACK_MOE) * SUBLANES, LANES), F32),
            pltpu.VMEM((sub * SUBLANES, LANES), F32),
            pltpu.SMEM((LANES,), I32),
        ],
        compiler_params=pltpu.CompilerParams(
            dimension_semantics=("arbitrary", "arbitrary"), vmem_limit_bytes=VMEM_LIMIT_MOE),
        name="moe",
    )(p0, p1, p0, p1, w0, w1, cnt, h2t, x1t, wg16, wu16, wd16, final_norm)


def _pad_lanes(a, start):
    rows, n = a.shape
    return jnp.pad(a, ((0, 0), (start, LANES - start - n)))


def kernel(x, attn_norm, w_in, conv_a, a_log, dt_bias, norm_a, norm_b, w_out, ffn_norm,
           w_router_group, w_router_expert, w_gate, w_up, w_down, final_norm):
    batch, seq, d = x.shape
    t = batch * seq
    x2 = x.reshape(t, d)
    assert attn_norm.shape[0] == 1, "the closing RMSNorm is fused into the single layer's MoE combine step"

    pos = jnp.arange(seq, dtype=F32)
    inv_freq = ROPE_BASE ** (-jnp.arange(0, DH, 2, dtype=F32) / DH)
    ang = pos[:, None] * inv_freq[None, :]
    cos, sin = jnp.cos(ang), jnp.sin(ang)
    cosf = jnp.concatenate([cos, cos], axis=-1)
    sinf = jnp.concatenate([-sin, sin], axis=-1)

    wl = w_in[0]
    w_main = jnp.concatenate([wl[:, 0:4 * QK], wl[:, 4 * QK + 2 * HEADS:]], axis=1).astype(BF16)
    ws_hi, ws_lo = _split_bf16(_pad_lanes(wl[:, 4 * QK:4 * QK + 2 * HEADS], 0))
    alog_l = _pad_lanes(a_log[0][None, :], HEADS)
    dtb_l = _pad_lanes(dt_bias[0][None, :], HEADS)
    wr = jnp.concatenate(
        [_pad_lanes(w_router_group[0], 0)[:, 0:N_EXPERTS], w_router_expert[0]], axis=1)
    wr_hi, wr_lo = _split_bf16(wr)
    x1t, h2t, rr, cnt, wg16, wu16, wd16 = _front(
        x2, attn_norm[0][None, :], w_main, ws_hi, ws_lo, alog_l, dtb_l, conv_a[0], w_gate[0], w_up[0], w_down[0],
        norm_a[0][None, :], cosf, sinf, norm_b[0][None, :], w_out[0].astype(BF16), ffn_norm[0][None, :],
        wr_hi, wr_lo, seq)

    p0 = rr[:, 0, :].reshape(-1).astype(I32)
    p1 = rr[:, 1, :].reshape(-1).astype(I32)
    out = _moe(p0, p1, rr[:, 2, :].reshape(-1), rr[:, 3, :].reshape(-1), cnt[:, 0, :].reshape(-1), h2t, x1t,
               wg16, wu16, wd16, final_norm[None, :])
    return out.reshape(batch, seq, d)
```

```python
import functools
import math

import jax
import jax.numpy as jnp
from jax import lax
from jax.experimental import pallas as pl
from jax.experimental.pallas import tpu as pltpu

F32 = jnp.float32
BF16 = jnp.bfloat16
I32 = jnp.int32

D_MODEL = 1024
HEADS = 4
DH = 128
CONV_K = 4
CHUNK = 64
ROPE_BASE = 10000.0
N_GROUPS = 8
EPG = 8
N_EXPERTS = N_GROUPS * EPG
TOP_K = 2
D_EXPERT = 256
EPS = 1e-6
QK = HEADS * DH
N_MAIN = 8 * QK
LANES = 128
SUBLANES = 8

TS_FRONT = 512
CHUNK_RET = 256
TM_MIX = 256
TS_MOE = 4096
SUB_MOE = 256
CH_FIRST_MOE = 160
CH_MOE = 128
SLACK_MOE = CH_FIRST_MOE + CH_MOE
EXPERTS_PER_STEP = 4
RANK_BITS = 16
RANK_RADIX = 1 << RANK_BITS
VMEM_LIMIT_FRONT = 58 * 1024 * 1024
VMEM_LIMIT_MOE = 58 * 1024 * 1024
_LOG_GAMMA = tuple(math.log(1.0 - 2.0 ** (-5.0 - hh)) for hh in range(HEADS))


def _sigmoid(x):
    return 1.0 / (1.0 + jnp.exp(-x))


def _softplus(x):
    return jnp.maximum(x, 0.0) + jnp.log1p(jnp.exp(-jnp.abs(x)))


def _split_bf16(a):
    hi = a.astype(BF16)
    lo = (a - hi.astype(F32)).astype(BF16)
    return hi, lo


def _dot(a, b):
    return jnp.dot(a, b, preferred_element_type=F32)


def _dot_nt(a, b):
    return lax.dot_general(a, b, (((1,), (1,)), ((), ())), preferred_element_type=F32)


def _dot_tn(a, b):
    return lax.dot_general(a, b, (((0,), (0,)), ((), ())), preferred_element_type=F32)


def _bmm(a, b):
    return jnp.stack([_dot(a[c], b[c]) for c in range(a.shape[0])])


def _bmm_nt(a, b):
    return jnp.stack([_dot_nt(a[c], b[c]) for c in range(a.shape[0])])


def _project(x, nw_ref, wm_ref, wsh_ref, wsl_ref, alog_ref, dtb_ref, cw_ref, tail_ref, grow_ref):
    tm = x.shape[0]
    h = x * lax.rsqrt(jnp.mean(x * x, axis=-1, keepdims=True) + EPS) * nw_ref[...]
    hb = h.astype(BF16)
    blocks = [None] * (N_MAIN // QK)
    for j in (0, 3, 1, 4, 2, 5, 6, 7):
        blk = _dot(hb, wm_ref[:, j * QK:(j + 1) * QK])
        if j < 3:
            ext = jnp.concatenate([tail_ref[j], blk], axis=0)
            tail_ref[j] = blk[tm - SUBLANES:tm, :]
            w = cw_ref[:, j * QK:(j + 1) * QK]
            y = w[CONV_K - 1:CONV_K, :] * blk
            for jj in range(CONV_K - 1):
                y = y + w[jj:jj + 1, :] * pltpu.roll(ext, CONV_K - 1 - jj, axis=0)[SUBLANES:, :]
            blk = y * _sigmoid(y)
        if j < 2:
            scale = DH ** -0.5 if j == 0 else 1.0
            blk = jnp.concatenate(
                [blk[:, hh * DH:(hh + 1) * DH]
                 * (lax.rsqrt(jnp.sum(blk[:, hh * DH:(hh + 1) * DH] ** 2, axis=-1, keepdims=True) + EPS) * scale)
                 for hh in range(HEADS)], axis=1)
        blocks[j] = blk
    hl = (h - hb.astype(F32)).astype(BF16)
    sm = _dot(hb, wsh_ref[...]) + _dot(hb, wsl_ref[...]) + _dot(hl, wsh_ref[...])
    lane = lax.broadcasted_iota(I32, (tm, LANES), 1)
    row = lax.broadcasted_iota(I32, (tm, LANES), 0)
    beta = _sigmoid(sm)
    g = -jnp.exp(alog_ref[...]) * _softplus(sm + dtb_ref[...])
    rin = row & (CHUNK - 1)
    gc = g
    d = 1
    while d < CHUNK:
        gc = gc + jnp.where(rin >= d, pltpu.roll(gc, d, axis=0), 0.0)
        d *= 2
    out = jnp.where(lane < HEADS, beta, jnp.where(lane < 2 * HEADS, gc, 0.0))
    tr = out.T
    for hh in range(HEADS):
        for c in range(tm // CHUNK):
            grow_ref[hh, c, 0:1, :] = tr[hh:hh + 1, c * CHUNK:(c + 1) * CHUNK]
            grow_ref[hh, c, 1:2, :] = tr[HEADS + hh:HEADS + hh + 1, c * CHUNK:(c + 1) * CHUNK]
    return blocks, out


def _gated_norm(o, gate, nw):
    outs = []
    for h in range(HEADS):
        sl = slice(h * DH, (h + 1) * DH)
        oh = o[:, sl]
        wh = nw if nw.shape[1] == DH else nw[:, sl]
        oh = oh * lax.rsqrt(jnp.mean(oh * oh, axis=-1, keepdims=True) + EPS) * wh
        outs.append(oh * (gate[:, sl] * _sigmoid(gate[:, sl])))
    return jnp.concatenate(outs, axis=1)


def _gdn_heads(q, k, v, gblk, grow, st_ref):
    ts = q.shape[0]
    ncl = ts // CHUNK
    nc = HEADS * ncl

    def per_head(f):
        return jnp.concatenate([f(h, slice(h * DH, (h + 1) * DH)) for h in range(HEADS)], axis=0)

    lane = lax.broadcasted_iota(I32, (ts, LANES), 1)
    q3 = per_head(lambda h, sl: q[:, sl].reshape(ncl, CHUNK, DH))
    k3 = per_head(lambda h, sl: k[:, sl].reshape(ncl, CHUNK, DH))
    v3 = per_head(lambda h, sl: v[:, sl].reshape(ncl, CHUNK, DH))
    bc = per_head(lambda h, sl: jnp.sum(jnp.where(lane == h, gblk, 0.0), axis=-1, keepdims=True
                                        ).reshape(ncl, CHUNK, 1))
    gcc = per_head(lambda h, sl: jnp.sum(jnp.where(lane == h + HEADS, gblk, 0.0), axis=-1, keepdims=True
                                         ).reshape(ncl, CHUNK, 1))
    gr = grow.reshape(nc, 2, CHUNK)
    beta_r = gr[:, 0:1, :]
    gc_r = gr[:, 1:2, :]
    gl = gcc[:, CHUNK - 1:CHUNK, :]

    ii = lax.broadcasted_iota(I32, (nc, CHUNK, CHUNK), 1)
    jj = lax.broadcasted_iota(I32, (nc, CHUNK, CHUNK), 2)
    causal = ii >= jj
    strict = ii > jj
    dec = jnp.where(causal, jnp.exp(jnp.where(causal, gcc - gc_r, 0.0)), 0.0)

    k16 = k3.astype(BF16)
    q16 = q3.astype(BF16)
    v16 = v3.astype(BF16)
    gram = _bmm_nt(k16, k16)
    qk = _bmm_nt(q16, k16)
    a = jnp.where(strict, gram * dec * bc, 0.0)

    eye = (ii == jj).astype(F32)
    dm = eye - jnp.where((ii >> 1) == (jj >> 1), a, 0.0)
    sblk = 2
    while sblk < CHUNK:
        sh = sblk.bit_length() - 1
        off = jnp.where(((ii >> (sh + 1)) == (jj >> (sh + 1))) & ((ii >> sh) != (jj >> sh)), a, 0.0)
        d16 = dm.astype(BF16)
        t1 = _bmm(d16, off.astype(BF16))
        dm = dm - _bmm(t1.astype(BF16), d16)
        sblk *= 2

    tb = dm * beta_r
    tbe = tb * jnp.exp(gc_r)
    u = _bmm(tb.astype(BF16), v16)
    w = _bmm(tbe.astype(BF16), k16)
    attn = qk * dec
    wu = jnp.concatenate([w, u], axis=-1)
    wu16 = wu.astype(BF16)
    aw = _bmm(attn.astype(BF16), wu16)
    qe = q3 * jnp.exp(gcc) - aw[:, :, 0:DH]
    ol = aw[:, :, DH:2 * DH]
    kdec = k3 * jnp.exp(gl - gcc)

    trow = lax.broadcasted_iota(I32, (2 * CHUNK, DH), 0)
    xn = []
    for p in range(nc // 2):
        kd2 = kdec[2 * p:2 * p + 2].reshape(2 * CHUNK, DH)
        wu2 = wu16[2 * p:2 * p + 2].reshape(2 * CHUNK, 2 * DH)
        xn.append(_dot_tn(jnp.where(trow < CHUNK, kd2, 0.0).astype(BF16), wu2))
        xn.append(_dot_tn(jnp.where(trow >= CHUNK, kd2, 0.0).astype(BF16), wu2))

    sts = [st_ref[h] for h in range(HEADS)]
    outs = [[] for _ in range(HEADS)]
    for c in range(ncl):
        for h in range(HEADS):
            i = h * ncl + c
            s16 = sts[h].astype(BF16)
            outs[h].append(_dot(qe[i].astype(BF16), s16) + ol[i])
            sts[h] = jnp.exp(gl[i]) * sts[h] + (xn[i][:, DH:2 * DH] - _dot(xn[i][:, 0:DH].astype(BF16), s16))
    for h in range(HEADS):
        st_ref[h] = sts[h]
    return jnp.concatenate([jnp.concatenate(outs[h], axis=0) for h in range(HEADS)], axis=1)


def _ret_decay():
    c = CHUNK_RET
    ii = lax.broadcasted_iota(I32, (c, c), 0)
    jj = lax.broadcasted_iota(I32, (c, c), 1)
    rel = (ii - jj).astype(F32)
    return jnp.stack([jnp.where(rel >= 0, jnp.exp(jnp.maximum(rel, 0.0) * _LOG_GAMMA[h]), 0.0)
                      for h in range(HEADS)])


def _ret_heads(qa, ka, va, cosf, sinf, dec_ref, st_ref):
    rows = qa.shape[0]
    c = CHUNK_RET
    pos = lax.broadcasted_iota(I32, (c, 1), 0).astype(F32)
    decs = [dec_ref[h] for h in range(HEADS)]
    sts = [st_ref[h] for h in range(HEADS)]
    outs = [[] for _ in range(HEADS)]
    for ck in range(rows // c):
        rs = slice(ck * c, (ck + 1) * c)
        for h in range(HEADS):
            sl = slice(h * DH, (h + 1) * DH)
            lg = _LOG_GAMMA[h]
            q = qa[rs, sl] * cosf[rs] + pltpu.roll(qa[rs, sl], DH // 2, axis=1) * sinf[rs]
            k = (ka[rs, sl] * cosf[rs] + pltpu.roll(ka[rs, sl], DH // 2, axis=1) * sinf[rs]) * (DH ** -0.5)
            v16 = va[rs, sl]
            k16 = k.astype(BF16)
            attn = _dot_nt(q.astype(BF16), k16) * decs[h]
            inner = _dot(attn.astype(BF16), v16)
            qd = q * jnp.exp((pos + 1.0) * lg)
            cross = _dot(qd.astype(BF16), sts[h].astype(BF16))
            kd = k * jnp.exp((c - 1.0 - pos) * lg)
            sts[h] = sts[h] * math.exp(c * lg) + _dot_tn(kd.astype(BF16), v16)
            outs[h].append(cross + inner)
    for h in range(HEADS):
        st_ref[h] = sts[h]
    return jnp.concatenate([jnp.concatenate(outs[h], axis=0) for h in range(HEADS)], axis=1)


def _route(x, o16, wo_ref, nw, wrh_ref, wrl_ref, lstrict, carry):
    tm = x.shape[0]
    x1 = x + _dot(o16, wo_ref[...])
    h2 = x1 * lax.rsqrt(jnp.mean(x1 * x1, axis=-1, keepdims=True) + EPS) * nw
    hh, hl = _split_bf16(h2)
    lg = _dot(hh, wrh_ref[...]) + _dot(hh, wrl_ref[...]) + _dot(hl, wrh_ref[...])
    lane = lax.broadcasted_iota(I32, (tm, LANES), 1)
    big = jnp.int32(1 << 20)
    neg = jnp.float32(-jnp.inf)

    def argmax_first(vals, mask):
        mv = jnp.where(mask, vals, neg)
        m = jnp.max(mv, axis=-1, keepdims=True)
        idx = jnp.min(jnp.where(mask & (mv == m), lane, big), axis=-1, keepdims=True)
        return m, idx

    gmask = lane < N_GROUPS
    gm, gi = argmax_first(lg, gmask)
    gw = 1.0 / jnp.sum(jnp.where(gmask, jnp.exp(lg - gm), 0.0), axis=-1, keepdims=True)
    emask = (lane >= N_EXPERTS) & ((lane >> 3) == gi + N_EXPERTS // EPG)
    m1, i1 = argmax_first(lg, emask)
    m2, i2 = argmax_first(lg, emask & (lane != i1))
    e21 = jnp.exp(m2 - m1)
    w0 = gw / (1.0 + e21)
    w1 = gw * e21 / (1.0 + e21)
    e0 = i1 - N_EXPERTS
    e1 = i2 - N_EXPERTS

    oh = ((lane == e0) | (lane == e1)).astype(BF16)
    pref = _dot(lstrict, oh) + carry
    r0 = jnp.sum(jnp.where(lane == e0, pref, 0.0), axis=-1, keepdims=True)
    r1 = jnp.sum(jnp.where(lane == e1, pref, 0.0), axis=-1, keepdims=True)
    carry = carry + jnp.sum(oh.astype(F32), axis=0, keepdims=True)

    rmat = jnp.where(lane == 0, e0.astype(F32) * RANK_RADIX + r0,
                     jnp.where(lane == 1, e1.astype(F32) * RANK_RADIX + r1,
                               jnp.where(lane == 2, w0, jnp.where(lane == 3, w1, 0.0))))
    return x1, h2, rmat.T[0:SUBLANES, :], carry


def _finish_slots(rr_ref, carry):
    lane8 = lax.broadcasted_iota(I32, (SUBLANES, LANES), 1)
    cnt8 = jnp.broadcast_to(carry, (SUBLANES, LANES))
    inc = cnt8
    d = 1
    while d < LANES:
        inc = inc + jnp.where(lane8 >= d, pltpu.roll(inc, d, axis=1), 0.0)
        d *= 2
    base = inc - cnt8
    pk = rr_ref[:, 0:TOP_K, :]
    ex = jnp.floor(pk * (1.0 / RANK_RADIX))
    slot = pk - ex * RANK_RADIX
    for kk in range(N_EXPERTS):
        bk = jnp.sum(jnp.where(lane8 == kk, base, 0.0), axis=-1, keepdims=True)[0:1, :]
        slot = slot + jnp.where(ex == kk, bk, 0.0)
    rr_ref[:, 0:TOP_K, :] = slot * SUBLANES


def _front_kernel(xc_ref, x_ref, anw_ref, wm_ref, wsh_ref, wsl_ref, alog_ref, dtb_ref, cw_ref,
                  wg_ref, wu_ref, wd_ref, na_ref, cos_ref, sin_ref, nb_ref, wo_ref, fw_ref, wrh_ref, wrl_ref,
                  x1t_ref, h2t_ref, rr_ref, cnt_ref, wg16_ref, wu16_ref, wd16_ref,
                  tail_ref, grow_ref, sta_ref, stb_ref, oprev_ref, carry_ref, dec_ref, ls_ref,
                  *, tiles_per_seq, per):
    i = pl.program_id(0)
    ts = x_ref.shape[0]
    tm = TM_MIX
    prev = i + per - 1
    wg16_ref[...] = wg_ref[...].astype(BF16)
    wu16_ref[...] = wu_ref[...].astype(BF16)
    wd16_ref[...] = wd_ref[...].astype(BF16)

    @pl.when(i == 0)
    def _first():
        oprev_ref[...] = jnp.zeros(oprev_ref.shape, oprev_ref.dtype)
        dec_ref[...] = _ret_decay()
        ri = lax.broadcasted_iota(I32, (tm, tm), 0)
        ci = lax.broadcasted_iota(I32, (tm, tm), 1)
        ls_ref[...] = (ri > ci).astype(BF16)

    @pl.when(i % tiles_per_seq == 0)
    def _seq_start():
        tail_ref[...] = jnp.zeros(tail_ref.shape, F32)
        sta_ref[...] = jnp.zeros(sta_ref.shape, F32)
        stb_ref[...] = jnp.zeros(stb_ref.shape, F32)

    @pl.when((i == 0) | (prev % per == 0))
    def _moe_tile_start():
        carry_ref[...] = jnp.zeros((1, LANES), F32)

    carry = carry_ref[...]
    nw = fw_ref[...]
    for hf in range(ts // tm):
        rows = slice(hf * tm, (hf + 1) * tm)
        x1, h2, rmt, carry = _route(x_ref[rows, :], oprev_ref[rows, :], wo_ref, nw, wrh_ref, wrl_ref,
                                    ls_ref[...], carry)
        for s in range(SUBLANES):
            x1t_ref[pl.ds(hf * tm * SUBLANES + s, tm, stride=SUBLANES), :] = x1[:, s * LANES:(s + 1) * LANES]
            h2t_ref[pl.ds(hf * tm * SUBLANES + s, tm, stride=SUBLANES), :] = h2[:, s * LANES:(s + 1) * LANES]
        rr_ref[(prev % per) * (ts // tm) + hf] = rmt
    carry_ref[...] = carry
    cnt_ref[...] = jnp.broadcast_to(carry, (SUBLANES, LANES)).astype(I32)

    (qa, ka, va, ga, qb, kb, vb, gb), gcol = _project(
        xc_ref[...], anw_ref, wm_ref, wsh_ref, wsl_ref, alog_ref, dtb_ref, cw_ref, tail_ref, grow_ref)
    oa = _gdn_heads(qa, ka, va, gcol, grow_ref[...], sta_ref)
    ob = _ret_heads(qb, kb, vb.astype(BF16), cos_ref[...], sin_ref[...], dec_ref, stb_ref)
    oa = _gated_norm(oa, ga, na_ref[...])
    ob = _gated_norm(ob, gb, nb_ref[...])
    oprev_ref[...] = jnp.concatenate([oa, ob], axis=1).astype(oprev_ref.dtype)

    @pl.when((i > 0) & (prev % per == per - 1))
    def _moe_tile_end():
        _finish_slots(rr_ref, carry)


def _front(x2, attn_norm, w_main, ws_hi, ws_lo, alog_l, dtb_l, conv_a, w_gate, w_up, w_down,
           norm_a, cosf, sinf, norm_b, w_out16, ffn_norm, wr_hi, wr_lo, seq):
    t = x2.shape[0]
    ts = TS_FRONT
    n = t // ts
    nc = ts // CHUNK
    tiles_per_seq = seq // ts
    per = TS_MOE // ts
    assert N_EXPERTS % n == 0, "each tile step casts an equal share of the expert weights"
    epb = N_EXPERTS // n

    def cur(i):
        return jnp.minimum(i, n - 1)

    def prv(i):
        return jnp.maximum(i - 1, 0)

    def const(shape):
        return pl.BlockSpec(shape, lambda i: (0,) * len(shape))

    def wspec(shape):
        return pl.BlockSpec((epb,) + shape, lambda i: (cur(i), 0, 0))

    def wshape(shape):
        return jax.ShapeDtypeStruct((N_EXPERTS,) + shape, BF16)

    return pl.pallas_call(
        functools.partial(_front_kernel, tiles_per_seq=tiles_per_seq, per=per),
        grid=(n + 1,),
        in_specs=[
            pl.BlockSpec((ts, D_MODEL), lambda i: (cur(i), 0)),
            pl.BlockSpec((ts, D_MODEL), lambda i: (prv(i), 0)),
            const((1, D_MODEL)), const((D_MODEL, N_MAIN)), const((D_MODEL, LANES)), const((D_MODEL, LANES)),
            const((1, LANES)), const((1, LANES)), const((CONV_K, 3 * QK)),
            wspec((D_MODEL, D_EXPERT)), wspec((D_MODEL, D_EXPERT)), wspec((D_EXPERT, D_MODEL)),
            const((1, DH)),
            pl.BlockSpec((ts, DH), lambda i: (cur(i) % tiles_per_seq, 0)),
            pl.BlockSpec((ts, DH), lambda i: (cur(i) % tiles_per_seq, 0)),
            const((1, QK)),
            const((D_MODEL, D_MODEL)), const((1, D_MODEL)), const((D_MODEL, LANES)), const((D_MODEL, LANES)),
        ],
        out_specs=[
            pl.BlockSpec((ts * SUBLANES, LANES), lambda i: (prv(i), 0)),
            pl.BlockSpec((ts * SUBLANES, LANES), lambda i: (prv(i), 0)),
            pl.BlockSpec((TS_MOE // TM_MIX, SUBLANES, TM_MIX), lambda i: (prv(i) // per, 0, 0)),
            pl.BlockSpec((None, SUBLANES, LANES), lambda i: (prv(i) // per, 0, 0)),
            wspec((D_MODEL, D_EXPERT)), wspec((D_MODEL, D_EXPERT)), wspec((D_EXPERT, D_MODEL)),
        ],
        out_shape=[
            jax.ShapeDtypeStruct((t * SUBLANES, LANES), F32),
            jax.ShapeDtypeStruct((t * SUBLANES, LANES), F32),
            jax.ShapeDtypeStruct((t // TM_MIX, SUBLANES, TM_MIX), F32),
            jax.ShapeDtypeStruct((t // TS_MOE, SUBLANES, LANES), I32),
            wshape((D_MODEL, D_EXPERT)), wshape((D_MODEL, D_EXPERT)), wshape((D_EXPERT, D_MODEL)),
        ],
        scratch_shapes=[
            pltpu.VMEM((3, SUBLANES, QK), F32),
            pltpu.VMEM((HEADS, nc, 2, CHUNK), F32),
            pltpu.VMEM((HEADS, DH, DH), F32),
            pltpu.VMEM((HEADS, DH, DH), F32),
            pltpu.VMEM((ts, D_MODEL), BF16),
            pltpu.VMEM((1, LANES), F32),
            pltpu.VMEM((HEADS, CHUNK_RET, CHUNK_RET), F32),
            pltpu.VMEM((TM_MIX, TM_MIX), BF16),
        ],
        compiler_params=pltpu.CompilerParams(
            dimension_semantics=("arbitrary",), vmem_limit_bytes=VMEM_LIMIT_FRONT),
        name="front",
    )(x2, x2, attn_norm, w_main, ws_hi, ws_lo, alog_l, dtb_l, conv_a, w_gate, w_up, w_down,
      norm_a, cosf, sinf, norm_b, w_out16, ffn_norm, wr_hi, wr_lo)


def _moe_kernel(p0_ref, p1_ref, w0_ref, w1_ref, cnt_ref,
                h2t_hbm, x1t_hbm, wg_ref, wu_ref, wd_ref, fnw_ref,
                out_hbm, stag_ref, acc_ref, base_ref, ibuf_ref, obuf_ref, isem, osem):
    i = pl.program_id(0)
    j = pl.program_id(1)
    sub = SUB_MOE
    nsub = TS_MOE // sub
    npair = N_EXPERTS // EXPERTS_PER_STEP
    tok0 = i * TS_MOE

    def row8(r):
        return pl.ds(pl.multiple_of(r * SUBLANES, SUBLANES), SUBLANES)

    def at(r8):
        return pl.ds(pl.multiple_of(r8, SUBLANES), SUBLANES)

    def in_copy(src_hbm, k, slot):
        rows = pl.ds(pl.multiple_of((tok0 + k * sub) * SUBLANES, SUBLANES), sub * SUBLANES)
        return pltpu.make_async_copy(src_hbm.at[rows], ibuf_ref.at[slot], isem.at[slot])

    def out_copy(k, slot):
        rows = pl.ds(pl.multiple_of(tok0 + k * sub, SUBLANES), sub)
        return pltpu.make_async_copy(obuf_ref.at[slot], out_hbm.at[rows], osem.at[slot])

    @pl.when(j == 0)
    def _scatter():
        stag_ref[pl.ds(TOP_K * TS_MOE * SUBLANES, SLACK_MOE * SUBLANES), :] = jnp.zeros(
            (SLACK_MOE * SUBLANES, LANES), F32)

        def prefix(e, run):
            base_ref[e] = run
            return run + cnt_ref[e]

        lax.fori_loop(0, N_EXPERTS, prefix, jnp.int32(0))

        in_copy(h2t_hbm, 0, 0).start()
        for k in range(nsub):
            slot = k % 2
            if k + 1 < nsub:
                in_copy(h2t_hbm, k + 1, 1 - slot).start()
            in_copy(h2t_hbm, k, slot).wait()

            def body(t, carry, k=k, slot=slot):
                row = ibuf_ref[slot, row8(t), :]
                stag_ref[at(p0_ref[k * sub + t]), :] = row
                stag_ref[at(p1_ref[k * sub + t]), :] = row
                return carry

            lax.fori_loop(0, sub, body, 0, unroll=8)

    def load_rows(r0, m):
        xs = [stag_ref[pl.ds(r0 + s, m, stride=SUBLANES), :] for s in range(SUBLANES)]
        return xs, jnp.concatenate([v.astype(BF16) for v in xs], axis=1)

    def store_rows(r0, m, xs, y, nvalid):
        keep = lax.broadcasted_iota(I32, (m, LANES), 0) < nvalid
        for s in range(SUBLANES):
            stag_ref[pl.ds(r0 + s, m, stride=SUBLANES), :] = jnp.where(
                keep, y[:, s * LANES:(s + 1) * LANES], xs[s])

    qs = range(EXPERTS_PER_STEP)
    es = [j * EXPERTS_PER_STEP + q for q in qs]
    ns = [cnt_ref[e] for e in es]
    bs = [base_ref[e] for e in es]
    m = CH_FIRST_MOE
    r0s = [pl.multiple_of(b * SUBLANES, SUBLANES) for b in bs]
    ld = [load_rows(r0, m) for r0 in r0s]
    gs = [_dot(ld[q][1], wg_ref[q]) for q in qs]
    us = [_dot(ld[q][1], wu_ref[q]) for q in qs]
    hs = [((gs[q] * _sigmoid(gs[q])) * us[q]).astype(BF16) for q in qs]
    ys = [_dot(hs[q], wd_ref[q]) for q in qs]
    for q in qs:
        store_rows(r0s[q], m, ld[q][0], ys[q], ns[q])
    ch = CH_MOE
    for q in qs:
        def chunk(c, carry, q=q):
            r0 = pl.multiple_of((bs[q] + m + c * ch) * SUBLANES, SUBLANES)
            xs, x16 = load_rows(r0, ch)
            g = _dot(x16, wg_ref[q])
            u = _dot(x16, wu_ref[q])
            y = _dot(((g * _sigmoid(g)) * u).astype(BF16), wd_ref[q])
            store_rows(r0, ch, xs, y, ns[q] - m - c * ch)
            return carry

        lax.fori_loop(0, (jnp.maximum(ns[q] - m, 0) + ch - 1) // ch, chunk, 0)

    @pl.when(j == npair - 1)
    def _combine():
        in_copy(x1t_hbm, 0, 0).start()
        for k in range(nsub):
            slot = k % 2
            if k + 1 < nsub:
                in_copy(x1t_hbm, k + 1, 1 - slot).start()
            in_copy(x1t_hbm, k, slot).wait()

            def body(t, carry, k=k, slot=slot):
                acc_ref[row8(t), :] = (ibuf_ref[slot, row8(t), :]
                                       + w0_ref[k * sub + t] * stag_ref[at(p0_ref[k * sub + t]), :]
                                       + w1_ref[k * sub + t] * stag_ref[at(p1_ref[k * sub + t]), :])
                return carry

            lax.fori_loop(0, sub, body, 0, unroll=8)
            xo = jnp.concatenate([acc_ref[pl.ds(s, sub, stride=SUBLANES), :] for s in range(SUBLANES)], axis=1)
            if k >= 2:
                out_copy(k - 2, slot).wait()
            obuf_ref[slot] = xo * lax.rsqrt(jnp.mean(xo * xo, axis=-1, keepdims=True) + EPS) * fnw_ref[...]
            out_copy(k, slot).start()
        for k in (nsub - 2, nsub - 1):
            out_copy(k, k % 2).wait()


def _moe(p0, p1, w0, w1, cnt, h2t, x1t, wg16, wu16, wd16, final_norm):
    t = p0.shape[0]
    sub = SUB_MOE
    npair = N_EXPERTS // EXPERTS_PER_STEP
    nt = t // TS_MOE
    smem = functools.partial(pl.BlockSpec, memory_space=pltpu.SMEM)

    def wspec(shape):
        return pl.BlockSpec((EXPERTS_PER_STEP,) + shape, lambda i, j: (j, 0, 0))

    return pl.pallas_call(
        _moe_kernel,
        grid=(nt, npair),
        in_specs=[
            smem((TS_MOE,), lambda i, j: (i,)), smem((TS_MOE,), lambda i, j: (i,)),
            smem((TS_MOE,), lambda i, j: (i,)), smem((TS_MOE,), lambda i, j: (i,)),
            smem((LANES,), lambda i, j: (i,)),
            pl.BlockSpec(memory_space=pl.ANY),
            pl.BlockSpec(memory_space=pl.ANY),
            wspec((D_MODEL, D_EXPERT)), wspec((D_MODEL, D_EXPERT)), wspec((D_EXPERT, D_MODEL)),
            pl.BlockSpec((1, D_MODEL), lambda i, j: (0, 0)),
        ],
        out_specs=pl.BlockSpec(memory_space=pl.ANY),
        out_shape=jax.ShapeDtypeStruct((t, D_MODEL), F32),
        scratch_shapes=[
            pltpu.VMEM(((TOP_K * TS_MOE + SLACK_MOE) * SUBLANES, LANES), F32),
            pltpu.VMEM((sub * SUBLANES, LANES), F32),
            pltpu.SMEM((LANES,), I32),
            pltpu.VMEM((2, sub * SUBLANES, LANES), F32),
            pltpu.VMEM((2, sub, D_MODEL), F32),
            pltpu.SemaphoreType.DMA((2,)),
            pltpu.SemaphoreType.DMA((2,)),
        ],
        compiler_params=pltpu.CompilerParams(
            dimension_semantics=("arbitrary", "arbitrary"), vmem_limit_bytes=VMEM_LIMIT_MOE),
        name="moe",
    )(p0, p1, w0, w1, cnt, h2t, x1t, wg16, wu16, wd16, final_norm)


def _pad_lanes(a, start):
    rows, n = a.shape
    return jnp.pad(a, ((0, 0), (start, LANES - start - n)))


def kernel(x, attn_norm, w_in, conv_a, a_log, dt_bias, norm_a, norm_b, w_out, ffn_norm,
           w_router_group, w_router_expert, w_gate, w_up, w_down, final_norm):
    batch, seq, d = x.shape
    t = batch * seq
    x2 = x.reshape(t, d)
    assert attn_norm.shape[0] == 1, "the closing RMSNorm is fused into the single layer's MoE combine step"

    pos = jnp.arange(seq, dtype=F32)
    inv_freq = ROPE_BASE ** (-jnp.arange(0, DH, 2, dtype=F32) / DH)
    ang = pos[:, None] * inv_freq[None, :]
    cos, sin = jnp.cos(ang), jnp.sin(ang)
    cosf = jnp.concatenate([cos, cos], axis=-1)
    sinf = jnp.concatenate([-sin, sin], axis=-1)

    wl = w_in[0]
    w_main = jnp.concatenate([wl[:, 0:4 * QK], wl[:, 4 * QK + 2 * HEADS:]], axis=1).astype(BF16)
    ws_hi, ws_lo = _split_bf16(_pad_lanes(wl[:, 4 * QK:4 * QK + 2 * HEADS], 0))
    alog_l = _pad_lanes(a_log[0][None, :], HEADS)
    dtb_l = _pad_lanes(dt_bias[0][None, :], HEADS)
    wr = jnp.concatenate(
        [_pad_lanes(w_router_group[0], 0)[:, 0:N_EXPERTS], w_router_expert[0]], axis=1)
    wr_hi, wr_lo = _split_bf16(wr)
    x1t, h2t, rr, cnt, wg16, wu16, wd16 = _front(
        x2, attn_norm[0][None, :], w_main, ws_hi, ws_lo, alog_l, dtb_l, conv_a[0], w_gate[0], w_up[0], w_down[0],
        norm_a[0][None, :], cosf, sinf, norm_b[0][None, :], w_out[0].astype(BF16), ffn_norm[0][None, :],
        wr_hi, wr_lo, seq)

    p0 = rr[:, 0, :].reshape(-1).astype(I32)
    p1 = rr[:, 1, :].reshape(-1).astype(I32)
    out = _moe(p0, p1, rr[:, 2, :].reshape(-1), rr[:, 3, :].reshape(-1), cnt[:, 0, :].reshape(-1), h2t, x1t,
               wg16, wu16, wd16, final_norm[None, :])
    return out.reshape(batch, seq, d)
```

```python
import functools
import math

import jax
import jax.numpy as jnp
from jax import lax
from jax.experimental import pallas as pl
from jax.experimental.pallas import tpu as pltpu

F32 = jnp.float32
BF16 = jnp.bfloat16
I32 = jnp.int32

D_MODEL = 1024
HEADS = 4
DH = 128
CONV_K = 4
CHUNK = 64
ROPE_BASE = 10000.0
N_GROUPS = 8
EPG = 8
N_EXPERTS = N_GROUPS * EPG
TOP_K = 2
D_EXPERT = 256
EPS = 1e-6
QK = HEADS * DH
N_MAIN = 8 * QK
LANES = 128
SUBLANES = 8

TS_FRONT = 512
CHUNK_RET = 128
TM_MIX = 256
TS_MOE = 4096
SUB_MOE = 256
CH_FIRST_MOE = 160
CH_MOE = 128
SLACK_MOE = CH_FIRST_MOE + CH_MOE
EXPERTS_PER_STEP = 4
RANK_BITS = 16
RANK_RADIX = 1 << RANK_BITS
VMEM_LIMIT_FRONT = 58 * 1024 * 1024
VMEM_LIMIT_MOE = 58 * 1024 * 1024
_LOG_GAMMA = tuple(math.log(1.0 - 2.0 ** (-5.0 - hh)) for hh in range(HEADS))


def _sigmoid(x):
    return 1.0 / (1.0 + jnp.exp(-x))


def _softplus(x):
    return jnp.maximum(x, 0.0) + jnp.log1p(jnp.exp(-jnp.abs(x)))


def _split_bf16(a):
    hi = a.astype(BF16)
    lo = (a - hi.astype(F32)).astype(BF16)
    return hi, lo


def _dot(a, b):
    return jnp.dot(a, b, preferred_element_type=F32)


def _dot_nt(a, b):
    return lax.dot_general(a, b, (((1,), (1,)), ((), ())), preferred_element_type=F32)


def _dot_tn(a, b):
    return lax.dot_general(a, b, (((0,), (0,)), ((), ())), preferred_element_type=F32)


def _bmm(a, b):
    return jnp.stack([_dot(a[c], b[c]) for c in range(a.shape[0])])


def _bmm_nt(a, b):
    return jnp.stack([_dot_nt(a[c], b[c]) for c in range(a.shape[0])])


def _project(x, nw_ref, wm_ref, wsh_ref, wsl_ref, alog_ref, dtb_ref, cw_ref, tail_ref, grow_ref):
    tm = x.shape[0]
    h = x * lax.rsqrt(jnp.mean(x * x, axis=-1, keepdims=True) + EPS) * nw_ref[...]
    hb = h.astype(BF16)
    blocks = [None] * (N_MAIN // QK)
    for j in (0, 3, 1, 4, 2, 5, 6, 7):
        blk = _dot(hb, wm_ref[:, j * QK:(j + 1) * QK])
        if j < 3:
            ext = jnp.concatenate([tail_ref[j], blk], axis=0)
            tail_ref[j] = blk[tm - SUBLANES:tm, :]
            w = cw_ref[:, j * QK:(j + 1) * QK]
            y = w[CONV_K - 1:CONV_K, :] * blk
            for jj in range(CONV_K - 1):
                y = y + w[jj:jj + 1, :] * pltpu.roll(ext, CONV_K - 1 - jj, axis=0)[SUBLANES:, :]
            blk = y * _sigmoid(y)
        if j < 2:
            scale = DH ** -0.5 if j == 0 else 1.0
            blk = jnp.concatenate(
                [blk[:, hh * DH:(hh + 1) * DH]
                 * (lax.rsqrt(jnp.sum(blk[:, hh * DH:(hh + 1) * DH] ** 2, axis=-1, keepdims=True) + EPS) * scale)
                 for hh in range(HEADS)], axis=1)
        blocks[j] = blk
    sm = _dot(hb, wsh_ref[...]) + _dot(hb, wsl_ref[...])
    lane = lax.broadcasted_iota(I32, (tm, LANES), 1)
    row = lax.broadcasted_iota(I32, (tm, LANES), 0)
    beta = _sigmoid(sm)
    g = -jnp.exp(alog_ref[...]) * _softplus(sm + dtb_ref[...])
    rin = row & (CHUNK - 1)
    gc = g
    d = 1
    while d < CHUNK:
        gc = gc + jnp.where(rin >= d, pltpu.roll(gc, d, axis=0), 0.0)
        d *= 2
    out = jnp.where(lane < HEADS, beta, jnp.where(lane < 2 * HEADS, gc, 0.0))
    tr = out.T
    for hh in range(HEADS):
        for c in range(tm // CHUNK):
            grow_ref[hh, c, 0:1, :] = tr[hh:hh + 1, c * CHUNK:(c + 1) * CHUNK]
            grow_ref[hh, c, 1:2, :] = tr[HEADS + hh:HEADS + hh + 1, c * CHUNK:(c + 1) * CHUNK]
    return blocks, out


def _gated_norm(o, gate, nw):
    outs = []
    for h in range(HEADS):
        sl = slice(h * DH, (h + 1) * DH)
        oh = o[:, sl]
        wh = nw if nw.shape[1] == DH else nw[:, sl]
        oh = oh * lax.rsqrt(jnp.mean(oh * oh, axis=-1, keepdims=True) + EPS) * wh
        outs.append(oh * (gate[:, sl] * _sigmoid(gate[:, sl])))
    return jnp.concatenate(outs, axis=1)


def _gdn_heads(q, k, v, gblk, grow, st_ref):
    ts = q.shape[0]
    ncl = ts // CHUNK
    nc = HEADS * ncl

    def per_head(f):
        return jnp.concatenate([f(h, slice(h * DH, (h + 1) * DH)) for h in range(HEADS)], axis=0)

    lane = lax.broadcasted_iota(I32, (ts, LANES), 1)
    q3 = per_head(lambda h, sl: q[:, sl].reshape(ncl, CHUNK, DH))
    k3 = per_head(lambda h, sl: k[:, sl].reshape(ncl, CHUNK, DH))
    v3 = per_head(lambda h, sl: v[:, sl].reshape(ncl, CHUNK, DH))
    bc = per_head(lambda h, sl: jnp.sum(jnp.where(lane == h, gblk, 0.0), axis=-1, keepdims=True
                                        ).reshape(ncl, CHUNK, 1))
    gcc = per_head(lambda h, sl: jnp.sum(jnp.where(lane == h + HEADS, gblk, 0.0), axis=-1, keepdims=True
                                         ).reshape(ncl, CHUNK, 1))
    gr = grow.reshape(nc, 2, CHUNK)
    beta_r = gr[:, 0:1, :]
    gc_r = gr[:, 1:2, :]
    gl = gcc[:, CHUNK - 1:CHUNK, :]

    ii = lax.broadcasted_iota(I32, (nc, CHUNK, CHUNK), 1)
    jj = lax.broadcasted_iota(I32, (nc, CHUNK, CHUNK), 2)
    causal = ii >= jj
    strict = ii > jj
    dec = jnp.where(causal, jnp.exp(jnp.where(causal, gcc - gc_r, 0.0)), 0.0)

    k16 = k3.astype(BF16)
    q16 = q3.astype(BF16)
    v16 = v3.astype(BF16)
    gram = _bmm_nt(k16, k16)
    qk = _bmm_nt(q16, k16)
    a = jnp.where(strict, gram * dec * bc, 0.0)

    eye = (ii == jj).astype(F32)
    dm = eye - jnp.where((ii >> 1) == (jj >> 1), a, 0.0)
    sblk = 2
    while sblk < CHUNK:
        sh = sblk.bit_length() - 1
        off = jnp.where(((ii >> (sh + 1)) == (jj >> (sh + 1))) & ((ii >> sh) != (jj >> sh)), a, 0.0)
        d16 = dm.astype(BF16)
        t1 = _bmm(d16, off.astype(BF16))
        dm = dm - _bmm(t1.astype(BF16), d16)
        sblk *= 2

    tb = dm * beta_r
    tbe = tb * jnp.exp(gc_r)
    u = _bmm(tb.astype(BF16), v16)
    w = _bmm(tbe.astype(BF16), k16)
    attn = qk * dec
    wu = jnp.concatenate([w, u], axis=-1)
    wu16 = wu.astype(BF16)
    aw = _bmm(attn.astype(BF16), wu16)
    qe = q3 * jnp.exp(gcc) - aw[:, :, 0:DH]
    ol = aw[:, :, DH:2 * DH]
    kdec = k3 * jnp.exp(gl - gcc)

    trow = lax.broadcasted_iota(I32, (2 * CHUNK, DH), 0)
    xn = []
    for p in range(nc // 2):
        kd2 = kdec[2 * p:2 * p + 2].reshape(2 * CHUNK, DH)
        wu2 = wu16[2 * p:2 * p + 2].reshape(2 * CHUNK, 2 * DH)
        xn.append(_dot_tn(jnp.where(trow < CHUNK, kd2, 0.0).astype(BF16), wu2))
        xn.append(_dot_tn(jnp.where(trow >= CHUNK, kd2, 0.0).astype(BF16), wu2))

    sts = [st_ref[h] for h in range(HEADS)]
    outs = [[] for _ in range(HEADS)]
    for c in range(ncl):
        for h in range(HEADS):
            i = h * ncl + c
            s16 = sts[h].astype(BF16)
            outs[h].append(_dot(qe[i].astype(BF16), s16) + ol[i])
            sts[h] = jnp.exp(gl[i]) * sts[h] + (xn[i][:, DH:2 * DH] - _dot(xn[i][:, 0:DH].astype(BF16), s16))
    for h in range(HEADS):
        st_ref[h] = sts[h]
    return jnp.concatenate([jnp.concatenate(outs[h], axis=0) for h in range(HEADS)], axis=1)


def _ret_decay():
    c = CHUNK_RET
    ii = lax.broadcasted_iota(I32, (c, c), 0)
    jj = lax.broadcasted_iota(I32, (c, c), 1)
    rel = (ii - jj).astype(F32)
    return jnp.stack([jnp.where(rel >= 0, jnp.exp(jnp.maximum(rel, 0.0) * _LOG_GAMMA[h]), 0.0)
                      for h in range(HEADS)])


def _ret_heads(qa, ka, va, cosf, sinf, dec_ref, st_ref):
    rows = qa.shape[0]
    c = CHUNK_RET
    pos = lax.broadcasted_iota(I32, (c, 1), 0).astype(F32)
    decs = [dec_ref[h] for h in range(HEADS)]
    sts = [st_ref[h] for h in range(HEADS)]
    outs = [[] for _ in range(HEADS)]
    for ck in range(rows // c):
        rs = slice(ck * c, (ck + 1) * c)
        for h in range(HEADS):
            sl = slice(h * DH, (h + 1) * DH)
            lg = _LOG_GAMMA[h]
            q = qa[rs, sl] * cosf[rs] + pltpu.roll(qa[rs, sl], DH // 2, axis=1) * sinf[rs]
            k = (ka[rs, sl] * cosf[rs] + pltpu.roll(ka[rs, sl], DH // 2, axis=1) * sinf[rs]) * (DH ** -0.5)
            v16 = va[rs, sl]
            k16 = k.astype(BF16)
            attn = _dot_nt(q.astype(BF16), k16) * decs[h]
            inner = _dot(attn.astype(BF16), v16)
            qd = q * jnp.exp((pos + 1.0) * lg)
            cross = _dot(qd.astype(BF16), sts[h].astype(BF16))
            kd = k * jnp.exp((c - 1.0 - pos) * lg)
            sts[h] = sts[h] * math.exp(c * lg) + _dot_tn(kd.astype(BF16), v16)
            outs[h].append(cross + inner)
    for h in range(HEADS):
        st_ref[h] = sts[h]
    return jnp.concatenate([jnp.concatenate(outs[h], axis=0) for h in range(HEADS)], axis=1)


def _route(x, o16, wo_ref, nw, wrh_ref, wrl_ref, lstrict, carry):
    tm = x.shape[0]
    x1 = x + _dot(o16, wo_ref[...])
    h2 = x1 * lax.rsqrt(jnp.mean(x1 * x1, axis=-1, keepdims=True) + EPS) * nw
    hh = h2.astype(BF16)
    lg = _dot(hh, wrh_ref[...]) + _dot(hh, wrl_ref[...])
    lane = lax.broadcasted_iota(I32, (tm, LANES), 1)
    big = jnp.int32(1 << 20)
    neg = jnp.float32(-jnp.inf)

    def argmax_first(vals, mask):
        mv = jnp.where(mask, vals, neg)
        m = jnp.max(mv, axis=-1, keepdims=True)
        idx = jnp.min(jnp.where(mask & (mv == m), lane, big), axis=-1, keepdims=True)
        return m, idx

    gmask = lane < N_GROUPS
    gm, gi = argmax_first(lg, gmask)
    gw = 1.0 / jnp.sum(jnp.where(gmask, jnp.exp(lg - gm), 0.0), axis=-1, keepdims=True)
    emask = (lane >= N_EXPERTS) & ((lane >> 3) == gi + N_EXPERTS // EPG)
    m1, i1 = argmax_first(lg, emask)
    m2, i2 = argmax_first(lg, emask & (lane != i1))
    e21 = jnp.exp(m2 - m1)
    w0 = gw / (1.0 + e21)
    w1 = gw * e21 / (1.0 + e21)
    e0 = i1 - N_EXPERTS
    e1 = i2 - N_EXPERTS

    oh = ((lane == e0) | (lane == e1)).astype(BF16)
    pref = _dot(lstrict, oh) + carry
    r0 = jnp.sum(jnp.where(lane == e0, pref, 0.0), axis=-1, keepdims=True)
    r1 = jnp.sum(jnp.where(lane == e1, pref, 0.0), axis=-1, keepdims=True)
    carry = carry + jnp.sum(oh.astype(F32), axis=0, keepdims=True)

    rmat = jnp.where(lane == 0, e0.astype(F32) * RANK_RADIX + r0,
                     jnp.where(lane == 1, e1.astype(F32) * RANK_RADIX + r1,
                               jnp.where(lane == 2, w0, jnp.where(lane == 3, w1, 0.0))))
    return x1, h2, rmat.T[0:SUBLANES, :], carry


def _finish_slots(rr_ref, carry):
    lane8 = lax.broadcasted_iota(I32, (SUBLANES, LANES), 1)
    cnt8 = jnp.broadcast_to(carry, (SUBLANES, LANES))
    inc = cnt8
    d = 1
    while d < LANES:
        inc = inc + jnp.where(lane8 >= d, pltpu.roll(inc, d, axis=1), 0.0)
        d *= 2
    base = inc - cnt8
    pk = rr_ref[:, 0:TOP_K, :]
    ex = jnp.floor(pk * (1.0 / RANK_RADIX))
    slot = pk - ex * RANK_RADIX
    for kk in range(N_EXPERTS):
        bk = jnp.sum(jnp.where(lane8 == kk, base, 0.0), axis=-1, keepdims=True)[0:1, :]
        slot = slot + jnp.where(ex == kk, bk, 0.0)
    rr_ref[:, 0:TOP_K, :] = slot * SUBLANES


def _front_kernel(xc_ref, x_ref, anw_ref, wm_ref, wsh_ref, wsl_ref, alog_ref, dtb_ref, cw_ref,
                  wg_ref, wu_ref, wd_ref, na_ref, cos_ref, sin_ref, nb_ref, wo_ref, fw_ref, wrh_ref, wrl_ref,
                  x1t_ref, h2t_ref, rr_ref, cnt_ref, wg16_ref, wu16_ref, wd16_ref,
                  tail_ref, grow_ref, sta_ref, stb_ref, oprev_ref, carry_ref, dec_ref, ls_ref,
                  *, tiles_per_seq, per):
    i = pl.program_id(0)
    ts = x_ref.shape[0]
    tm = TM_MIX
    prev = i + per - 1
    wg16_ref[...] = wg_ref[...].astype(BF16)
    wu16_ref[...] = wu_ref[...].astype(BF16)
    wd16_ref[...] = wd_ref[...].astype(BF16)

    @pl.when(i == 0)
    def _first():
        oprev_ref[...] = jnp.zeros(oprev_ref.shape, oprev_ref.dtype)
        dec_ref[...] = _ret_decay()
        ri = lax.broadcasted_iota(I32, (tm, tm), 0)
        ci = lax.broadcasted_iota(I32, (tm, tm), 1)
        ls_ref[...] = (ri > ci).astype(BF16)

    @pl.when(i % tiles_per_seq == 0)
    def _seq_start():
        tail_ref[...] = jnp.zeros(tail_ref.shape, F32)
        sta_ref[...] = jnp.zeros(sta_ref.shape, F32)
        stb_ref[...] = jnp.zeros(stb_ref.shape, F32)

    @pl.when((i == 0) | (prev % per == 0))
    def _moe_tile_start():
        carry_ref[...] = jnp.zeros((1, LANES), F32)

    carry = carry_ref[...]
    nw = fw_ref[...]
    for hf in range(ts // tm):
        rows = slice(hf * tm, (hf + 1) * tm)
        x1, h2, rmt, carry = _route(x_ref[rows, :], oprev_ref[rows, :], wo_ref, nw, wrh_ref, wrl_ref,
                                    ls_ref[...], carry)
        for s in range(SUBLANES):
            x1t_ref[pl.ds(hf * tm * SUBLANES + s, tm, stride=SUBLANES), :] = x1[:, s * LANES:(s + 1) * LANES]
            h2t_ref[pl.ds(hf * tm * SUBLANES + s, tm, stride=SUBLANES), :] = h2[:, s * LANES:(s + 1) * LANES]
        rr_ref[(prev % per) * (ts // tm) + hf] = rmt
    carry_ref[...] = carry
    cnt_ref[...] = jnp.broadcast_to(carry, (SUBLANES, LANES)).astype(I32)

    (qa, ka, va, ga, qb, kb, vb, gb), gcol = _project(
        xc_ref[...], anw_ref, wm_ref, wsh_ref, wsl_ref, alog_ref, dtb_ref, cw_ref, tail_ref, grow_ref)
    oa = _gdn_heads(qa, ka, va, gcol, grow_ref[...], sta_ref)
    ob = _ret_heads(qb, kb, vb.astype(BF16), cos_ref[...], sin_ref[...], dec_ref, stb_ref)
    oa = _gated_norm(oa, ga, na_ref[...])
    ob = _gated_norm(ob, gb, nb_ref[...])
    oprev_ref[...] = jnp.concatenate([oa, ob], axis=1).astype(oprev_ref.dtype)

    @pl.when((i > 0) & (prev % per == per - 1))
    def _moe_tile_end():
        _finish_slots(rr_ref, carry)


def _front(x2, attn_norm, w_main, ws_hi, ws_lo, alog_l, dtb_l, conv_a, w_gate, w_up, w_down,
           norm_a, cosf, sinf, norm_b, w_out16, ffn_norm, wr_hi, wr_lo, seq):
    t = x2.shape[0]
    ts = TS_FRONT
    n = t // ts
    nc = ts // CHUNK
    tiles_per_seq = seq // ts
    per = TS_MOE // ts
    assert N_EXPERTS % n == 0, "each tile step casts an equal share of the expert weights"
    epb = N_EXPERTS // n

    def cur(i):
        return jnp.minimum(i, n - 1)

    def prv(i):
        return jnp.maximum(i - 1, 0)

    def const(shape):
        return pl.BlockSpec(shape, lambda i: (0,) * len(shape))

    def wspec(shape):
        return pl.BlockSpec((epb,) + shape, lambda i: (cur(i), 0, 0))

    def wshape(shape):
        return jax.ShapeDtypeStruct((N_EXPERTS,) + shape, BF16)

    return pl.pallas_call(
        functools.partial(_front_kernel, tiles_per_seq=tiles_per_seq, per=per),
        grid=(n + 1,),
        in_specs=[
            pl.BlockSpec((ts, D_MODEL), lambda i: (cur(i), 0)),
            pl.BlockSpec((ts, D_MODEL), lambda i: (prv(i), 0)),
            const((1, D_MODEL)), const((D_MODEL, N_MAIN)), const((D_MODEL, LANES)), const((D_MODEL, LANES)),
            const((1, LANES)), const((1, LANES)), const((CONV_K, 3 * QK)),
            wspec((D_MODEL, D_EXPERT)), wspec((D_MODEL, D_EXPERT)), wspec((D_EXPERT, D_MODEL)),
            const((1, DH)),
            pl.BlockSpec((ts, DH), lambda i: (cur(i) % tiles_per_seq, 0)),
            pl.BlockSpec((ts, DH), lambda i: (cur(i) % tiles_per_seq, 0)),
            const((1, QK)),
            const((D_MODEL, D_MODEL)), const((1, D_MODEL)), const((D_MODEL, LANES)), const((D_MODEL, LANES)),
        ],
        out_specs=[
            pl.BlockSpec((ts * SUBLANES, LANES), lambda i: (prv(i), 0)),
            pl.BlockSpec((ts * SUBLANES, LANES), lambda i: (prv(i), 0)),
            pl.BlockSpec((TS_MOE // TM_MIX, SUBLANES, TM_MIX), lambda i: (prv(i) // per, 0, 0)),
            pl.BlockSpec((None, SUBLANES, LANES), lambda i: (prv(i) // per, 0, 0)),
            wspec((D_MODEL, D_EXPERT)), wspec((D_MODEL, D_EXPERT)), wspec((D_EXPERT, D_MODEL)),
        ],
        out_shape=[
            jax.ShapeDtypeStruct((t * SUBLANES, LANES), F32),
            jax.ShapeDtypeStruct((t * SUBLANES, LANES), F32),
            jax.ShapeDtypeStruct((t // TM_MIX, SUBLANES, TM_MIX), F32),
            jax.ShapeDtypeStruct((t // TS_MOE, SUBLANES, LANES), I32),
            wshape((D_MODEL, D_EXPERT)), wshape((D_MODEL, D_EXPERT)), wshape((D_EXPERT, D_MODEL)),
        ],
        scratch_shapes=[
            pltpu.VMEM((3, SUBLANES, QK), F32),
            pltpu.VMEM((HEADS, nc, 2, CHUNK), F32),
            pltpu.VMEM((HEADS, DH, DH), F32),
            pltpu.VMEM((HEADS, DH, DH), F32),
            pltpu.VMEM((ts, D_MODEL), BF16),
            pltpu.VMEM((1, LANES), F32),
            pltpu.VMEM((HEADS, CHUNK_RET, CHUNK_RET), F32),
            pltpu.VMEM((TM_MIX, TM_MIX), BF16),
        ],
        compiler_params=pltpu.CompilerParams(
            dimension_semantics=("arbitrary",), vmem_limit_bytes=VMEM_LIMIT_FRONT),
        name="front",
    )(x2, x2, attn_norm, w_main, ws_hi, ws_lo, alog_l, dtb_l, conv_a, w_gate, w_up, w_down,
      norm_a, cosf, sinf, norm_b, w_out16, ffn_norm, wr_hi, wr_lo)


def _moe_kernel(p0_ref, p1_ref, w0_ref, w1_ref, cnt_ref,
                h2t_hbm, x1t_hbm, wg_ref, wu_ref, wd_ref, fnw_ref,
                out_hbm, stag_ref, acc_ref, base_ref, ibuf_ref, obuf_ref, isem, osem):
    i = pl.program_id(0)
    j = pl.program_id(1)
    sub = SUB_MOE
    nsub = TS_MOE // sub
    npair = N_EXPERTS // EXPERTS_PER_STEP
    tok0 = i * TS_MOE

    def row8(r):
        return pl.ds(pl.multiple_of(r * SUBLANES, SUBLANES), SUBLANES)

    def at(r8):
        return pl.ds(pl.multiple_of(r8, SUBLANES), SUBLANES)

    def in_copy(src_hbm, k, slot):
        rows = pl.ds(pl.multiple_of((tok0 + k * sub) * SUBLANES, SUBLANES), sub * SUBLANES)
        return pltpu.make_async_copy(src_hbm.at[rows], ibuf_ref.at[slot], isem.at[slot])

    def out_copy(k, slot):
        rows = pl.ds(pl.multiple_of(tok0 + k * sub, SUBLANES), sub)
        return pltpu.make_async_copy(obuf_ref.at[slot], out_hbm.at[rows], osem.at[slot])

    @pl.when(j == 0)
    def _scatter():
        stag_ref[pl.ds(TOP_K * TS_MOE * SUBLANES, SLACK_MOE * SUBLANES), :] = jnp.zeros(
            (SLACK_MOE * SUBLANES, LANES), F32)

        def prefix(e, run):
            base_ref[e] = run
            return run + cnt_ref[e]

        lax.fori_loop(0, N_EXPERTS, prefix, jnp.int32(0))

        in_copy(h2t_hbm, 0, 0).start()
        for k in range(nsub):
            slot = k % 2
            if k + 1 < nsub:
                in_copy(h2t_hbm, k + 1, 1 - slot).start()
            in_copy(h2t_hbm, k, slot).wait()

            def body(t, carry, k=k, slot=slot):
                row = ibuf_ref[slot, row8(t), :]
                stag_ref[at(p0_ref[k * sub + t]), :] = row
                stag_ref[at(p1_ref[k * sub + t]), :] = row
                return carry

            lax.fori_loop(0, sub, body, 0, unroll=8)

    def load_rows(r0, m):
        xs = [stag_ref[pl.ds(r0 + s, m, stride=SUBLANES), :] for s in range(SUBLANES)]
        return xs, jnp.concatenate([v.astype(BF16) for v in xs], axis=1)

    def store_rows(r0, m, xs, y, nvalid):
        keep = lax.broadcasted_iota(I32, (m, LANES), 0) < nvalid
        for s in range(SUBLANES):
            stag_ref[pl.ds(r0 + s, m, stride=SUBLANES), :] = jnp.where(
                keep, y[:, s * LANES:(s + 1) * LANES], xs[s])

    qs = range(EXPERTS_PER_STEP)
    es = [j * EXPERTS_PER_STEP + q for q in qs]
    ns = [cnt_ref[e] for e in es]
    bs = [base_ref[e] for e in es]
    m = CH_FIRST_MOE
    r0s = [pl.multiple_of(b * SUBLANES, SUBLANES) for b in bs]
    ld = [load_rows(r0, m) for r0 in r0s]
    gs = [_dot(ld[q][1], wg_ref[q]) for q in qs]
    us = [_dot(ld[q][1], wu_ref[q]) for q in qs]
    hs = [((gs[q] * _sigmoid(gs[q])) * us[q]).astype(BF16) for q in qs]
    ys = [_dot(hs[q], wd_ref[q]) for q in qs]
    for q in qs:
        store_rows(r0s[q], m, ld[q][0], ys[q], ns[q])
    ch = CH_MOE
    for q in qs:
        def chunk(c, carry, q=q):
            r0 = pl.multiple_of((bs[q] + m + c * ch) * SUBLANES, SUBLANES)
            xs, x16 = load_rows(r0, ch)
            g = _dot(x16, wg_ref[q])
            u = _dot(x16, wu_ref[q])
            y = _dot(((g * _sigmoid(g)) * u).astype(BF16), wd_ref[q])
            store_rows(r0, ch, xs, y, ns[q] - m - c * ch)
            return carry

        lax.fori_loop(0, (jnp.maximum(ns[q] - m, 0) + ch - 1) // ch, chunk, 0)

    @pl.when(j == npair - 1)
    def _combine():
        in_copy(x1t_hbm, 0, 0).start()
        for k in range(nsub):
            slot = k % 2
            if k + 1 < nsub:
                in_copy(x1t_hbm, k + 1, 1 - slot).start()
            in_copy(x1t_hbm, k, slot).wait()

            def body(t, carry, k=k, slot=slot):
                acc_ref[row8(t), :] = (ibuf_ref[slot, row8(t), :]
                                       + w0_ref[k * sub + t] * stag_ref[at(p0_ref[k * sub + t]), :]
                                       + w1_ref[k * sub + t] * stag_ref[at(p1_ref[k * sub + t]), :])
                return carry

            lax.fori_loop(0, sub, body, 0, unroll=8)
            xo = jnp.concatenate([acc_ref[pl.ds(s, sub, stride=SUBLANES), :] for s in range(SUBLANES)], axis=1)
            if k >= 2:
                out_copy(k - 2, slot).wait()
            obuf_ref[slot] = xo * lax.rsqrt(jnp.mean(xo * xo, axis=-1, keepdims=True) + EPS) * fnw_ref[...]
            out_copy(k, slot).start()
        for k in (nsub - 2, nsub - 1):
            out_copy(k, k % 2).wait()


def _moe(p0, p1, w0, w1, cnt, h2t, x1t, wg16, wu16, wd16, final_norm):
    t = p0.shape[0]
    sub = SUB_MOE
    npair = N_EXPERTS // EXPERTS_PER_STEP
    nt = t // TS_MOE
    smem = functools.partial(pl.BlockSpec, memory_space=pltpu.SMEM)

    def wspec(shape):
        return pl.BlockSpec((EXPERTS_PER_STEP,) + shape, lambda i, j: (j, 0, 0))

    return pl.pallas_call(
        _moe_kernel,
        grid=(nt, npair),
        in_specs=[
            smem((TS_MOE,), lambda i, j: (i,)), smem((TS_MOE,), lambda i, j: (i,)),
            smem((TS_MOE,), lambda i, j: (i,)), smem((TS_MOE,), lambda i, j: (i,)),
            smem((LANES,), lambda i, j: (i,)),
            pl.BlockSpec(memory_space=pl.ANY),
            pl.BlockSpec(memory_space=pl.ANY),
            wspec((D_MODEL, D_EXPERT)), wspec((D_MODEL, D_EXPERT)), wspec((D_EXPERT, D_MODEL)),
            pl.BlockSpec((1, D_MODEL), lambda i, j: (0, 0)),
        ],
        out_specs=pl.BlockSpec(memory_space=pl.ANY),
        out_shape=jax.ShapeDtypeStruct((t, D_MODEL), F32),
        scratch_shapes=[
            pltpu.VMEM(((TOP_K * TS_MOE + SLACK_MOE) * SUBLANES, LANES), F32),
            pltpu.VMEM((sub * SUBLANES, LANES), F32),
            pltpu.SMEM((LANES,), I32),
            pltpu.VMEM((2, sub * SUBLANES, LANES), F32),
            pltpu.VMEM((2, sub, D_MODEL), F32),
            pltpu.SemaphoreType.DMA((2,)),
            pltpu.SemaphoreType.DMA((2,)),
        ],
        compiler_params=pltpu.CompilerParams(
            dimension_semantics=("arbitrary", "arbitrary"), vmem_limit_bytes=VMEM_LIMIT_MOE),
        name="moe",
    )(p0, p1, w0, w1, cnt, h2t, x1t, wg16, wu16, wd16, final_norm)


def _pad_lanes(a, start):
    rows, n = a.shape
    return jnp.pad(a, ((0, 0), (start, LANES - start - n)))


def kernel(x, attn_norm, w_in, conv_a, a_log, dt_bias, norm_a, norm_b, w_out, ffn_norm,
           w_router_group, w_router_expert, w_gate, w_up, w_down, final_norm):
    batch, seq, d = x.shape
    t = batch * seq
    x2 = x.reshape(t, d)
    assert attn_norm.shape[0] == 1, "the closing RMSNorm is fused into the single layer's MoE combine step"

    pos = jnp.arange(seq, dtype=F32)
    inv_freq = ROPE_BASE ** (-jnp.arange(0, DH, 2, dtype=F32) / DH)
    ang = pos[:, None] * inv_freq[None, :]
    cos, sin = jnp.cos(ang), jnp.sin(ang)
    cosf = jnp.concatenate([cos, cos], axis=-1)
    sinf = jnp.concatenate([-sin, sin], axis=-1)

    wl = w_in[0]
    w_main = jnp.concatenate([wl[:, 0:4 * QK], wl[:, 4 * QK + 2 * HEADS:]], axis=1).astype(BF16)
    ws_hi, ws_lo = _split_bf16(_pad_lanes(wl[:, 4 * QK:4 * QK + 2 * HEADS], 0))
    alog_l = _pad_lanes(a_log[0][None, :], HEADS)
    dtb_l = _pad_lanes(dt_bias[0][None, :], HEADS)
    wr = jnp.concatenate(
        [_pad_lanes(w_router_group[0], 0)[:, 0:N_EXPERTS], w_router_expert[0]], axis=1)
    wr_hi, wr_lo = _split_bf16(wr)
    x1t, h2t, rr, cnt, wg16, wu16, wd16 = _front(
        x2, attn_norm[0][None, :], w_main, ws_hi, ws_lo, alog_l, dtb_l, conv_a[0], w_gate[0], w_up[0], w_down[0],
        norm_a[0][None, :], cosf, sinf, norm_b[0][None, :], w_out[0].astype(BF16), ffn_norm[0][None, :],
        wr_hi, wr_lo, seq)

    p0 = rr[:, 0, :].reshape(-1).astype(I32)
    p1 = rr[:, 1, :].reshape(-1).astype(I32)
    out = _moe(p0, p1, rr[:, 2, :].reshape(-1), rr[:, 3, :].reshape(-1), cnt[:, 0, :].reshape(-1), h2t, x1t,
               wg16, wu16, wd16, final_norm[None, :])
    return out.reshape(batch, seq, d)
```

```python
import functools
import math

import jax
import jax.numpy as jnp
from jax import lax
from jax.experimental import pallas as pl
from jax.experimental.pallas import tpu as pltpu

F32 = jnp.float32
BF16 = jnp.bfloat16
I32 = jnp.int32

D_MODEL = 1024
HEADS = 4
DH = 128
CONV_K = 4
CHUNK = 64
ROPE_BASE = 10000.0
N_GROUPS = 8
EPG = 8
N_EXPERTS = N_GROUPS * EPG
TOP_K = 2
D_EXPERT = 256
EPS = 1e-6
QK = HEADS * DH
N_MAIN = 8 * QK
LANES = 128
SUBLANES = 8

TS_FRONT = 512
CHUNK_RET = 128
TM_MIX = 256
TS_MOE = 4096
SUB_MOE = 256
CH_FIRST_MOE = 160
CH_MOE = 128
SLACK_MOE = CH_FIRST_MOE + CH_MOE
EXPERTS_PER_STEP = 4
RANK_BITS = 16
RANK_RADIX = 1 << RANK_BITS
VMEM_LIMIT_FRONT = 58 * 1024 * 1024
VMEM_LIMIT_MOE = 58 * 1024 * 1024
_LOG_GAMMA = tuple(math.log(1.0 - 2.0 ** (-5.0 - hh)) for hh in range(HEADS))


def _sigmoid(x):
    return 1.0 / (1.0 + jnp.exp(-x))


def _softplus(x):
    return jnp.maximum(x, 0.0) + jnp.log1p(jnp.exp(-jnp.abs(x)))


def _split_bf16(a):
    hi = a.astype(BF16)
    lo = (a - hi.astype(F32)).astype(BF16)
    return hi, lo


def _dot(a, b):
    return jnp.dot(a, b, preferred_element_type=F32)


def _dot_nt(a, b):
    return lax.dot_general(a, b, (((1,), (1,)), ((), ())), preferred_element_type=F32)


def _dot_tn(a, b):
    return lax.dot_general(a, b, (((0,), (0,)), ((), ())), preferred_element_type=F32)


def _bmm(a, b):
    return jnp.stack([_dot(a[c], b[c]) for c in range(a.shape[0])])


def _bmm_nt(a, b):
    return jnp.stack([_dot_nt(a[c], b[c]) for c in range(a.shape[0])])


def _project(x, nw_ref, wm_ref, wsh_ref, wsl_ref, alog_ref, dtb_ref, cw_ref, tail_ref, grow_ref):
    tm = x.shape[0]
    h = x * lax.rsqrt(jnp.mean(x * x, axis=-1, keepdims=True) + EPS) * nw_ref[...]
    hb = h.astype(BF16)
    blocks = [None] * (N_MAIN // QK)
    for j in (0, 3, 1, 4, 2, 5, 6, 7):
        blk = _dot(hb, wm_ref[:, j * QK:(j + 1) * QK])
        if j < 3:
            ext = jnp.concatenate([tail_ref[j], blk], axis=0)
            tail_ref[j] = blk[tm - SUBLANES:tm, :]
            w = cw_ref[:, j * QK:(j + 1) * QK]
            y = w[CONV_K - 1:CONV_K, :] * blk
            for jj in range(CONV_K - 1):
                y = y + w[jj:jj + 1, :] * pltpu.roll(ext, CONV_K - 1 - jj, axis=0)[SUBLANES:, :]
            blk = y * _sigmoid(y)
        if j < 2:
            scale = DH ** -0.5 if j == 0 else 1.0
            blk = jnp.concatenate(
                [blk[:, hh * DH:(hh + 1) * DH]
                 * (lax.rsqrt(jnp.sum(blk[:, hh * DH:(hh + 1) * DH] ** 2, axis=-1, keepdims=True) + EPS) * scale)
                 for hh in range(HEADS)], axis=1)
        blocks[j] = blk
    sm = _dot(hb, wsh_ref[...]) + _dot(hb, wsl_ref[...])
    lane = lax.broadcasted_iota(I32, (tm, LANES), 1)
    row = lax.broadcasted_iota(I32, (tm, LANES), 0)
    beta = _sigmoid(sm)
    g = -jnp.exp(alog_ref[...]) * _softplus(sm + dtb_ref[...])
    rin = row & (CHUNK - 1)
    gc = g
    d = 1
    while d < CHUNK:
        gc = gc + jnp.where(rin >= d, pltpu.roll(gc, d, axis=0), 0.0)
        d *= 2
    out = jnp.where(lane < HEADS, beta, jnp.where(lane < 2 * HEADS, gc, 0.0))
    tr = out.T
    for hh in range(HEADS):
        for c in range(tm // CHUNK):
            grow_ref[hh, c, 0:1, :] = tr[hh:hh + 1, c * CHUNK:(c + 1) * CHUNK]
            grow_ref[hh, c, 1:2, :] = tr[HEADS + hh:HEADS + hh + 1, c * CHUNK:(c + 1) * CHUNK]
    return blocks, out


def _gated_norm(o, gate, nw):
    outs = []
    for h in range(HEADS):
        sl = slice(h * DH, (h + 1) * DH)
        oh = o[:, sl]
        wh = nw if nw.shape[1] == DH else nw[:, sl]
        oh = oh * lax.rsqrt(jnp.mean(oh * oh, axis=-1, keepdims=True) + EPS) * wh
        outs.append(oh * (gate[:, sl] * _sigmoid(gate[:, sl])))
    return jnp.concatenate(outs, axis=1)


def _gdn_heads(q, k, v, gblk, grow, st_ref):
    ts = q.shape[0]
    ncl = ts // CHUNK
    nc = HEADS * ncl

    def per_head(f):
        return jnp.concatenate([f(h, slice(h * DH, (h + 1) * DH)) for h in range(HEADS)], axis=0)

    lane = lax.broadcasted_iota(I32, (ts, LANES), 1)
    q3 = per_head(lambda h, sl: q[:, sl].reshape(ncl, CHUNK, DH))
    k3 = per_head(lambda h, sl: k[:, sl].reshape(ncl, CHUNK, DH))
    v3 = per_head(lambda h, sl: v[:, sl].reshape(ncl, CHUNK, DH))
    bc = per_head(lambda h, sl: jnp.sum(jnp.where(lane == h, gblk, 0.0), axis=-1, keepdims=True
                                        ).reshape(ncl, CHUNK, 1))
    gcc = per_head(lambda h, sl: jnp.sum(jnp.where(lane == h + HEADS, gblk, 0.0), axis=-1, keepdims=True
                                         ).reshape(ncl, CHUNK, 1))
    gr = grow.reshape(nc, 2, CHUNK)
    beta_r = gr[:, 0:1, :]
    gc_r = gr[:, 1:2, :]
    gl = gcc[:, CHUNK - 1:CHUNK, :]

    ii = lax.broadcasted_iota(I32, (nc, CHUNK, CHUNK), 1)
    jj = lax.broadcasted_iota(I32, (nc, CHUNK, CHUNK), 2)
    causal = ii >= jj
    strict = ii > jj
    dec = jnp.where(causal, jnp.exp(jnp.where(causal, gcc - gc_r, 0.0)), 0.0)

    k16 = k3.astype(BF16)
    q16 = q3.astype(BF16)
    v16 = v3.astype(BF16)
    gram = _bmm_nt(k16, k16)
    qk = _bmm_nt(q16, k16)
    a = jnp.where(strict, gram * dec * bc, 0.0)

    eye = (ii == jj).astype(F32)
    dm = eye - jnp.where((ii >> 1) == (jj >> 1), a, 0.0)
    sblk = 2
    while sblk < CHUNK:
        sh = sblk.bit_length() - 1
        off = jnp.where(((ii >> (sh + 1)) == (jj >> (sh + 1))) & ((ii >> sh) != (jj >> sh)), a, 0.0)
        d16 = dm.astype(BF16)
        t1 = _bmm(d16, off.astype(BF16))
        dm = dm - _bmm(t1.astype(BF16), d16)
        sblk *= 2

    tb = dm * beta_r
    tbe = tb * jnp.exp(gc_r)
    u = _bmm(tb.astype(BF16), v16)
    w = _bmm(tbe.astype(BF16), k16)
    attn = qk * dec
    wu = jnp.concatenate([w, u], axis=-1)
    wu16 = wu.astype(BF16)
    aw = _bmm(attn.astype(BF16), wu16)
    qe = q3 * jnp.exp(gcc) - aw[:, :, 0:DH]
    ol = aw[:, :, DH:2 * DH]
    kdec = k3 * jnp.exp(gl - gcc)

    trow = lax.broadcasted_iota(I32, (2 * CHUNK, DH), 0)
    xn = []
    for p in range(nc // 2):
        kd2 = kdec[2 * p:2 * p + 2].reshape(2 * CHUNK, DH)
        wu2 = wu16[2 * p:2 * p + 2].reshape(2 * CHUNK, 2 * DH)
        xn.append(_dot_tn(jnp.where(trow < CHUNK, kd2, 0.0).astype(BF16), wu2))
        xn.append(_dot_tn(jnp.where(trow >= CHUNK, kd2, 0.0).astype(BF16), wu2))

    sts = [st_ref[h] for h in range(HEADS)]
    outs = [[] for _ in range(HEADS)]
    for c in range(ncl):
        for h in range(HEADS):
            i = h * ncl + c
            s16 = sts[h].astype(BF16)
            outs[h].append(_dot(qe[i].astype(BF16), s16) + ol[i])
            sts[h] = jnp.exp(gl[i]) * sts[h] + (xn[i][:, DH:2 * DH] - _dot(xn[i][:, 0:DH].astype(BF16), s16))
    for h in range(HEADS):
        st_ref[h] = sts[h]
    return jnp.concatenate([jnp.concatenate(outs[h], axis=0) for h in range(HEADS)], axis=1)


def _ret_decay():
    c = CHUNK_RET
    ii = lax.broadcasted_iota(I32, (c, c), 0)
    jj = lax.broadcasted_iota(I32, (c, c), 1)
    rel = (ii - jj).astype(F32)
    return jnp.stack([jnp.where(rel >= 0, jnp.exp(jnp.maximum(rel, 0.0) * _LOG_GAMMA[h]), 0.0)
                      for h in range(HEADS)])


def _ret_heads(qa, ka, va, cosf, sinf, dec_ref, st_ref):
    rows = qa.shape[0]
    c = CHUNK_RET
    pos = lax.broadcasted_iota(I32, (c, 1), 0).astype(F32)
    decs = [dec_ref[h] for h in range(HEADS)]
    sts = [st_ref[h] for h in range(HEADS)]
    outs = [[] for _ in range(HEADS)]
    for ck in range(rows // c):
        rs = slice(ck * c, (ck + 1) * c)
        for h in range(HEADS):
            sl = slice(h * DH, (h + 1) * DH)
            lg = _LOG_GAMMA[h]
            q = qa[rs, sl] * cosf[rs] + pltpu.roll(qa[rs, sl], DH // 2, axis=1) * sinf[rs]
            k = (ka[rs, sl] * cosf[rs] + pltpu.roll(ka[rs, sl], DH // 2, axis=1) * sinf[rs]) * (DH ** -0.5)
            v16 = va[rs, sl]
            k16 = k.astype(BF16)
            attn = _dot_nt(q.astype(BF16), k16) * decs[h]
            inner = _dot(attn.astype(BF16), v16)
            qd = q * jnp.exp((pos + 1.0) * lg)
            cross = _dot(qd.astype(BF16), sts[h].astype(BF16))
            kd = k * jnp.exp((c - 1.0 - pos) * lg)
            sts[h] = sts[h] * math.exp(c * lg) + _dot_tn(kd.astype(BF16), v16)
            outs[h].append(cross + inner)
    for h in range(HEADS):
        st_ref[h] = sts[h]
    return jnp.concatenate([jnp.concatenate(outs[h], axis=0) for h in range(HEADS)], axis=1)


def _route(x, o16, wo_ref, nw, wrh_ref, wrl_ref, lstrict, carry):
    tm = x.shape[0]
    x1 = x + _dot(o16, wo_ref[...])
    h2 = x1 * lax.rsqrt(jnp.mean(x1 * x1, axis=-1, keepdims=True) + EPS) * nw
    hh = h2.astype(BF16)
    lg = _dot(hh, wrh_ref[...]) + _dot(hh, wrl_ref[...])
    lane = lax.broadcasted_iota(I32, (tm, LANES), 1)
    big = jnp.int32(1 << 20)
    neg = jnp.float32(-jnp.inf)

    def argmax_first(vals, mask):
        mv = jnp.where(mask, vals, neg)
        m = jnp.max(mv, axis=-1, keepdims=True)
        idx = jnp.min(jnp.where(mask & (mv == m), lane, big), axis=-1, keepdims=True)
        return m, idx

    gmask = lane < N_GROUPS
    gm, gi = argmax_first(lg, gmask)
    gw = 1.0 / jnp.sum(jnp.where(gmask, jnp.exp(lg - gm), 0.0), axis=-1, keepdims=True)
    emask = (lane >= N_EXPERTS) & ((lane >> 3) == gi + N_EXPERTS // EPG)
    m1, i1 = argmax_first(lg, emask)
    m2, i2 = argmax_first(lg, emask & (lane != i1))
    e21 = jnp.exp(m2 - m1)
    w0 = gw / (1.0 + e21)
    w1 = gw * e21 / (1.0 + e21)
    e0 = i1 - N_EXPERTS
    e1 = i2 - N_EXPERTS

    oh = ((lane == e0) | (lane == e1)).astype(BF16)
    pref = _dot(lstrict, oh) + carry
    r0 = jnp.sum(jnp.where(lane == e0, pref, 0.0), axis=-1, keepdims=True)
    r1 = jnp.sum(jnp.where(lane == e1, pref, 0.0), axis=-1, keepdims=True)
    carry = carry + jnp.sum(oh.astype(F32), axis=0, keepdims=True)

    rmat = jnp.where(lane == 0, e0.astype(F32) * RANK_RADIX + r0,
                     jnp.where(lane == 1, e1.astype(F32) * RANK_RADIX + r1,
                               jnp.where(lane == 2, w0, jnp.where(lane == 3, w1, 0.0))))
    return x1, h2, rmat.T[0:SUBLANES, :], carry


def _finish_slots(rr_ref, carry):
    lane8 = lax.broadcasted_iota(I32, (SUBLANES, LANES), 1)
    cnt8 = jnp.broadcast_to(carry, (SUBLANES, LANES))
    inc = cnt8
    d = 1
    while d < LANES:
        inc = inc + jnp.where(lane8 >= d, pltpu.roll(inc, d, axis=1), 0.0)
        d *= 2
    base = inc - cnt8
    pk = rr_ref[:, 0:TOP_K, :]
    ex = jnp.floor(pk * (1.0 / RANK_RADIX))
    slot = pk - ex * RANK_RADIX
    for kk in range(N_EXPERTS):
        bk = jnp.sum(jnp.where(lane8 == kk, base, 0.0), axis=-1, keepdims=True)[0:1, :]
        slot = slot + jnp.where(ex == kk, bk, 0.0)
    rr_ref[:, 0:TOP_K, :] = slot * SUBLANES


def _front_kernel(xc_ref, x_ref, anw_ref, wm_ref, wsh_ref, wsl_ref, alog_ref, dtb_ref, cw_ref,
                  wg_ref, wu_ref, wd_ref, na_ref, cos_ref, sin_ref, nb_ref, wo_ref, fw_ref, wrh_ref, wrl_ref,
                  x1t_ref, h2t_ref, rr_ref, cnt_ref, wg16_ref, wu16_ref, wd16_ref,
                  tail_ref, grow_ref, sta_ref, stb_ref, oprev_ref, carry_ref, dec_ref, ls_ref,
                  *, tiles_per_seq, per):
    i = pl.program_id(0)
    ts = x_ref.shape[0]
    tm = TM_MIX
    prev = i + per - 1
    wg16_ref[...] = wg_ref[...].astype(BF16)
    wu16_ref[...] = wu_ref[...].astype(BF16)
    wd16_ref[...] = wd_ref[...].astype(BF16)

    @pl.when(i == 0)
    def _first():
        oprev_ref[...] = jnp.zeros(oprev_ref.shape, oprev_ref.dtype)
        dec_ref[...] = _ret_decay()
        ri = lax.broadcasted_iota(I32, (tm, tm), 0)
        ci = lax.broadcasted_iota(I32, (tm, tm), 1)
        ls_ref[...] = (ri > ci).astype(BF16)

    @pl.when(i % tiles_per_seq == 0)
    def _seq_start():
        tail_ref[...] = jnp.zeros(tail_ref.shape, F32)
        sta_ref[...] = jnp.zeros(sta_ref.shape, F32)
        stb_ref[...] = jnp.zeros(stb_ref.shape, F32)

    @pl.when((i == 0) | (prev % per == 0))
    def _moe_tile_start():
        carry_ref[...] = jnp.zeros((1, LANES), F32)

    (qa, ka, va, ga, qb, kb, vb, gb), gcol = _project(
        xc_ref[...], anw_ref, wm_ref, wsh_ref, wsl_ref, alog_ref, dtb_ref, cw_ref, tail_ref, grow_ref)

    carry = carry_ref[...]
    nw = fw_ref[...]
    for hf in range(ts // tm):
        rows = slice(hf * tm, (hf + 1) * tm)
        x1, h2, rmt, carry = _route(x_ref[rows, :], oprev_ref[rows, :], wo_ref, nw, wrh_ref, wrl_ref,
                                    ls_ref[...], carry)
        for s in range(SUBLANES):
            x1t_ref[pl.ds(hf * tm * SUBLANES + s, tm, stride=SUBLANES), :] = x1[:, s * LANES:(s + 1) * LANES]
            h2t_ref[pl.ds(hf * tm * SUBLANES + s, tm, stride=SUBLANES), :] = h2[:, s * LANES:(s + 1) * LANES]
        rr_ref[(prev % per) * (ts // tm) + hf] = rmt
    carry_ref[...] = carry
    cnt_ref[...] = jnp.broadcast_to(carry, (SUBLANES, LANES)).astype(I32)

    oa = _gdn_heads(qa, ka, va, gcol, grow_ref[...], sta_ref)
    ob = _ret_heads(qb, kb, vb.astype(BF16), cos_ref[...], sin_ref[...], dec_ref, stb_ref)
    oa = _gated_norm(oa, ga, na_ref[...])
    ob = _gated_norm(ob, gb, nb_ref[...])
    oprev_ref[...] = jnp.concatenate([oa, ob], axis=1).astype(oprev_ref.dtype)

    @pl.when((i > 0) & (prev % per == per - 1))
    def _moe_tile_end():
        _finish_slots(rr_ref, carry)


def _front(x2, attn_norm, w_main, ws_hi, ws_lo, alog_l, dtb_l, conv_a, w_gate, w_up, w_down,
           norm_a, cosf, sinf, norm_b, w_out16, ffn_norm, wr_hi, wr_lo, seq):
    t = x2.shape[0]
    ts = TS_FRONT
    n = t // ts
    nc = ts // CHUNK
    tiles_per_seq = seq // ts
    per = TS_MOE // ts
    assert N_EXPERTS % n == 0, "each tile step casts an equal share of the expert weights"
    epb = N_EXPERTS // n

    def cur(i):
        return jnp.minimum(i, n - 1)

    def prv(i):
        return jnp.maximum(i - 1, 0)

    def const(shape):
        return pl.BlockSpec(shape, lambda i: (0,) * len(shape))

    def wspec(shape):
        return pl.BlockSpec((epb,) + shape, lambda i: (cur(i), 0, 0))

    def wshape(shape):
        return jax.ShapeDtypeStruct((N_EXPERTS,) + shape, BF16)

    return pl.pallas_call(
        functools.partial(_front_kernel, tiles_per_seq=tiles_per_seq, per=per),
        grid=(n + 1,),
        in_specs=[
            pl.BlockSpec((ts, D_MODEL), lambda i: (cur(i), 0)),
            pl.BlockSpec((ts, D_MODEL), lambda i: (prv(i), 0)),
            const((1, D_MODEL)), const((D_MODEL, N_MAIN)), const((D_MODEL, LANES)), const((D_MODEL, LANES)),
            const((1, LANES)), const((1, LANES)), const((CONV_K, 3 * QK)),
            wspec((D_MODEL, D_EXPERT)), wspec((D_MODEL, D_EXPERT)), wspec((D_EXPERT, D_MODEL)),
            const((1, DH)),
            pl.BlockSpec((ts, DH), lambda i: (cur(i) % tiles_per_seq, 0)),
            pl.BlockSpec((ts, DH), lambda i: (cur(i) % tiles_per_seq, 0)),
            const((1, QK)),
            const((D_MODEL, D_MODEL)), const((1, D_MODEL)), const((D_MODEL, LANES)), const((D_MODEL, LANES)),
        ],
        out_specs=[
            pl.BlockSpec((ts * SUBLANES, LANES), lambda i: (prv(i), 0)),
            pl.BlockSpec((ts * SUBLANES, LANES), lambda i: (prv(i), 0)),
            pl.BlockSpec((TS_MOE // TM_MIX, SUBLANES, TM_MIX), lambda i: (prv(i) // per, 0, 0)),
            pl.BlockSpec((None, SUBLANES, LANES), lambda i: (prv(i) // per, 0, 0)),
            wspec((D_MODEL, D_EXPERT)), wspec((D_MODEL, D_EXPERT)), wspec((D_EXPERT, D_MODEL)),
        ],
        out_shape=[
            jax.ShapeDtypeStruct((t * SUBLANES, LANES), F32),
            jax.ShapeDtypeStruct((t * SUBLANES, LANES), F32),
            jax.ShapeDtypeStruct((t // TM_MIX, SUBLANES, TM_MIX), F32),
            jax.ShapeDtypeStruct((t // TS_MOE, SUBLANES, LANES), I32),
            wshape((D_MODEL, D_EXPERT)), wshape((D_MODEL, D_EXPERT)), wshape((D_EXPERT, D_MODEL)),
        ],
        scratch_shapes=[
            pltpu.VMEM((3, SUBLANES, QK), F32),
            pltpu.VMEM((HEADS, nc, 2, CHUNK), F32),
            pltpu.VMEM((HEADS, DH, DH), F32),
            pltpu.VMEM((HEADS, DH, DH), F32),
            pltpu.VMEM((ts, D_MODEL), BF16),
            pltpu.VMEM((1, LANES), F32),
            pltpu.VMEM((HEADS, CHUNK_RET, CHUNK_RET), F32),
            pltpu.VMEM((TM_MIX, TM_MIX), BF16),
        ],
        compiler_params=pltpu.CompilerParams(
            dimension_semantics=("arbitrary",), vmem_limit_bytes=VMEM_LIMIT_FRONT),
        name="front",
    )(x2, x2, attn_norm, w_main, ws_hi, ws_lo, alog_l, dtb_l, conv_a, w_gate, w_up, w_down,
      norm_a, cosf, sinf, norm_b, w_out16, ffn_norm, wr_hi, wr_lo)


def _moe_kernel(p0_ref, p1_ref, w0_ref, w1_ref, cnt_ref,
                h2t_hbm, x1t_hbm, wg_ref, wu_ref, wd_ref, fnw_ref,
                out_hbm, stag_ref, acc_ref, base_ref, ibuf_ref, obuf_ref, isem, osem):
    i = pl.program_id(0)
    j = pl.program_id(1)
    sub = SUB_MOE
    nsub = TS_MOE // sub
    npair = N_EXPERTS // EXPERTS_PER_STEP
    tok0 = i * TS_MOE

    def row8(r):
        return pl.ds(pl.multiple_of(r * SUBLANES, SUBLANES), SUBLANES)

    def at(r8):
        return pl.ds(pl.multiple_of(r8, SUBLANES), SUBLANES)

    def in_copy(src_hbm, k, slot):
        rows = pl.ds(pl.multiple_of((tok0 + k * sub) * SUBLANES, SUBLANES), sub * SUBLANES)
        return pltpu.make_async_copy(src_hbm.at[rows], ibuf_ref.at[slot], isem.at[slot])

    def out_copy(k, slot):
        rows = pl.ds(pl.multiple_of(tok0 + k * sub, SUBLANES), sub)
        return pltpu.make_async_copy(obuf_ref.at[slot], out_hbm.at[rows], osem.at[slot])

    @pl.when(j == 0)
    def _scatter():
        stag_ref[pl.ds(TOP_K * TS_MOE * SUBLANES, SLACK_MOE * SUBLANES), :] = jnp.zeros(
            (SLACK_MOE * SUBLANES, LANES), F32)

        def prefix(e, run):
            base_ref[e] = run
            return run + cnt_ref[e]

        lax.fori_loop(0, N_EXPERTS, prefix, jnp.int32(0))

        in_copy(h2t_hbm, 0, 0).start()
        for k in range(nsub):
            slot = k % 2
            if k + 1 < nsub:
                in_copy(h2t_hbm, k + 1, 1 - slot).start()
            in_copy(h2t_hbm, k, slot).wait()

            def body(t, carry, k=k, slot=slot):
                row = ibuf_ref[slot, row8(t), :]
                stag_ref[at(p0_ref[k * sub + t]), :] = row
                stag_ref[at(p1_ref[k * sub + t]), :] = row
                return carry

            lax.fori_loop(0, sub, body, 0, unroll=8)

    def load_rows(r0, m):
        xs = [stag_ref[pl.ds(r0 + s, m, stride=SUBLANES), :] for s in range(SUBLANES)]
        return xs, jnp.concatenate([v.astype(BF16) for v in xs], axis=1)

    def store_rows(r0, m, xs, y, nvalid):
        keep = lax.broadcasted_iota(I32, (m, LANES), 0) < nvalid
        for s in range(SUBLANES):
            stag_ref[pl.ds(r0 + s, m, stride=SUBLANES), :] = jnp.where(
                keep, y[:, s * LANES:(s + 1) * LANES], xs[s])

    qs = range(EXPERTS_PER_STEP)
    es = [j * EXPERTS_PER_STEP + q for q in qs]
    ns = [cnt_ref[e] for e in es]
    bs = [base_ref[e] for e in es]
    m = CH_FIRST_MOE
    r0s = [pl.multiple_of(b * SUBLANES, SUBLANES) for b in bs]
    ld = [load_rows(r0, m) for r0 in r0s]
    gs = [_dot(ld[q][1], wg_ref[q]) for q in qs]
    us = [_dot(ld[q][1], wu_ref[q]) for q in qs]
    hs = [((gs[q] * _sigmoid(gs[q])) * us[q]).astype(BF16) for q in qs]
    ys = [_dot(hs[q], wd_ref[q]) for q in qs]
    for q in qs:
        store_rows(r0s[q], m, ld[q][0], ys[q], ns[q])
    ch = CH_MOE
    for q in qs:
        def chunk(c, carry, q=q):
            r0 = pl.multiple_of((bs[q] + m + c * ch) * SUBLANES, SUBLANES)
            xs, x16 = load_rows(r0, ch)
            g = _dot(x16, wg_ref[q])
            u = _dot(x16, wu_ref[q])
            y = _dot(((g * _sigmoid(g)) * u).astype(BF16), wd_ref[q])
            store_rows(r0, ch, xs, y, ns[q] - m - c * ch)
            return carry

        lax.fori_loop(0, (jnp.maximum(ns[q] - m, 0) + ch - 1) // ch, chunk, 0)

    @pl.when(j == npair - 1)
    def _combine():
        in_copy(x1t_hbm, 0, 0).start()
        for k in range(nsub):
            slot = k % 2
            if k + 1 < nsub:
                in_copy(x1t_hbm, k + 1, 1 - slot).start()
            in_copy(x1t_hbm, k, slot).wait()

            def body(t, carry, k=k, slot=slot):
                acc_ref[row8(t), :] = (ibuf_ref[slot, row8(t), :]
                                       + w0_ref[k * sub + t] * stag_ref[at(p0_ref[k * sub + t]), :]
                                       + w1_ref[k * sub + t] * stag_ref[at(p1_ref[k * sub + t]), :])
                return carry

            lax.fori_loop(0, sub, body, 0, unroll=8)
            xo = jnp.concatenate([acc_ref[pl.ds(s, sub, stride=SUBLANES), :] for s in range(SUBLANES)], axis=1)
            if k >= 2:
                out_copy(k - 2, slot).wait()
            obuf_ref[slot] = xo * lax.rsqrt(jnp.mean(xo * xo, axis=-1, keepdims=True) + EPS) * fnw_ref[...]
            out_copy(k, slot).start()
        for k in (nsub - 2, nsub - 1):
            out_copy(k, k % 2).wait()


def _moe(p0, p1, w0, w1, cnt, h2t, x1t, wg16, wu16, wd16, final_norm):
    t = p0.shape[0]
    sub = SUB_MOE
    npair = N_EXPERTS // EXPERTS_PER_STEP
    nt = t // TS_MOE
    smem = functools.partial(pl.BlockSpec, memory_space=pltpu.SMEM)

    def wspec(shape):
        return pl.BlockSpec((EXPERTS_PER_STEP,) + shape, lambda i, j: (j, 0, 0))

    return pl.pallas_call(
        _moe_kernel,
        grid=(nt, npair),
        in_specs=[
            smem((TS_MOE,), lambda i, j: (i,)), smem((TS_MOE,), lambda i, j: (i,)),
            smem((TS_MOE,), lambda i, j: (i,)), smem((TS_MOE,), lambda i, j: (i,)),
            smem((LANES,), lambda i, j: (i,)),
            pl.BlockSpec(memory_space=pl.ANY),
            pl.BlockSpec(memory_space=pl.ANY),
            wspec((D_MODEL, D_EXPERT)), wspec((D_MODEL, D_EXPERT)), wspec((D_EXPERT, D_MODEL)),
            pl.BlockSpec((1, D_MODEL), lambda i, j: (0, 0)),
        ],
        out_specs=pl.BlockSpec(memory_space=pl.ANY),
        out_shape=jax.ShapeDtypeStruct((t, D_MODEL), F32),
        scratch_shapes=[
            pltpu.VMEM(((TOP_K * TS_MOE + SLACK_MOE) * SUBLANES, LANES), F32),
            pltpu.VMEM((sub * SUBLANES, LANES), F32),
            pltpu.SMEM((LANES,), I32),
            pltpu.VMEM((2, sub * SUBLANES, LANES), F32),
            pltpu.VMEM((2, sub, D_MODEL), F32),
            pltpu.SemaphoreType.DMA((2,)),
            pltpu.SemaphoreType.DMA((2,)),
        ],
        compiler_params=pltpu.CompilerParams(
            dimension_semantics=("arbitrary", "arbitrary"), vmem_limit_bytes=VMEM_LIMIT_MOE),
        name="moe",
    )(p0, p1, w0, w1, cnt, h2t, x1t, wg16, wu16, wd16, final_norm)


def _pad_lanes(a, start):
    rows, n = a.shape
    return jnp.pad(a, ((0, 0), (start, LANES - start - n)))


def kernel(x, attn_norm, w_in, conv_a, a_log, dt_bias, norm_a, norm_b, w_out, ffn_norm,
           w_router_group, w_router_expert, w_gate, w_up, w_down, final_norm):
    batch, seq, d = x.shape
    t = batch * seq
    x2 = x.reshape(t, d)
    assert attn_norm.shape[0] == 1, "the closing RMSNorm is fused into the single layer's MoE combine step"

    pos = jnp.arange(seq, dtype=F32)
    inv_freq = ROPE_BASE ** (-jnp.arange(0, DH, 2, dtype=F32) / DH)
    ang = pos[:, None] * inv_freq[None, :]
    cos, sin = jnp.cos(ang), jnp.sin(ang)
    cosf = jnp.concatenate([cos, cos], axis=-1)
    sinf = jnp.concatenate([-sin, sin], axis=-1)

    wl = w_in[0]
    w_main = jnp.concatenate([wl[:, 0:4 * QK], wl[:, 4 * QK + 2 * HEADS:]], axis=1).astype(BF16)
    ws_hi, ws_lo = _split_bf16(_pad_lanes(wl[:, 4 * QK:4 * QK + 2 * HEADS], 0))
    alog_l = _pad_lanes(a_log[0][None, :], HEADS)
    dtb_l = _pad_lanes(dt_bias[0][None, :], HEADS)
    wr = jnp.concatenate(
        [_pad_lanes(w_router_group[0], 0)[:, 0:N_EXPERTS], w_router_expert[0]], axis=1)
    wr_hi, wr_lo = _split_bf16(wr)
    x1t, h2t, rr, cnt, wg16, wu16, wd16 = _front(
        x2, attn_norm[0][None, :], w_main, ws_hi, ws_lo, alog_l, dtb_l, conv_a[0], w_gate[0], w_up[0], w_down[0],
        norm_a[0][None, :], cosf, sinf, norm_b[0][None, :], w_out[0].astype(BF16), ffn_norm[0][None, :],
        wr_hi, wr_lo, seq)

    p0 = rr[:, 0, :].reshape(-1).astype(I32)
    p1 = rr[:, 1, :].reshape(-1).astype(I32)
    out = _moe(p0, p1, rr[:, 2, :].reshape(-1), rr[:, 3, :].reshape(-1), cnt[:, 0, :].reshape(-1), h2t, x1t,
               wg16, wu16, wd16, final_norm[None, :])
    return out.reshape(batch, seq, d)
```

```python
import functools
import math

import jax
import jax.numpy as jnp
from jax import lax
from jax.experimental import pallas as pl
from jax.experimental.pallas import tpu as pltpu

F32 = jnp.float32
BF16 = jnp.bfloat16
I32 = jnp.int32

D_MODEL = 1024
HEADS = 4
DH = 128
CONV_K = 4
CHUNK = 64
ROPE_BASE = 10000.0
N_GROUPS = 8
EPG = 8
N_EXPERTS = N_GROUPS * EPG
TOP_K = 2
D_EXPERT = 256
EPS = 1e-6
QK = HEADS * DH
N_MAIN = 8 * QK
LANES = 128
SUBLANES = 8

TS_FRONT = 512
CHUNK_RET = 128
TM_MIX = 256
TS_MOE = 4096
SUB_MOE = 256
CH_FIRST_MOE = 160
CH_MOE = 128
SLACK_MOE = CH_FIRST_MOE + CH_MOE
EXPERTS_PER_STEP = 4
ROW_UNROLL_MOE = 32
RANK_BITS = 16
RANK_RADIX = 1 << RANK_BITS
VMEM_LIMIT_FRONT = 58 * 1024 * 1024
VMEM_LIMIT_MOE = 58 * 1024 * 1024
_LOG_GAMMA = tuple(math.log(1.0 - 2.0 ** (-5.0 - hh)) for hh in range(HEADS))


def _sigmoid(x):
    return 1.0 / (1.0 + jnp.exp(-x))


def _softplus(x):
    return jnp.maximum(x, 0.0) + jnp.log1p(jnp.exp(-jnp.abs(x)))


def _split_bf16(a):
    hi = a.astype(BF16)
    lo = (a - hi.astype(F32)).astype(BF16)
    return hi, lo


def _dot(a, b):
    return jnp.dot(a, b, preferred_element_type=F32)


def _dot_nt(a, b):
    return lax.dot_general(a, b, (((1,), (1,)), ((), ())), preferred_element_type=F32)


def _dot_tn(a, b):
    return lax.dot_general(a, b, (((0,), (0,)), ((), ())), preferred_element_type=F32)


def _bmm(a, b):
    return jnp.stack([_dot(a[c], b[c]) for c in range(a.shape[0])])


def _bmm_nt(a, b):
    return jnp.stack([_dot_nt(a[c], b[c]) for c in range(a.shape[0])])


def _project(x, nw_ref, wm_ref, wsh_ref, wsl_ref, alog_ref, dtb_ref, cw_ref, tail_ref, grow_ref):
    tm = x.shape[0]
    h = x * lax.rsqrt(jnp.mean(x * x, axis=-1, keepdims=True) + EPS) * nw_ref[...]
    hb = h.astype(BF16)
    blocks = [None] * (N_MAIN // QK)
    for j in (0, 3, 1, 4, 2, 5, 6, 7):
        blk = _dot(hb, wm_ref[:, j * QK:(j + 1) * QK])
        if j < 3:
            ext = jnp.concatenate([tail_ref[j], blk], axis=0)
            tail_ref[j] = blk[tm - SUBLANES:tm, :]
            w = cw_ref[:, j * QK:(j + 1) * QK]
            y = w[CONV_K - 1:CONV_K, :] * blk
            for jj in range(CONV_K - 1):
                y = y + w[jj:jj + 1, :] * pltpu.roll(ext, CONV_K - 1 - jj, axis=0)[SUBLANES:, :]
            blk = y * _sigmoid(y)
        if j < 2:
            scale = DH ** -0.5 if j == 0 else 1.0
            blk = jnp.concatenate(
                [blk[:, hh * DH:(hh + 1) * DH]
                 * (lax.rsqrt(jnp.sum(blk[:, hh * DH:(hh + 1) * DH] ** 2, axis=-1, keepdims=True) + EPS) * scale)
                 for hh in range(HEADS)], axis=1)
        blocks[j] = blk
    sm = _dot(hb, wsh_ref[...]) + _dot(hb, wsl_ref[...])
    lane = lax.broadcasted_iota(I32, (tm, LANES), 1)
    row = lax.broadcasted_iota(I32, (tm, LANES), 0)
    beta = _sigmoid(sm)
    g = -jnp.exp(alog_ref[...]) * _softplus(sm + dtb_ref[...])
    rin = row & (CHUNK - 1)
    gc = g
    d = 1
    while d < CHUNK:
        gc = gc + jnp.where(rin >= d, pltpu.roll(gc, d, axis=0), 0.0)
        d *= 2
    out = jnp.where(lane < HEADS, beta, jnp.where(lane < 2 * HEADS, gc, 0.0))
    tr = out.T
    for hh in range(HEADS):
        for c in range(tm // CHUNK):
            grow_ref[hh, c, 0:1, :] = tr[hh:hh + 1, c * CHUNK:(c + 1) * CHUNK]
            grow_ref[hh, c, 1:2, :] = tr[HEADS + hh:HEADS + hh + 1, c * CHUNK:(c + 1) * CHUNK]
    return blocks, out


def _gated_norm(o, gate, nw):
    outs = []
    for h in range(HEADS):
        sl = slice(h * DH, (h + 1) * DH)
        oh = o[:, sl]
        wh = nw if nw.shape[1] == DH else nw[:, sl]
        oh = oh * lax.rsqrt(jnp.mean(oh * oh, axis=-1, keepdims=True) + EPS) * wh
        outs.append(oh * (gate[:, sl] * _sigmoid(gate[:, sl])))
    return jnp.concatenate(outs, axis=1)


def _gdn_heads(q, k, v, gblk, grow, st_ref):
    ts = q.shape[0]
    ncl = ts // CHUNK
    nc = HEADS * ncl

    def per_head(f):
        return jnp.concatenate([f(h, slice(h * DH, (h + 1) * DH)) for h in range(HEADS)], axis=0)

    lane = lax.broadcasted_iota(I32, (ts, LANES), 1)
    q3 = per_head(lambda h, sl: q[:, sl].reshape(ncl, CHUNK, DH))
    k3 = per_head(lambda h, sl: k[:, sl].reshape(ncl, CHUNK, DH))
    v3 = per_head(lambda h, sl: v[:, sl].reshape(ncl, CHUNK, DH))
    bc = per_head(lambda h, sl: jnp.sum(jnp.where(lane == h, gblk, 0.0), axis=-1, keepdims=True
                                        ).reshape(ncl, CHUNK, 1))
    gcc = per_head(lambda h, sl: jnp.sum(jnp.where(lane == h + HEADS, gblk, 0.0), axis=-1, keepdims=True
                                         ).reshape(ncl, CHUNK, 1))
    gr = grow.reshape(nc, 2, CHUNK)
    beta_r = gr[:, 0:1, :]
    gc_r = gr[:, 1:2, :]
    gl = gcc[:, CHUNK - 1:CHUNK, :]

    ii = lax.broadcasted_iota(I32, (nc, CHUNK, CHUNK), 1)
    jj = lax.broadcasted_iota(I32, (nc, CHUNK, CHUNK), 2)
    causal = ii >= jj
    strict = ii > jj
    dec = jnp.where(causal, jnp.exp(jnp.where(causal, gcc - gc_r, 0.0)), 0.0)

    k16 = k3.astype(BF16)
    q16 = q3.astype(BF16)
    v16 = v3.astype(BF16)
    gram = _bmm_nt(k16, k16)
    qk = _bmm_nt(q16, k16)
    a = jnp.where(strict, gram * dec * bc, 0.0)

    eye = (ii == jj).astype(F32)
    dm = eye - jnp.where((ii >> 1) == (jj >> 1), a, 0.0)
    sblk = 2
    while sblk < CHUNK:
        sh = sblk.bit_length() - 1
        off = jnp.where(((ii >> (sh + 1)) == (jj >> (sh + 1))) & ((ii >> sh) != (jj >> sh)), a, 0.0)
        d16 = dm.astype(BF16)
        t1 = _bmm(d16, off.astype(BF16))
        dm = dm - _bmm(t1.astype(BF16), d16)
        sblk *= 2

    tb = dm * beta_r
    tbe = tb * jnp.exp(gc_r)
    u = _bmm(tb.astype(BF16), v16)
    w = _bmm(tbe.astype(BF16), k16)
    attn = qk * dec
    wu = jnp.concatenate([w, u], axis=-1)
    wu16 = wu.astype(BF16)
    aw = _bmm(attn.astype(BF16), wu16)
    qe = q3 * jnp.exp(gcc) - aw[:, :, 0:DH]
    ol = aw[:, :, DH:2 * DH]
    kdec = k3 * jnp.exp(gl - gcc)

    trow = lax.broadcasted_iota(I32, (2 * CHUNK, DH), 0)
    xn = []
    for p in range(nc // 2):
        kd2 = kdec[2 * p:2 * p + 2].reshape(2 * CHUNK, DH)
        wu2 = wu16[2 * p:2 * p + 2].reshape(2 * CHUNK, 2 * DH)
        xn.append(_dot_tn(jnp.where(trow < CHUNK, kd2, 0.0).astype(BF16), wu2))
        xn.append(_dot_tn(jnp.where(trow >= CHUNK, kd2, 0.0).astype(BF16), wu2))

    sts = [st_ref[h] for h in range(HEADS)]
    outs = [[] for _ in range(HEADS)]
    for c in range(ncl):
        for h in range(HEADS):
            i = h * ncl + c
            s16 = sts[h].astype(BF16)
            outs[h].append(_dot(qe[i].astype(BF16), s16) + ol[i])
            sts[h] = jnp.exp(gl[i]) * sts[h] + (xn[i][:, DH:2 * DH] - _dot(xn[i][:, 0:DH].astype(BF16), s16))
    for h in range(HEADS):
        st_ref[h] = sts[h]
    return jnp.concatenate([jnp.concatenate(outs[h], axis=0) for h in range(HEADS)], axis=1)


def _ret_decay():
    c = CHUNK_RET
    ii = lax.broadcasted_iota(I32, (c, c), 0)
    jj = lax.broadcasted_iota(I32, (c, c), 1)
    rel = (ii - jj).astype(F32)
    return jnp.stack([jnp.where(rel >= 0, jnp.exp(jnp.maximum(rel, 0.0) * _LOG_GAMMA[h]), 0.0)
                      for h in range(HEADS)])


def _ret_heads(qa, ka, va, cosf, sinf, dec_ref, st_ref):
    rows = qa.shape[0]
    c = CHUNK_RET
    pos = lax.broadcasted_iota(I32, (c, 1), 0).astype(F32)
    decs = [dec_ref[h] for h in range(HEADS)]
    sts = [st_ref[h] for h in range(HEADS)]
    outs = [[] for _ in range(HEADS)]
    for ck in range(rows // c):
        rs = slice(ck * c, (ck + 1) * c)
        for h in range(HEADS):
            sl = slice(h * DH, (h + 1) * DH)
            lg = _LOG_GAMMA[h]
            q = qa[rs, sl] * cosf[rs] + pltpu.roll(qa[rs, sl], DH // 2, axis=1) * sinf[rs]
            k = (ka[rs, sl] * cosf[rs] + pltpu.roll(ka[rs, sl], DH // 2, axis=1) * sinf[rs]) * (DH ** -0.5)
            v16 = va[rs, sl]
            k16 = k.astype(BF16)
            attn = _dot_nt(q.astype(BF16), k16) * decs[h]
            inner = _dot(attn.astype(BF16), v16)
            qd = q * jnp.exp((pos + 1.0) * lg)
            cross = _dot(qd.astype(BF16), sts[h].astype(BF16))
            kd = k * jnp.exp((c - 1.0 - pos) * lg)
            sts[h] = sts[h] * math.exp(c * lg) + _dot_tn(kd.astype(BF16), v16)
            outs[h].append(cross + inner)
    for h in range(HEADS):
        st_ref[h] = sts[h]
    return jnp.concatenate([jnp.concatenate(outs[h], axis=0) for h in range(HEADS)], axis=1)


def _route(x, o16, wo_ref, nw, wrh_ref, wrl_ref, lstrict, carry):
    tm = x.shape[0]
    x1 = x + _dot(o16, wo_ref[...])
    h2 = x1 * lax.rsqrt(jnp.mean(x1 * x1, axis=-1, keepdims=True) + EPS) * nw
    hh = h2.astype(BF16)
    lg = _dot(hh, wrh_ref[...]) + _dot(hh, wrl_ref[...])
    lane = lax.broadcasted_iota(I32, (tm, LANES), 1)
    big = jnp.int32(1 << 20)
    neg = jnp.float32(-jnp.inf)

    def argmax_first(vals, mask):
        mv = jnp.where(mask, vals, neg)
        m = jnp.max(mv, axis=-1, keepdims=True)
        idx = jnp.min(jnp.where(mask & (mv == m), lane, big), axis=-1, keepdims=True)
        return m, idx

    gmask = lane < N_GROUPS
    gm, gi = argmax_first(lg, gmask)
    gw = 1.0 / jnp.sum(jnp.where(gmask, jnp.exp(lg - gm), 0.0), axis=-1, keepdims=True)
    emask = (lane >= N_EXPERTS) & ((lane >> 3) == gi + N_EXPERTS // EPG)
    m1, i1 = argmax_first(lg, emask)
    m2, i2 = argmax_first(lg, emask & (lane != i1))
    e21 = jnp.exp(m2 - m1)
    w0 = gw / (1.0 + e21)
    w1 = gw * e21 / (1.0 + e21)
    e0 = i1 - N_EXPERTS
    e1 = i2 - N_EXPERTS

    oh = ((lane == e0) | (lane == e1)).astype(BF16)
    pref = _dot(lstrict, oh) + carry
    r0 = jnp.sum(jnp.where(lane == e0, pref, 0.0), axis=-1, keepdims=True)
    r1 = jnp.sum(jnp.where(lane == e1, pref, 0.0), axis=-1, keepdims=True)
    carry = carry + jnp.sum(oh.astype(F32), axis=0, keepdims=True)

    rmat = jnp.where(lane == 0, e0.astype(F32) * RANK_RADIX + r0,
                     jnp.where(lane == 1, e1.astype(F32) * RANK_RADIX + r1,
                               jnp.where(lane == 2, w0, jnp.where(lane == 3, w1, 0.0))))
    return x1, h2, rmat.T[0:SUBLANES, :], carry


def _finish_slots(rr_ref, carry):
    lane8 = lax.broadcasted_iota(I32, (SUBLANES, LANES), 1)
    cnt8 = jnp.broadcast_to(carry, (SUBLANES, LANES))
    inc = cnt8
    d = 1
    while d < LANES:
        inc = inc + jnp.where(lane8 >= d, pltpu.roll(inc, d, axis=1), 0.0)
        d *= 2
    base = inc - cnt8
    pk = rr_ref[:, 0:TOP_K, :]
    ex = jnp.floor(pk * (1.0 / RANK_RADIX))
    slot = pk - ex * RANK_RADIX
    for kk in range(N_EXPERTS):
        bk = jnp.sum(jnp.where(lane8 == kk, base, 0.0), axis=-1, keepdims=True)[0:1, :]
        slot = slot + jnp.where(ex == kk, bk, 0.0)
    rr_ref[:, 0:TOP_K, :] = slot * SUBLANES


def _front_kernel(xc_ref, x_ref, anw_ref, wm_ref, wsh_ref, wsl_ref, alog_ref, dtb_ref, cw_ref,
                  wg_ref, wu_ref, wd_ref, na_ref, cos_ref, sin_ref, nb_ref, wo_ref, fw_ref, wrh_ref, wrl_ref,
                  x1t_ref, h2t_ref, rr_ref, cnt_ref, wg16_ref, wu16_ref, wd16_ref,
                  tail_ref, grow_ref, sta_ref, stb_ref, oprev_ref, carry_ref, dec_ref, ls_ref,
                  *, tiles_per_seq, per):
    i = pl.program_id(0)
    ts = x_ref.shape[0]
    tm = TM_MIX
    prev = i + per - 1
    wg16_ref[...] = wg_ref[...].astype(BF16)
    wu16_ref[...] = wu_ref[...].astype(BF16)
    wd16_ref[...] = wd_ref[...].astype(BF16)

    @pl.when(i == 0)
    def _first():
        oprev_ref[...] = jnp.zeros(oprev_ref.shape, oprev_ref.dtype)
        dec_ref[...] = _ret_decay()
        ri = lax.broadcasted_iota(I32, (tm, tm), 0)
        ci = lax.broadcasted_iota(I32, (tm, tm), 1)
        ls_ref[...] = (ri > ci).astype(BF16)

    @pl.when(i % tiles_per_seq == 0)
    def _seq_start():
        tail_ref[...] = jnp.zeros(tail_ref.shape, F32)
        sta_ref[...] = jnp.zeros(sta_ref.shape, F32)
        stb_ref[...] = jnp.zeros(stb_ref.shape, F32)

    @pl.when((i == 0) | (prev % per == 0))
    def _moe_tile_start():
        carry_ref[...] = jnp.zeros((1, LANES), F32)

    (qa, ka, va, ga, qb, kb, vb, gb), gcol = _project(
        xc_ref[...], anw_ref, wm_ref, wsh_ref, wsl_ref, alog_ref, dtb_ref, cw_ref, tail_ref, grow_ref)

    carry = carry_ref[...]
    nw = fw_ref[...]
    for hf in range(ts // tm):
        rows = slice(hf * tm, (hf + 1) * tm)
        x1, h2, rmt, carry = _route(x_ref[rows, :], oprev_ref[rows, :], wo_ref, nw, wrh_ref, wrl_ref,
                                    ls_ref[...], carry)
        for s in range(SUBLANES):
            x1t_ref[pl.ds(hf * tm * SUBLANES + s, tm, stride=SUBLANES), :] = x1[:, s * LANES:(s + 1) * LANES]
            h2t_ref[pl.ds(hf * tm * SUBLANES + s, tm, stride=SUBLANES), :] = h2[:, s * LANES:(s + 1) * LANES]
        rr_ref[(prev % per) * (ts // tm) + hf] = rmt
    carry_ref[...] = carry
    cnt_ref[...] = jnp.broadcast_to(carry, (SUBLANES, LANES)).astype(I32)

    oa = _gdn_heads(qa, ka, va, gcol, grow_ref[...], sta_ref)
    ob = _ret_heads(qb, kb, vb.astype(BF16), cos_ref[...], sin_ref[...], dec_ref, stb_ref)
    oa = _gated_norm(oa, ga, na_ref[...])
    ob = _gated_norm(ob, gb, nb_ref[...])
    oprev_ref[...] = jnp.concatenate([oa, ob], axis=1).astype(oprev_ref.dtype)

    @pl.when((i > 0) & (prev % per == per - 1))
    def _moe_tile_end():
        _finish_slots(rr_ref, carry)


def _front(x2, attn_norm, w_main, ws_hi, ws_lo, alog_l, dtb_l, conv_a, w_gate, w_up, w_down,
           norm_a, cosf, sinf, norm_b, w_out16, ffn_norm, wr_hi, wr_lo, seq):
    t = x2.shape[0]
    ts = TS_FRONT
    n = t // ts
    nc = ts // CHUNK
    tiles_per_seq = seq // ts
    per = TS_MOE // ts
    assert N_EXPERTS % n == 0, "each tile step casts an equal share of the expert weights"
    epb = N_EXPERTS // n

    def cur(i):
        return jnp.minimum(i, n - 1)

    def prv(i):
        return jnp.maximum(i - 1, 0)

    def const(shape):
        return pl.BlockSpec(shape, lambda i: (0,) * len(shape))

    def wspec(shape):
        return pl.BlockSpec((epb,) + shape, lambda i: (cur(i), 0, 0))

    def wshape(shape):
        return jax.ShapeDtypeStruct((N_EXPERTS,) + shape, BF16)

    return pl.pallas_call(
        functools.partial(_front_kernel, tiles_per_seq=tiles_per_seq, per=per),
        grid=(n + 1,),
        in_specs=[
            pl.BlockSpec((ts, D_MODEL), lambda i: (cur(i), 0)),
            pl.BlockSpec((ts, D_MODEL), lambda i: (prv(i), 0)),
            const((1, D_MODEL)), const((D_MODEL, N_MAIN)), const((D_MODEL, LANES)), const((D_MODEL, LANES)),
            const((1, LANES)), const((1, LANES)), const((CONV_K, 3 * QK)),
            wspec((D_MODEL, D_EXPERT)), wspec((D_MODEL, D_EXPERT)), wspec((D_EXPERT, D_MODEL)),
            const((1, DH)),
            pl.BlockSpec((ts, DH), lambda i: (cur(i) % tiles_per_seq, 0)),
            pl.BlockSpec((ts, DH), lambda i: (cur(i) % tiles_per_seq, 0)),
            const((1, QK)),
            const((D_MODEL, D_MODEL)), const((1, D_MODEL)), const((D_MODEL, LANES)), const((D_MODEL, LANES)),
        ],
        out_specs=[
            pl.BlockSpec((ts * SUBLANES, LANES), lambda i: (prv(i), 0)),
            pl.BlockSpec((ts * SUBLANES, LANES), lambda i: (prv(i), 0)),
            pl.BlockSpec((TS_MOE // TM_MIX, SUBLANES, TM_MIX), lambda i: (prv(i) // per, 0, 0)),
            pl.BlockSpec((None, SUBLANES, LANES), lambda i: (prv(i) // per, 0, 0)),
            wspec((D_MODEL, D_EXPERT)), wspec((D_MODEL, D_EXPERT)), wspec((D_EXPERT, D_MODEL)),
        ],
        out_shape=[
            jax.ShapeDtypeStruct((t * SUBLANES, LANES), F32),
            jax.ShapeDtypeStruct((t * SUBLANES, LANES), F32),
            jax.ShapeDtypeStruct((t // TM_MIX, SUBLANES, TM_MIX), F32),
            jax.ShapeDtypeStruct((t // TS_MOE, SUBLANES, LANES), I32),
            wshape((D_MODEL, D_EXPERT)), wshape((D_MODEL, D_EXPERT)), wshape((D_EXPERT, D_MODEL)),
        ],
        scratch_shapes=[
            pltpu.VMEM((3, SUBLANES, QK), F32),
            pltpu.VMEM((HEADS, nc, 2, CHUNK), F32),
            pltpu.VMEM((HEADS, DH, DH), F32),
            pltpu.VMEM((HEADS, DH, DH), F32),
            pltpu.VMEM((ts, D_MODEL), BF16),
            pltpu.VMEM((1, LANES), F32),
            pltpu.VMEM((HEADS, CHUNK_RET, CHUNK_RET), F32),
            pltpu.VMEM((TM_MIX, TM_MIX), BF16),
        ],
        compiler_params=pltpu.CompilerParams(
            dimension_semantics=("arbitrary",), vmem_limit_bytes=VMEM_LIMIT_FRONT),
        name="front",
    )(x2, x2, attn_norm, w_main, ws_hi, ws_lo, alog_l, dtb_l, conv_a, w_gate, w_up, w_down,
      norm_a, cosf, sinf, norm_b, w_out16, ffn_norm, wr_hi, wr_lo)


def _moe_kernel(p0_ref, p1_ref, w0_ref, w1_ref, cnt_ref,
                h2t_hbm, x1t_hbm, wg_ref, wu_ref, wd_ref, fnw_ref,
                out_hbm, stag_ref, acc_ref, base_ref, ibuf_ref, obuf_ref, isem, osem):
    i = pl.program_id(0)
    j = pl.program_id(1)
    sub = SUB_MOE
    nsub = TS_MOE // sub
    npair = N_EXPERTS // EXPERTS_PER_STEP
    tok0 = i * TS_MOE

    def row8(r):
        return pl.ds(pl.multiple_of(r * SUBLANES, SUBLANES), SUBLANES)

    def at(r8):
        return pl.ds(pl.multiple_of(r8, SUBLANES), SUBLANES)

    def in_copy(src_hbm, k, slot):
        rows = pl.ds(pl.multiple_of((tok0 + k * sub) * SUBLANES, SUBLANES), sub * SUBLANES)
        return pltpu.make_async_copy(src_hbm.at[rows], ibuf_ref.at[slot], isem.at[slot])

    def out_copy(k, slot):
        rows = pl.ds(pl.multiple_of(tok0 + k * sub, SUBLANES), sub)
        return pltpu.make_async_copy(obuf_ref.at[slot], out_hbm.at[rows], osem.at[slot])

    @pl.when(j == 0)
    def _scatter():
        stag_ref[pl.ds(TOP_K * TS_MOE * SUBLANES, SLACK_MOE * SUBLANES), :] = jnp.zeros(
            (SLACK_MOE * SUBLANES, LANES), F32)

        def prefix(e, run):
            base_ref[e] = run
            return run + cnt_ref[e]

        lax.fori_loop(0, N_EXPERTS, prefix, jnp.int32(0))

        in_copy(h2t_hbm, 0, 0).start()
        for k in range(nsub):
            slot = k % 2
            if k + 1 < nsub:
                in_copy(h2t_hbm, k + 1, 1 - slot).start()
            in_copy(h2t_hbm, k, slot).wait()

            def body(t, carry, k=k, slot=slot):
                row = ibuf_ref[slot, row8(t), :]
                stag_ref[at(p0_ref[k * sub + t]), :] = row
                stag_ref[at(p1_ref[k * sub + t]), :] = row
                return carry

            lax.fori_loop(0, sub, body, 0, unroll=ROW_UNROLL_MOE)

    def load_rows(r0, m):
        xs = [stag_ref[pl.ds(r0 + s, m, stride=SUBLANES), :] for s in range(SUBLANES)]
        return xs, jnp.concatenate([v.astype(BF16) for v in xs], axis=1)

    def store_rows(r0, m, xs, y, nvalid):
        keep = lax.broadcasted_iota(I32, (m, LANES), 0) < nvalid
        for s in range(SUBLANES):
            stag_ref[pl.ds(r0 + s, m, stride=SUBLANES), :] = jnp.where(
                keep, y[:, s * LANES:(s + 1) * LANES], xs[s])

    qs = range(EXPERTS_PER_STEP)
    es = [j * EXPERTS_PER_STEP + q for q in qs]
    ns = [cnt_ref[e] for e in es]
    bs = [base_ref[e] for e in es]
    m = CH_FIRST_MOE
    r0s = [pl.multiple_of(b * SUBLANES, SUBLANES) for b in bs]
    ld = [load_rows(r0, m) for r0 in r0s]
    gs = [_dot(ld[q][1], wg_ref[q]) for q in qs]
    us = [_dot(ld[q][1], wu_ref[q]) for q in qs]
    hs = [((gs[q] * _sigmoid(gs[q])) * us[q]).astype(BF16) for q in qs]
    ys = [_dot(hs[q], wd_ref[q]) for q in qs]
    for q in qs:
        store_rows(r0s[q], m, ld[q][0], ys[q], ns[q])
    ch = CH_MOE
    for q in qs:
        def chunk(c, carry, q=q):
            r0 = pl.multiple_of((bs[q] + m + c * ch) * SUBLANES, SUBLANES)
            xs, x16 = load_rows(r0, ch)
            g = _dot(x16, wg_ref[q])
            u = _dot(x16, wu_ref[q])
            y = _dot(((g * _sigmoid(g)) * u).astype(BF16), wd_ref[q])
            store_rows(r0, ch, xs, y, ns[q] - m - c * ch)
            return carry

        lax.fori_loop(0, (jnp.maximum(ns[q] - m, 0) + ch - 1) // ch, chunk, 0)

    @pl.when(j == npair - 1)
    def _combine():
        in_copy(x1t_hbm, 0, 0).start()
        for k in range(nsub):
            slot = k % 2
            if k + 1 < nsub:
                in_copy(x1t_hbm, k + 1, 1 - slot).start()
            in_copy(x1t_hbm, k, slot).wait()

            def body(t, carry, k=k, slot=slot):
                acc_ref[row8(t), :] = (ibuf_ref[slot, row8(t), :]
                                       + w0_ref[k * sub + t] * stag_ref[at(p0_ref[k * sub + t]), :]
                                       + w1_ref[k * sub + t] * stag_ref[at(p1_ref[k * sub + t]), :])
                return carry

            lax.fori_loop(0, sub, body, 0, unroll=ROW_UNROLL_MOE)
            xo = jnp.concatenate([acc_ref[pl.ds(s, sub, stride=SUBLANES), :] for s in range(SUBLANES)], axis=1)
            if k >= 2:
                out_copy(k - 2, slot).wait()
            obuf_ref[slot] = xo * lax.rsqrt(jnp.mean(xo * xo, axis=-1, keepdims=True) + EPS) * fnw_ref[...]
            out_copy(k, slot).start()
        for k in (nsub - 2, nsub - 1):
            out_copy(k, k % 2).wait()


def _moe(p0, p1, w0, w1, cnt, h2t, x1t, wg16, wu16, wd16, final_norm):
    t = p0.shape[0]
    sub = SUB_MOE
    npair = N_EXPERTS // EXPERTS_PER_STEP
    nt = t // TS_MOE
    smem = functools.partial(pl.BlockSpec, memory_space=pltpu.SMEM)

    def wspec(shape):
        return pl.BlockSpec((EXPERTS_PER_STEP,) + shape, lambda i, j: (j, 0, 0))

    return pl.pallas_call(
        _moe_kernel,
        grid=(nt, npair),
        in_specs=[
            smem((TS_MOE,), lambda i, j: (i,)), smem((TS_MOE,), lambda i, j: (i,)),
            smem((TS_MOE,), lambda i, j: (i,)), smem((TS_MOE,), lambda i, j: (i,)),
            smem((LANES,), lambda i, j: (i,)),
            pl.BlockSpec(memory_space=pl.ANY),
            pl.BlockSpec(memory_space=pl.ANY),
            wspec((D_MODEL, D_EXPERT)), wspec((D_MODEL, D_EXPERT)), wspec((D_EXPERT, D_MODEL)),
            pl.BlockSpec((1, D_MODEL), lambda i, j: (0, 0)),
        ],
        out_specs=pl.BlockSpec(memory_space=pl.ANY),
        out_shape=jax.ShapeDtypeStruct((t, D_MODEL), F32),
        scratch_shapes=[
            pltpu.VMEM(((TOP_K * TS_MOE + SLACK_MOE) * SUBLANES, LANES), F32),
            pltpu.VMEM((sub * SUBLANES, LANES), F32),
            pltpu.SMEM((LANES,), I32),
            pltpu.VMEM((2, sub * SUBLANES, LANES), F32),
            pltpu.VMEM((2, sub, D_MODEL), F32),
            pltpu.SemaphoreType.DMA((2,)),
            pltpu.SemaphoreType.DMA((2,)),
        ],
        compiler_params=pltpu.CompilerParams(
            dimension_semantics=("arbitrary", "arbitrary"), vmem_limit_bytes=VMEM_LIMIT_MOE),
        name="moe",
    )(p0, p1, w0, w1, cnt, h2t, x1t, wg16, wu16, wd16, final_norm)


def _pad_lanes(a, start):
    rows, n = a.shape
    return jnp.pad(a, ((0, 0), (start, LANES - start - n)))


def kernel(x, attn_norm, w_in, conv_a, a_log, dt_bias, norm_a, norm_b, w_out, ffn_norm,
           w_router_group, w_router_expert, w_gate, w_up, w_down, final_norm):
    batch, seq, d = x.shape
    t = batch * seq
    x2 = x.reshape(t, d)
    assert attn_norm.shape[0] == 1, "the closing RMSNorm is fused into the single layer's MoE combine step"

    pos = jnp.arange(seq, dtype=F32)
    inv_freq = ROPE_BASE ** (-jnp.arange(0, DH, 2, dtype=F32) / DH)
    ang = pos[:, None] * inv_freq[None, :]
    cos, sin = jnp.cos(ang), jnp.sin(ang)
    cosf = jnp.concatenate([cos, cos], axis=-1)
    sinf = jnp.concatenate([-sin, sin], axis=-1)

    wl = w_in[0]
    w_main = jnp.concatenate([wl[:, 0:4 * QK], wl[:, 4 * QK + 2 * HEADS:]], axis=1).astype(BF16)
    ws_hi, ws_lo = _split_bf16(_pad_lanes(wl[:, 4 * QK:4 * QK + 2 * HEADS], 0))
    alog_l = _pad_lanes(a_log[0][None, :], HEADS)
    dtb_l = _pad_lanes(dt_bias[0][None, :], HEADS)
    wr = jnp.concatenate(
        [_pad_lanes(w_router_group[0], 0)[:, 0:N_EXPERTS], w_router_expert[0]], axis=1)
    wr_hi, wr_lo = _split_bf16(wr)
    x1t, h2t, rr, cnt, wg16, wu16, wd16 = _front(
        x2, attn_norm[0][None, :], w_main, ws_hi, ws_lo, alog_l, dtb_l, conv_a[0], w_gate[0], w_up[0], w_down[0],
        norm_a[0][None, :], cosf, sinf, norm_b[0][None, :], w_out[0].astype(BF16), ffn_norm[0][None, :],
        wr_hi, wr_lo, seq)

    p0 = rr[:, 0, :].reshape(-1).astype(I32)
    p1 = rr[:, 1, :].reshape(-1).astype(I32)
    out = _moe(p0, p1, rr[:, 2, :].reshape(-1), rr[:, 3, :].reshape(-1), cnt[:, 0, :].reshape(-1), h2t, x1t,
               wg16, wu16, wd16, final_norm[None, :])
    return out.reshape(batch, seq, d)
```

```python
import functools
import math

import jax
import jax.numpy as jnp
from jax import lax
from jax.experimental import pallas as pl
from jax.experimental.pallas import tpu as pltpu

F32 = jnp.float32
BF16 = jnp.bfloat16
I32 = jnp.int32

D_MODEL = 1024
HEADS = 4
DH = 128
CONV_K = 4
CHUNK = 64
ROPE_BASE = 10000.0
N_GROUPS = 8
EPG = 8
N_EXPERTS = N_GROUPS * EPG
TOP_K = 2
D_EXPERT = 256
EPS = 1e-6
QK = HEADS * DH
N_MAIN = 8 * QK
LANES = 128
SUBLANES = 8

TS_FRONT = 512
CHUNK_RET = 128
TM_MIX = 256
TS_MOE = 4096
SUB_MOE = 256
CH_FIRST_MOE = 160
CH_MOE = 128
SLACK_MOE = CH_FIRST_MOE + CH_MOE
EXPERTS_PER_STEP = 4
ROW_UNROLL_MOE = 32
RANK_BITS = 16
RANK_RADIX = 1 << RANK_BITS
VMEM_LIMIT_FRONT = 58 * 1024 * 1024
VMEM_LIMIT_MOE = 58 * 1024 * 1024
_LOG_GAMMA = tuple(math.log(1.0 - 2.0 ** (-5.0 - hh)) for hh in range(HEADS))


def _sigmoid(x):
    return 1.0 / (1.0 + jnp.exp(-x))


def _softplus(x):
    return jnp.maximum(x, 0.0) + jnp.log1p(jnp.exp(-jnp.abs(x)))


def _split_bf16(a):
    hi = a.astype(BF16)
    lo = (a - hi.astype(F32)).astype(BF16)
    return hi, lo


def _dot(a, b):
    return jnp.dot(a, b, preferred_element_type=F32)


def _dot_nt(a, b):
    return lax.dot_general(a, b, (((1,), (1,)), ((), ())), preferred_element_type=F32)


def _dot_tn(a, b):
    return lax.dot_general(a, b, (((0,), (0,)), ((), ())), preferred_element_type=F32)


def _bmm(a, b):
    return jnp.stack([_dot(a[c], b[c]) for c in range(a.shape[0])])


def _bmm_nt(a, b):
    return jnp.stack([_dot_nt(a[c], b[c]) for c in range(a.shape[0])])


def _project(x, nw_ref, wm_ref, wsh_ref, wsl_ref, alog_ref, dtb_ref, cw_ref, tail_ref, grow_ref):
    tm = x.shape[0]
    h = x * lax.rsqrt(jnp.mean(x * x, axis=-1, keepdims=True) + EPS) * nw_ref[...]
    hb = h.astype(BF16)
    blocks = [None] * (N_MAIN // QK)
    for j in (0, 3, 1, 4, 2, 5, 6, 7):
        blk = _dot(hb, wm_ref[:, j * QK:(j + 1) * QK])
        if j < 3:
            ext = jnp.concatenate([tail_ref[j], blk], axis=0)
            tail_ref[j] = blk[tm - SUBLANES:tm, :]
            w = cw_ref[:, j * QK:(j + 1) * QK]
            y = w[CONV_K - 1:CONV_K, :] * blk
            for jj in range(CONV_K - 1):
                y = y + w[jj:jj + 1, :] * pltpu.roll(ext, CONV_K - 1 - jj, axis=0)[SUBLANES:, :]
            blk = y * _sigmoid(y)
        if j < 2:
            scale = DH ** -0.5 if j == 0 else 1.0
            blk = jnp.concatenate(
                [blk[:, hh * DH:(hh + 1) * DH]
                 * (lax.rsqrt(jnp.sum(blk[:, hh * DH:(hh + 1) * DH] ** 2, axis=-1, keepdims=True) + EPS) * scale)
                 for hh in range(HEADS)], axis=1)
        blocks[j] = blk
    sm = _dot(hb, wsh_ref[...]) + _dot(hb, wsl_ref[...])
    lane = lax.broadcasted_iota(I32, (tm, LANES), 1)
    row = lax.broadcasted_iota(I32, (tm, LANES), 0)
    beta = _sigmoid(sm)
    g = -jnp.exp(alog_ref[...]) * _softplus(sm + dtb_ref[...])
    rin = row & (CHUNK - 1)
    gc = g
    d = 1
    while d < CHUNK:
        gc = gc + jnp.where(rin >= d, pltpu.roll(gc, d, axis=0), 0.0)
        d *= 2
    out = jnp.where(lane < HEADS, beta, jnp.where(lane < 2 * HEADS, gc, 0.0))
    tr = out.T
    for hh in range(HEADS):
        for c in range(tm // CHUNK):
            grow_ref[hh, c, 0:1, :] = tr[hh:hh + 1, c * CHUNK:(c + 1) * CHUNK]
            grow_ref[hh, c, 1:2, :] = tr[HEADS + hh:HEADS + hh + 1, c * CHUNK:(c + 1) * CHUNK]
    return blocks, out


def _gated_norm(o, gate, nw):
    outs = []
    for h in range(HEADS):
        sl = slice(h * DH, (h + 1) * DH)
        oh = o[:, sl]
        wh = nw if nw.shape[1] == DH else nw[:, sl]
        oh = oh * lax.rsqrt(jnp.mean(oh * oh, axis=-1, keepdims=True) + EPS) * wh
        outs.append(oh * (gate[:, sl] * _sigmoid(gate[:, sl])))
    return jnp.concatenate(outs, axis=1)


def _gdn_heads(q, k, v, gblk, grow, st_ref):
    ts = q.shape[0]
    ncl = ts // CHUNK
    nc = HEADS * ncl

    def per_head(f):
        return jnp.concatenate([f(h, slice(h * DH, (h + 1) * DH)) for h in range(HEADS)], axis=0)

    lane = lax.broadcasted_iota(I32, (ts, LANES), 1)
    q3 = per_head(lambda h, sl: q[:, sl].reshape(ncl, CHUNK, DH))
    k3 = per_head(lambda h, sl: k[:, sl].reshape(ncl, CHUNK, DH))
    v3 = per_head(lambda h, sl: v[:, sl].reshape(ncl, CHUNK, DH))
    bc = per_head(lambda h, sl: jnp.sum(jnp.where(lane == h, gblk, 0.0), axis=-1, keepdims=True
                                        ).reshape(ncl, CHUNK, 1))
    gcc = per_head(lambda h, sl: jnp.sum(jnp.where(lane == h + HEADS, gblk, 0.0), axis=-1, keepdims=True
                                         ).reshape(ncl, CHUNK, 1))
    gr = grow.reshape(nc, 2, CHUNK)
    beta_r = gr[:, 0:1, :]
    gc_r = gr[:, 1:2, :]
    gl = gcc[:, CHUNK - 1:CHUNK, :]

    ii = lax.broadcasted_iota(I32, (nc, CHUNK, CHUNK), 1)
    jj = lax.broadcasted_iota(I32, (nc, CHUNK, CHUNK), 2)
    causal = ii >= jj
    strict = ii > jj
    dec = jnp.where(causal, jnp.exp(jnp.where(causal, gcc - gc_r, 0.0)), 0.0)

    k16 = k3.astype(BF16)
    q16 = q3.astype(BF16)
    v16 = v3.astype(BF16)
    gram = _bmm_nt(k16, k16)
    qk = _bmm_nt(q16, k16)
    a = jnp.where(strict, gram * dec * bc, 0.0)

    eye = (ii == jj).astype(F32)
    dm = eye - jnp.where((ii >> 1) == (jj >> 1), a, 0.0)
    sblk = 2
    while sblk < CHUNK:
        sh = sblk.bit_length() - 1
        off = jnp.where(((ii >> (sh + 1)) == (jj >> (sh + 1))) & ((ii >> sh) != (jj >> sh)), a, 0.0)
        d16 = dm.astype(BF16)
        t1 = _bmm(d16, off.astype(BF16))
        dm = dm - _bmm(t1.astype(BF16), d16)
        sblk *= 2

    tb = dm * beta_r
    tbe = tb * jnp.exp(gc_r)
    u = _bmm(tb.astype(BF16), v16)
    w = _bmm(tbe.astype(BF16), k16)
    attn = qk * dec
    wu = jnp.concatenate([w, u], axis=-1)
    wu16 = wu.astype(BF16)
    aw = _bmm(attn.astype(BF16), wu16)
    qe = q3 * jnp.exp(gcc) - aw[:, :, 0:DH]
    ol = aw[:, :, DH:2 * DH]
    kdec = k3 * jnp.exp(gl - gcc)

    trow = lax.broadcasted_iota(I32, (2 * CHUNK, DH), 0)
    xn = []
    for p in range(nc // 2):
        kd2 = kdec[2 * p:2 * p + 2].reshape(2 * CHUNK, DH)
        wu2 = wu16[2 * p:2 * p + 2].reshape(2 * CHUNK, 2 * DH)
        xn.append(_dot_tn(jnp.where(trow < CHUNK, kd2, 0.0).astype(BF16), wu2))
        xn.append(_dot_tn(jnp.where(trow >= CHUNK, kd2, 0.0).astype(BF16), wu2))

    sts = [st_ref[h] for h in range(HEADS)]
    outs = [[] for _ in range(HEADS)]
    for c in range(ncl):
        for h in range(HEADS):
            i = h * ncl + c
            s16 = sts[h].astype(BF16)
            outs[h].append(_dot(qe[i].astype(BF16), s16) + ol[i])
            sts[h] = jnp.exp(gl[i]) * sts[h] + (xn[i][:, DH:2 * DH] - _dot(xn[i][:, 0:DH].astype(BF16), s16))
    for h in range(HEADS):
        st_ref[h] = sts[h]
    return jnp.concatenate([jnp.concatenate(outs[h], axis=0) for h in range(HEADS)], axis=1)


def _ret_decay():
    c = CHUNK_RET
    ii = lax.broadcasted_iota(I32, (c, c), 0)
    jj = lax.broadcasted_iota(I32, (c, c), 1)
    rel = (ii - jj).astype(F32)
    return jnp.stack([jnp.where(rel >= 0, jnp.exp(jnp.maximum(rel, 0.0) * _LOG_GAMMA[h]), 0.0)
                      for h in range(HEADS)])


def _ret_heads(qa, ka, va, cosf, sinf, dec_ref, st_ref):
    rows = qa.shape[0]
    c = CHUNK_RET
    pos = lax.broadcasted_iota(I32, (c, 1), 0).astype(F32)
    decs = [dec_ref[h] for h in range(HEADS)]
    sts = [st_ref[h] for h in range(HEADS)]
    outs = [[] for _ in range(HEADS)]
    for ck in range(rows // c):
        rs = slice(ck * c, (ck + 1) * c)
        for h in range(HEADS):
            sl = slice(h * DH, (h + 1) * DH)
            lg = _LOG_GAMMA[h]
            q = qa[rs, sl] * cosf[rs] + pltpu.roll(qa[rs, sl], DH // 2, axis=1) * sinf[rs]
            k = (ka[rs, sl] * cosf[rs] + pltpu.roll(ka[rs, sl], DH // 2, axis=1) * sinf[rs]) * (DH ** -0.5)
            v16 = va[rs, sl]
            k16 = k.astype(BF16)
            attn = _dot_nt(q.astype(BF16), k16) * decs[h]
            inner = _dot(attn.astype(BF16), v16)
            qd = q * jnp.exp((pos + 1.0) * lg)
            cross = _dot(qd.astype(BF16), sts[h].astype(BF16))
            kd = k * jnp.exp((c - 1.0 - pos) * lg)
            sts[h] = sts[h] * math.exp(c * lg) + _dot_tn(kd.astype(BF16), v16)
            outs[h].append(cross + inner)
    for h in range(HEADS):
        st_ref[h] = sts[h]
    return jnp.concatenate([jnp.concatenate(outs[h], axis=0) for h in range(HEADS)], axis=1)


def _route(x, o16, wo_ref, nw, wrh_ref, wrl_ref, lstrict, carry):
    tm = x.shape[0]
    x1 = x + _dot(o16, wo_ref[...])
    h2 = x1 * lax.rsqrt(jnp.mean(x1 * x1, axis=-1, keepdims=True) + EPS) * nw
    hh = h2.astype(BF16)
    lg = _dot(hh, wrh_ref[...]) + _dot(hh, wrl_ref[...])
    lane = lax.broadcasted_iota(I32, (tm, LANES), 1)
    big = jnp.int32(1 << 20)
    neg = jnp.float32(-jnp.inf)

    def argmax_first(vals, mask):
        mv = jnp.where(mask, vals, neg)
        m = jnp.max(mv, axis=-1, keepdims=True)
        idx = jnp.min(jnp.where(mask & (mv == m), lane, big), axis=-1, keepdims=True)
        return m, idx

    gmask = lane < N_GROUPS
    gm, gi = argmax_first(lg, gmask)
    gw = 1.0 / jnp.sum(jnp.where(gmask, jnp.exp(lg - gm), 0.0), axis=-1, keepdims=True)
    emask = (lane >= N_EXPERTS) & ((lane >> 3) == gi + N_EXPERTS // EPG)
    m1, i1 = argmax_first(lg, emask)
    m2, i2 = argmax_first(lg, emask & (lane != i1))
    e21 = jnp.exp(m2 - m1)
    w0 = gw / (1.0 + e21)
    w1 = gw * e21 / (1.0 + e21)
    e0 = i1 - N_EXPERTS
    e1 = i2 - N_EXPERTS

    oh = ((lane == e0) | (lane == e1)).astype(BF16)
    pref = _dot(lstrict, oh) + carry
    r0 = jnp.sum(jnp.where(lane == e0, pref, 0.0), axis=-1, keepdims=True)
    r1 = jnp.sum(jnp.where(lane == e1, pref, 0.0), axis=-1, keepdims=True)
    carry = carry + jnp.sum(oh.astype(F32), axis=0, keepdims=True)

    rmat = jnp.where(lane == 0, e0.astype(F32) * RANK_RADIX + r0,
                     jnp.where(lane == 1, e1.astype(F32) * RANK_RADIX + r1,
                               jnp.where(lane == 2, w0, jnp.where(lane == 3, w1, 0.0))))
    return x1, h2, rmat.T[0:SUBLANES, :], carry


def _finish_slots(rr_ref, carry):
    lane8 = lax.broadcasted_iota(I32, (SUBLANES, LANES), 1)
    cnt8 = jnp.broadcast_to(carry, (SUBLANES, LANES))
    inc = cnt8
    d = 1
    while d < LANES:
        inc = inc + jnp.where(lane8 >= d, pltpu.roll(inc, d, axis=1), 0.0)
        d *= 2
    base = inc - cnt8
    pk = rr_ref[:, 0:TOP_K, :]
    ex = jnp.floor(pk * (1.0 / RANK_RADIX))
    slot = pk - ex * RANK_RADIX
    for kk in range(N_EXPERTS):
        bk = jnp.sum(jnp.where(lane8 == kk, base, 0.0), axis=-1, keepdims=True)[0:1, :]
        slot = slot + jnp.where(ex == kk, bk, 0.0)
    rr_ref[:, 0:TOP_K, :] = slot * SUBLANES


def _front_kernel(xc_ref, x_ref, anw_ref, wm_ref, wsh_ref, wsl_ref, alog_ref, dtb_ref, cw_ref,
                  wg_ref, wu_ref, wd_ref, na_ref, cos_ref, sin_ref, nb_ref, wo_ref, fw_ref, wrh_ref, wrl_ref,
                  x1t_ref, h2t_ref, rr_ref, cnt_ref, wg16_ref, wu16_ref, wd16_ref,
                  tail_ref, grow_ref, sta_ref, stb_ref, oprev_ref, carry_ref, dec_ref, ls_ref,
                  *, tiles_per_seq, per):
    i = pl.program_id(0)
    ts = x_ref.shape[0]
    tm = TM_MIX
    prev = i + per - 1
    wg16_ref[...] = wg_ref[...].astype(BF16)
    wu16_ref[...] = wu_ref[...].astype(BF16)
    wd16_ref[...] = wd_ref[...].astype(BF16)

    @pl.when(i == 0)
    def _first():
        oprev_ref[...] = jnp.zeros(oprev_ref.shape, oprev_ref.dtype)
        dec_ref[...] = _ret_decay()
        ri = lax.broadcasted_iota(I32, (tm, tm), 0)
        ci = lax.broadcasted_iota(I32, (tm, tm), 1)
        ls_ref[...] = (ri > ci).astype(BF16)

    @pl.when(i % tiles_per_seq == 0)
    def _seq_start():
        tail_ref[...] = jnp.zeros(tail_ref.shape, F32)
        sta_ref[...] = jnp.zeros(sta_ref.shape, F32)
        stb_ref[...] = jnp.zeros(stb_ref.shape, F32)

    @pl.when((i == 0) | (prev % per == 0))
    def _moe_tile_start():
        carry_ref[...] = jnp.zeros((1, LANES), F32)

    (qa, ka, va, ga, qb, kb, vb, gb), gcol = _project(
        xc_ref[...], anw_ref, wm_ref, wsh_ref, wsl_ref, alog_ref, dtb_ref, cw_ref, tail_ref, grow_ref)

    carry = carry_ref[...]
    nw = fw_ref[...]
    for hf in range(ts // tm):
        rows = slice(hf * tm, (hf + 1) * tm)
        x1, h2, rmt, carry = _route(x_ref[rows, :], oprev_ref[rows, :], wo_ref, nw, wrh_ref, wrl_ref,
                                    ls_ref[...], carry)
        for s in range(SUBLANES):
            x1t_ref[pl.ds(hf * tm * SUBLANES + s, tm, stride=SUBLANES), :] = x1[:, s * LANES:(s + 1) * LANES]
            h2t_ref[pl.ds(hf * tm * SUBLANES + s, tm, stride=SUBLANES), :] = h2[:, s * LANES:(s + 1) * LANES]
        rr_ref[(prev % per) * (ts // tm) + hf] = rmt
    carry_ref[...] = carry
    cnt_ref[...] = jnp.broadcast_to(carry, (SUBLANES, LANES)).astype(I32)

    oa = _gdn_heads(qa, ka, va, gcol, grow_ref[...], sta_ref)
    ob = _ret_heads(qb, kb, vb.astype(BF16), cos_ref[...], sin_ref[...], dec_ref, stb_ref)
    oa = _gated_norm(oa, ga, na_ref[...])
    ob = _gated_norm(ob, gb, nb_ref[...])
    oprev_ref[...] = jnp.concatenate([oa, ob], axis=1).astype(oprev_ref.dtype)

    @pl.when((i > 0) & (prev % per == per - 1))
    def _moe_tile_end():
        _finish_slots(rr_ref, carry)


def _front(x2, attn_norm, w_main, ws_hi, ws_lo, alog_l, dtb_l, conv_a, w_gate, w_up, w_down,
           norm_a, cosf, sinf, norm_b, w_out16, ffn_norm, wr_hi, wr_lo, seq):
    t = x2.shape[0]
    ts = TS_FRONT
    n = t // ts
    nc = ts // CHUNK
    tiles_per_seq = seq // ts
    per = TS_MOE // ts
    assert N_EXPERTS % n == 0, "each tile step casts an equal share of the expert weights"
    epb = N_EXPERTS // n

    def cur(i):
        return jnp.minimum(i, n - 1)

    def prv(i):
        return jnp.maximum(i - 1, 0)

    def const(shape):
        return pl.BlockSpec(shape, lambda i: (0,) * len(shape))

    def wspec(shape):
        return pl.BlockSpec((epb,) + shape, lambda i: (cur(i), 0, 0))

    def wshape(shape):
        return jax.ShapeDtypeStruct((N_EXPERTS,) + shape, BF16)

    return pl.pallas_call(
        functools.partial(_front_kernel, tiles_per_seq=tiles_per_seq, per=per),
        grid=(n + 1,),
        in_specs=[
            pl.BlockSpec((ts, D_MODEL), lambda i: (cur(i), 0)),
            pl.BlockSpec((ts, D_MODEL), lambda i: (prv(i), 0)),
            const((1, D_MODEL)), const((D_MODEL, N_MAIN)), const((D_MODEL, LANES)), const((D_MODEL, LANES)),
            const((1, LANES)), const((1, LANES)), const((CONV_K, 3 * QK)),
            wspec((D_MODEL, D_EXPERT)), wspec((D_MODEL, D_EXPERT)), wspec((D_EXPERT, D_MODEL)),
            const((1, DH)),
            pl.BlockSpec((ts, DH), lambda i: (cur(i) % tiles_per_seq, 0)),
            pl.BlockSpec((ts, DH), lambda i: (cur(i) % tiles_per_seq, 0)),
            const((1, QK)),
            const((D_MODEL, D_MODEL)), const((1, D_MODEL)), const((D_MODEL, LANES)), const((D_MODEL, LANES)),
        ],
        out_specs=[
            pl.BlockSpec((ts * SUBLANES, LANES), lambda i: (prv(i), 0)),
            pl.BlockSpec((ts * SUBLANES, LANES), lambda i: (prv(i), 0)),
            pl.BlockSpec((TS_MOE // TM_MIX, SUBLANES, TM_MIX), lambda i: (prv(i) // per, 0, 0)),
            pl.BlockSpec((None, SUBLANES, LANES), lambda i: (prv(i) // per, 0, 0)),
            wspec((D_MODEL, D_EXPERT)), wspec((D_MODEL, D_EXPERT)), wspec((D_EXPERT, D_MODEL)),
        ],
        out_shape=[
            jax.ShapeDtypeStruct((t * SUBLANES, LANES), F32),
            jax.ShapeDtypeStruct((t * SUBLANES, LANES), F32),
            jax.ShapeDtypeStruct((t // TM_MIX, SUBLANES, TM_MIX), F32),
            jax.ShapeDtypeStruct((t // TS_MOE, SUBLANES, LANES), I32),
            wshape((D_MODEL, D_EXPERT)), wshape((D_MODEL, D_EXPERT)), wshape((D_EXPERT, D_MODEL)),
        ],
        scratch_shapes=[
            pltpu.VMEM((3, SUBLANES, QK), F32),
            pltpu.VMEM((HEADS, nc, 2, CHUNK), F32),
            pltpu.VMEM((HEADS, DH, DH), F32),
            pltpu.VMEM((HEADS, DH, DH), F32),
            pltpu.VMEM((ts, D_MODEL), BF16),
            pltpu.VMEM((1, LANES), F32),
            pltpu.VMEM((HEADS, CHUNK_RET, CHUNK_RET), F32),
            pltpu.VMEM((TM_MIX, TM_MIX), BF16),
        ],
        compiler_params=pltpu.CompilerParams(
            dimension_semantics=("arbitrary",), vmem_limit_bytes=VMEM_LIMIT_FRONT),
        name="front",
    )(x2, x2, attn_norm, w_main, ws_hi, ws_lo, alog_l, dtb_l, conv_a, w_gate, w_up, w_down,
      norm_a, cosf, sinf, norm_b, w_out16, ffn_norm, wr_hi, wr_lo)


def _moe_kernel(p0_ref, p1_ref, w0_ref, w1_ref, cnt_ref,
                h2t_hbm, x1t_hbm, wg_ref, wu_ref, wd_ref, fnw_ref,
                out_hbm, stag_ref, acc_ref, base_ref, ibuf_ref, obuf_ref, isem, osem):
    i = pl.program_id(0)
    j = pl.program_id(1)
    sub = SUB_MOE
    nsub = TS_MOE // sub
    npair = N_EXPERTS // EXPERTS_PER_STEP
    tok0 = i * TS_MOE

    def row8(r):
        return pl.ds(pl.multiple_of(r * SUBLANES, SUBLANES), SUBLANES)

    def at(r8):
        return pl.ds(pl.multiple_of(r8, SUBLANES), SUBLANES)

    def in_copy(src_hbm, k, slot):
        rows = pl.ds(pl.multiple_of((tok0 + k * sub) * SUBLANES, SUBLANES), sub * SUBLANES)
        return pltpu.make_async_copy(src_hbm.at[rows], ibuf_ref.at[slot], isem.at[slot])

    def out_copy(k, slot):
        rows = pl.ds(pl.multiple_of(tok0 + k * sub, SUBLANES), sub)
        return pltpu.make_async_copy(obuf_ref.at[slot], out_hbm.at[rows], osem.at[slot])

    @pl.when(j == 0)
    def _scatter():
        stag_ref[pl.ds(TOP_K * TS_MOE * SUBLANES, SLACK_MOE * SUBLANES), :] = jnp.zeros(
            (SLACK_MOE * SUBLANES, LANES), F32)

        def prefix(e, run):
            base_ref[e] = run
            return run + cnt_ref[e]

        lax.fori_loop(0, N_EXPERTS, prefix, jnp.int32(0))

        in_copy(h2t_hbm, 0, 0).start()
        for k in range(nsub):
            slot = k % 2
            if k + 1 < nsub:
                in_copy(h2t_hbm, k + 1, 1 - slot).start()
            in_copy(h2t_hbm, k, slot).wait()

            def body(t, carry, k=k, slot=slot):
                row = ibuf_ref[slot, row8(t), :]
                stag_ref[at(p0_ref[k * sub + t]), :] = row
                stag_ref[at(p1_ref[k * sub + t]), :] = row
                return carry

            lax.fori_loop(0, sub, body, 0, unroll=ROW_UNROLL_MOE)

    def load_rows(r0, m):
        xs = [stag_ref[pl.ds(r0 + s, m, stride=SUBLANES), :] for s in range(SUBLANES)]
        return xs, jnp.concatenate([v.astype(BF16) for v in xs], axis=1)

    def store_rows(r0, m, xs, y, nvalid):
        keep = lax.broadcasted_iota(I32, (m, LANES), 0) < nvalid
        for s in range(SUBLANES):
            stag_ref[pl.ds(r0 + s, m, stride=SUBLANES), :] = jnp.where(
                keep, y[:, s * LANES:(s + 1) * LANES], xs[s])

    qs = range(EXPERTS_PER_STEP)
    es = [j * EXPERTS_PER_STEP + q for q in qs]
    ns = [cnt_ref[e] for e in es]
    bs = [base_ref[e] for e in es]
    m = CH_FIRST_MOE
    r0s = [pl.multiple_of(b * SUBLANES, SUBLANES) for b in bs]
    ld = [load_rows(r0, m) for r0 in r0s]
    gu = [(_dot(ld[q][1], wg_ref[q]), _dot(ld[q][1], wu_ref[q])) for q in qs]
    hs = [((gu[q][0] * _sigmoid(gu[q][0])) * gu[q][1]).astype(BF16) for q in qs]
    ys = [_dot(hs[q], wd_ref[q]) for q in qs]
    for q in qs:
        store_rows(r0s[q], m, ld[q][0], ys[q], ns[q])
    ch = CH_MOE
    for q in qs:
        def chunk(c, carry, q=q):
            r0 = pl.multiple_of((bs[q] + m + c * ch) * SUBLANES, SUBLANES)
            xs, x16 = load_rows(r0, ch)
            g = _dot(x16, wg_ref[q])
            u = _dot(x16, wu_ref[q])
            y = _dot(((g * _sigmoid(g)) * u).astype(BF16), wd_ref[q])
            store_rows(r0, ch, xs, y, ns[q] - m - c * ch)
            return carry

        lax.fori_loop(0, (jnp.maximum(ns[q] - m, 0) + ch - 1) // ch, chunk, 0)

    @pl.when(j == npair - 1)
    def _combine():
        in_copy(x1t_hbm, 0, 0).start()
        for k in range(nsub):
            slot = k % 2
            if k + 1 < nsub:
                in_copy(x1t_hbm, k + 1, 1 - slot).start()
            in_copy(x1t_hbm, k, slot).wait()

            def body(t, carry, k=k, slot=slot):
                acc_ref[row8(t), :] = (ibuf_ref[slot, row8(t), :]
                                       + w0_ref[k * sub + t] * stag_ref[at(p0_ref[k * sub + t]), :]
                                       + w1_ref[k * sub + t] * stag_ref[at(p1_ref[k * sub + t]), :])
                return carry

            lax.fori_loop(0, sub, body, 0, unroll=ROW_UNROLL_MOE)
            xo = jnp.concatenate([acc_ref[pl.ds(s, sub, stride=SUBLANES), :] for s in range(SUBLANES)], axis=1)
            if k >= 2:
                out_copy(k - 2, slot).wait()
            obuf_ref[slot] = xo * lax.rsqrt(jnp.mean(xo * xo, axis=-1, keepdims=True) + EPS) * fnw_ref[...]
            out_copy(k, slot).start()
        for k in (nsub - 2, nsub - 1):
            out_copy(k, k % 2).wait()


def _moe(p0, p1, w0, w1, cnt, h2t, x1t, wg16, wu16, wd16, final_norm):
    t = p0.shape[0]
    sub = SUB_MOE
    npair = N_EXPERTS // EXPERTS_PER_STEP
    nt = t // TS_MOE
    smem = functools.partial(pl.BlockSpec, memory_space=pltpu.SMEM)

    def wspec(shape):
        return pl.BlockSpec((EXPERTS_PER_STEP,) + shape, lambda i, j: (j, 0, 0))

    return pl.pallas_call(
        _moe_kernel,
        grid=(nt, npair),
        in_specs=[
            smem((TS_MOE,), lambda i, j: (i,)), smem((TS_MOE,), lambda i, j: (i,)),
            smem((TS_MOE,), lambda i, j: (i,)), smem((TS_MOE,), lambda i, j: (i,)),
            smem((LANES,), lambda i, j: (i,)),
            pl.BlockSpec(memory_space=pl.ANY),
            pl.BlockSpec(memory_space=pl.ANY),
            wspec((D_MODEL, D_EXPERT)), wspec((D_MODEL, D_EXPERT)), wspec((D_EXPERT, D_MODEL)),
            pl.BlockSpec((1, D_MODEL), lambda i, j: (0, 0)),
        ],
        out_specs=pl.BlockSpec(memory_space=pl.ANY),
        out_shape=jax.ShapeDtypeStruct((t, D_MODEL), F32),
        scratch_shapes=[
            pltpu.VMEM(((TOP_K * TS_MOE + SLACK_MOE) * SUBLANES, LANES), F32),
            pltpu.VMEM((sub * SUBLANES, LANES), F32),
            pltpu.SMEM((LANES,), I32),
            pltpu.VMEM((2, sub * SUBLANES, LANES), F32),
            pltpu.VMEM((2, sub, D_MODEL), F32),
            pltpu.SemaphoreType.DMA((2,)),
            pltpu.SemaphoreType.DMA((2,)),
        ],
        compiler_params=pltpu.CompilerParams(
            dimension_semantics=("arbitrary", "arbitrary"), vmem_limit_bytes=VMEM_LIMIT_MOE),
        name="moe",
    )(p0, p1, w0, w1, cnt, h2t, x1t, wg16, wu16, wd16, final_norm)


def _pad_lanes(a, start):
    rows, n = a.shape
    return jnp.pad(a, ((0, 0), (start, LANES - start - n)))


def kernel(x, attn_norm, w_in, conv_a, a_log, dt_bias, norm_a, norm_b, w_out, ffn_norm,
           w_router_group, w_router_expert, w_gate, w_up, w_down, final_norm):
    batch, seq, d = x.shape
    t = batch * seq
    x2 = x.reshape(t, d)
    assert attn_norm.shape[0] == 1, "the closing RMSNorm is fused into the single layer's MoE combine step"

    pos = jnp.arange(seq, dtype=F32)
    inv_freq = ROPE_BASE ** (-jnp.arange(0, DH, 2, dtype=F32) / DH)
    ang = pos[:, None] * inv_freq[None, :]
    cos, sin = jnp.cos(ang), jnp.sin(ang)
    cosf = jnp.concatenate([cos, cos], axis=-1)
    sinf = jnp.concatenate([-sin, sin], axis=-1)

    wl = w_in[0]
    w_main = jnp.concatenate([wl[:, 0:4 * QK], wl[:, 4 * QK + 2 * HEADS:]], axis=1).astype(BF16)
    ws_hi, ws_lo = _split_bf16(_pad_lanes(wl[:, 4 * QK:4 * QK + 2 * HEADS], 0))
    alog_l = _pad_lanes(a_log[0][None, :], HEADS)
    dtb_l = _pad_lanes(dt_bias[0][None, :], HEADS)
    wr = jnp.concatenate(
        [_pad_lanes(w_router_group[0], 0)[:, 0:N_EXPERTS], w_router_expert[0]], axis=1)
    wr_hi, wr_lo = _split_bf16(wr)
    x1t, h2t, rr, cnt, wg16, wu16, wd16 = _front(
        x2, attn_norm[0][None, :], w_main, ws_hi, ws_lo, alog_l, dtb_l, conv_a[0], w_gate[0], w_up[0], w_down[0],
        norm_a[0][None, :], cosf, sinf, norm_b[0][None, :], w_out[0].astype(BF16), ffn_norm[0][None, :],
        wr_hi, wr_lo, seq)

    p0 = rr[:, 0, :].reshape(-1).astype(I32)
    p1 = rr[:, 1, :].reshape(-1).astype(I32)
    out = _moe(p0, p1, rr[:, 2, :].reshape(-1), rr[:, 3, :].reshape(-1), cnt[:, 0, :].reshape(-1), h2t, x1t,
               wg16, wu16, wd16, final_norm[None, :])
    return out.reshape(batch, seq, d)
```
